```python
import math
import jax
import jax.numpy as jnp
from jax import lax
import numpy as np

D_MODEL = 1024
BATCH = 8
SEQ = 4096
DEPTH = 2

HEAD_DIM = 64
SCALE = HEAD_DIM ** -0.5
NEG_INF = -1e30
EPS = 1e-6
A_GROUPS = ((128, 1), (512, 4), (2048, 16))
A_HEADS_PER_GROUP = 4
A_HEADS = A_HEADS_PER_GROUP * len(A_GROUPS)
A_WIDTH = A_HEADS * HEAD_DIM
A_OUT = A_HEADS_PER_GROUP * HEAD_DIM
A_BLOCK = max(w // d for w, d in A_GROUPS)
LRU_WIDTH = D_MODEL // 2
LRU_BLOCKS = 8
LRU_BLOCK_DIM = LRU_WIDTH // LRU_BLOCKS
CONV_WIDTH = 4
LRU_C = 8.0
C_HEADS = 8
C_WIDTH = C_HEADS * HEAD_DIM
MOBA_BLOCK = 256
MOBA_TOPK = 3
MOBA_Q_CHUNK = 32
REL_BUCKETS = 32
REL_MAX_DIST = 2048
REL_HEADS = A_HEADS + C_HEADS
FFN_HIDDEN = -(-8 * D_MODEL // (3 * 256)) * 256
IN_SIZES = (A_WIDTH, A_WIDTH, A_WIDTH, LRU_WIDTH, LRU_WIDTH, C_WIDTH, C_WIDTH, C_WIDTH, 3 * D_MODEL)
IN_COLS = sum(IN_SIZES)

kernel_name = "hybrid_dilated_rglru_moba_block"


def rms_norm(x, g):
    xf = x.astype(jnp.float32)
    y = xf * lax.rsqrt(jnp.mean(xf * xf, axis=-1, keepdims=True) + EPS)
    return (y * g.astype(jnp.float32)).astype(x.dtype)


def rel_bucket(dist):
    max_exact = REL_BUCKETS // 2
    d = jnp.maximum(dist, 0)
    df = jnp.maximum(d, 1).astype(jnp.float32)
    large = max_exact + (jnp.log(df / max_exact) / math.log(REL_MAX_DIST / max_exact)
                         * (REL_BUCKETS - max_exact)).astype(jnp.int32)
    large = jnp.minimum(large, REL_BUCKETS - 1)
    return jnp.where(d < max_exact, d, large)


def dilated_window_attention(q, k, v, dilation, n_back, bias_tab):
    B, S, H, E = q.shape
    L = S // dilation
    nb = -(-L // A_BLOCK)
    Lp = nb * A_BLOCK

    def to_sub(t):
        t = t.reshape(B, L, dilation, H, E).transpose(0, 2, 3, 1, 4)
        t = jnp.pad(t, ((0, 0), (0, 0), (0, 0), (0, Lp - L), (0, 0)))
        return t.reshape(B, dilation, H, nb, A_BLOCK, E)

    qb, kb, vb = to_sub(q), to_sub(k), to_sub(v)

    def with_prev(t):
        prev = jnp.pad(t, ((0, 0), (0, 0), (0, 0), (1, 0), (0, 0), (0, 0)))[:, :, :, :nb]
        return jnp.concatenate([prev, t], axis=4)

    kk, vv = with_prev(kb), with_prev(vb)
    qi = jnp.arange(A_BLOCK)[:, None] + A_BLOCK
    kj = jnp.arange(2 * A_BLOCK)[None, :]
    delta = qi - kj
    band = (delta >= 0) & (delta <= n_back)
    has_prev = (jnp.arange(nb)[:, None, None] > 0) | (kj >= A_BLOCK)[None]
    valid = band[None] & has_prev
    bias = bias_tab.astype(jnp.float32)[rel_bucket(delta * dilation)]
    bias = jnp.transpose(bias, (2, 0, 1))
    logits = jnp.einsum('bdhnqe,bdhnke->bdhnqk', qb, kk).astype(jnp.float32) * SCALE
    logits = logits + bias[None, None, :, None]
    logits = jnp.where(valid[None, None, None], logits, NEG_INF)
    m = jnp.max(logits, axis=-1, keepdims=True)
    p = jnp.exp(logits - m)
    den = jnp.sum(p, axis=-1, keepdims=True)
    o = jnp.einsum('bdhnqk,bdhnke->bdhnqe', p, vv.astype(jnp.float32)) / den
    lse = (m + jnp.log(den))[..., 0]
    o = o.reshape(B, dilation, H, Lp, E)[:, :, :, :L].transpose(0, 3, 1, 2, 4).reshape(B, S, H, E)
    lse = lse.reshape(B, dilation, H, Lp)[:, :, :, :L].transpose(0, 3, 1, 2).reshape(B, S, H)
    return o, lse


def rg_lru_branch(xb, gb, conv_w, conv_b, w_a, b_a, w_x, b_x, lam):
    B, S, W = xb.shape
    xp = jnp.pad(xb, ((0, 0), (CONV_WIDTH - 1, 0), (0, 0)))
    xc = conv_b
    for i in range(CONV_WIDTH):
        xc = xc + xp[:, i:i + S] * conv_w[i]
    xr = xc.reshape(B, S, LRU_BLOCKS, LRU_BLOCK_DIM)
    r = jax.nn.sigmoid((jnp.einsum('bsnd,nde->bsne', xr, w_a).reshape(B, S, W) + b_a).astype(jnp.float32))
    ig = jax.nn.sigmoid((jnp.einsum('bsnd,nde->bsne', xr, w_x).reshape(B, S, W) + b_x).astype(jnp.float32))
    log_a = -LRU_C * r * jax.nn.softplus(-lam.astype(jnp.float32))
    a = jnp.exp(log_a)
    bterm = jnp.sqrt(-jnp.expm1(2.0 * log_a)) * (ig * xc.astype(jnp.float32))

    def combine(left, right):
        a1, b1 = left
        a2, b2 = right
        return a1 * a2, a2 * b1 + b2

    _, h = lax.associative_scan(combine, (a, bterm), axis=1)
    return (h * jax.nn.gelu(gb.astype(jnp.float32))).astype(xb.dtype)


def moba_attention(q, k, v, bias_tab):
    B, S, H, E = q.shape
    f32 = jnp.float32
    nblk = -(-S // MOBA_BLOCK)
    Sp = nblk * MOBA_BLOCK
    pad = ((0, 0), (0, 0), (0, Sp - S), (0, 0))
    qh = q.transpose(0, 2, 1, 3)
    qb = jnp.pad(qh, pad).reshape(B, H, nblk, MOBA_BLOCK, E)
    kb = jnp.pad(k.transpose(0, 2, 1, 3), pad).reshape(B, H, nblk, MOBA_BLOCK, E)
    vb = jnp.pad(v.transpose(0, 2, 1, 3), pad).reshape(B, H, nblk, MOBA_BLOCK, E)
    tab = bias_tab.astype(f32)
    off = jnp.arange(MOBA_BLOCK)
    delta = off[:, None] - off[None, :]
    bias_own = jnp.transpose(tab[rel_bucket(delta)], (2, 0, 1))
    lg = jnp.einsum('bhnqe,bhnke->bhnqk', qb, kb).astype(f32) * SCALE + bias_own[None, :, None]
    lg = jnp.where(delta >= 0, lg, NEG_INF)
    m_own = jnp.max(lg, axis=-1)
    p = jnp.exp(lg - m_own[..., None])
    den_own = jnp.sum(p, axis=-1).reshape(B, H, Sp)[:, :, :S]
    num_own = jnp.einsum('bhnqk,bhnke->bhnqe', p, vb.astype(f32)).reshape(B, H, Sp, E)[:, :, :S]
    m_own = m_own.reshape(B, H, Sp)[:, :, :S]
    n_sel = min(MOBA_TOPK, nblk - 1)
    if n_sel == 0:
        out = num_own / den_own[..., None]
    else:
        kmean = jnp.mean(kb.astype(f32), axis=3)
        qpos = jnp.arange(S)
        qblk = qpos // MOBA_BLOCK
        gate = jnp.einsum('bhse,bhne->bhsn', qh.astype(f32), kmean)
        gate = jnp.where(jnp.arange(nblk)[None, :] < qblk[:, None], gate, NEG_INF)
        _, idx = lax.top_k(gate, n_sel)
        base = (jnp.arange(B)[:, None] * H + jnp.arange(H)[None, :]) * nblk
        fidx = idx + base[:, :, None, None]
        kflat = kb.reshape(B * H * nblk, MOBA_BLOCK, E)
        vflat = vb.reshape(B * H * nblk, MOBA_BLOCK, E)
        head = jnp.arange(H)[None, :, None, None, None]
        tab_t = tab.T
        nq = S // MOBA_Q_CHUNK

        def chunks(t):
            return jnp.moveaxis(t.reshape(B, H, nq, MOBA_Q_CHUNK, *t.shape[3:]), 2, 0)

        def step(args):
            qc, fi, bi, qp, m_o, d_o, n_o = args
            ks = jnp.take(kflat, fi, axis=0)
            vs = jnp.take(vflat, fi, axis=0)
            lgs = jnp.einsum('bhqe,bhqnke->bhqnk', qc, ks).astype(f32) * SCALE
            kpos = bi[..., None] * MOBA_BLOCK + jnp.arange(MOBA_BLOCK)
            lgs = lgs + tab_t[head, rel_bucket(qp[:, None, None] - kpos)]
            ok = bi < (qp // MOBA_BLOCK)[:, None]
            lgs = jnp.where(ok[..., None], lgs, NEG_INF)
            mm = jnp.maximum(jnp.max(lgs, axis=(-2, -1)), m_o)
            ps = jnp.exp(lgs - mm[..., None, None])
            c_o = jnp.exp(m_o - mm)
            den = d_o * c_o + jnp.sum(ps, axis=(-2, -1))
            num = n_o * c_o[..., None] + jnp.einsum('bhqnk,bhqnke->bhqe', ps, vs.astype(f32))
            return num / den[..., None]

        out = lax.map(step, (chunks(qh), chunks(fidx), chunks(idx), qpos.reshape(nq, MOBA_Q_CHUNK),
                             chunks(m_own), chunks(den_own), chunks(num_own)))
        out = jnp.moveaxis(out, 0, 2).reshape(B, H, S, E)
    return out.transpose(0, 2, 1, 3)


def hybrid_layer(x, rel_bias, g_mix, w_in, conv_w, conv_b, lru_wa, lru_ba, lru_wx, lru_bx, lru_lam,
                 p_a, p_b, p_c, w_out, g_ffn, w_gu, w_down):
    B, S, D = x.shape
    h = rms_norm(x, g_mix)
    z = h @ w_in
    offs = np.cumsum(IN_SIZES)[:-1].tolist()
    qa, ka, va, xb, gb, qc, kc, vc, gates = jnp.split(z, offs, axis=-1)
    qa = qa.reshape(B, S, A_HEADS, HEAD_DIM)
    ka = ka.reshape(B, S, A_HEADS, HEAD_DIM)
    va = va.reshape(B, S, A_HEADS, HEAD_DIM)
    outs, lses = [], []
    for g, (window, dilation) in enumerate(A_GROUPS):
        sl = slice(g * A_HEADS_PER_GROUP, (g + 1) * A_HEADS_PER_GROUP)
        o, lse = dilated_window_attention(qa[:, :, sl], ka[:, :, sl], va[:, :, sl],
                                          dilation, window // dilation, rel_bias[:, sl])
        outs.append(o)
        lses.append(lse)
    wts = jax.nn.softmax(jnp.stack(lses, axis=0), axis=0)
    o_a = jnp.sum(wts[..., None] * jnp.stack(outs, axis=0), axis=0).reshape(B, S, A_OUT).astype(x.dtype)
    o_b = rg_lru_branch(xb, gb, conv_w, conv_b, lru_wa, lru_ba, lru_wx, lru_bx, lru_lam)
    o_c = moba_attention(qc.reshape(B, S, C_HEADS, HEAD_DIM), kc.reshape(B, S, C_HEADS, HEAD_DIM),
                         vc.reshape(B, S, C_HEADS, HEAD_DIM), rel_bias[:, A_HEADS:])
    o_c = o_c.reshape(B, S, C_WIDTH).astype(x.dtype)
    gt = jax.nn.sigmoid(gates.astype(jnp.float32)).astype(x.dtype).reshape(B, S, 3, D)
    merged = gt[:, :, 0] * (o_a @ p_a) + gt[:, :, 1] * (o_b @ p_b) + gt[:, :, 2] * (o_c @ p_c)
    x = x + merged @ w_out
    u = rms_norm(x, g_ffn) @ w_gu
    gate, up = jnp.split(u, 2, axis=-1)
    return x + (jax.nn.silu(gate) * up) @ w_down


def setup_inputs(seed: int = 0) -> dict:
    key = jax.random.key(seed)
    ks = jax.random.split(key, 24)
    f32 = jnp.float32

    def nrm(k, shape, scale):
        return jax.random.normal(k, shape, f32) * scale

    u = jax.random.uniform(ks[11], (DEPTH, LRU_WIDTH), f32, 0.9, 0.999)
    a = u ** (1.0 / LRU_C)
    return {
        "x": nrm(ks[0], (BATCH, SEQ, D_MODEL), 1.0),
        "rel_bias": nrm(ks[1], (REL_BUCKETS, REL_HEADS), 0.5),
        "g_mix": 1.0 + nrm(ks[2], (DEPTH, D_MODEL), 0.05),
        "w_in": nrm(ks[3], (DEPTH, D_MODEL, IN_COLS), D_MODEL ** -0.5),
        "conv_w": nrm(ks[4], (DEPTH, CONV_WIDTH, LRU_WIDTH), CONV_WIDTH ** -0.5),
        "conv_b": nrm(ks[5], (DEPTH, LRU_WIDTH), 0.02),
        "lru_wa": nrm(ks[6], (DEPTH, LRU_BLOCKS, LRU_BLOCK_DIM, LRU_BLOCK_DIM), LRU_BLOCK_DIM ** -0.5),
        "lru_ba": nrm(ks[7], (DEPTH, LRU_WIDTH), 0.02),
        "lru_wx": nrm(ks[8], (DEPTH, LRU_BLOCKS, LRU_BLOCK_DIM, LRU_BLOCK_DIM), LRU_BLOCK_DIM ** -0.5),
        "lru_bx": nrm(ks[9], (DEPTH, LRU_WIDTH), 0.02),
        "lru_lam": jnp.log(a) - jnp.log1p(-a),
        "p_a": nrm(ks[12], (DEPTH, A_OUT, D_MODEL), A_OUT ** -0.5),
        "p_b": nrm(ks[13], (DEPTH, LRU_WIDTH, D_MODEL), LRU_WIDTH ** -0.5),
        "p_c": nrm(ks[14], (DEPTH, C_WIDTH, D_MODEL), C_WIDTH ** -0.5),
        "w_out": nrm(ks[15], (DEPTH, D_MODEL, D_MODEL), D_MODEL ** -0.5),
        "g_ffn": 1.0 + nrm(ks[16], (DEPTH, D_MODEL), 0.05),
        "w_gu": nrm(ks[17], (DEPTH, D_MODEL, 2 * FFN_HIDDEN), D_MODEL ** -0.5),
        "w_down": nrm(ks[18], (DEPTH, FFN_HIDDEN, D_MODEL), FFN_HIDDEN ** -0.5),
        "g_final": 1.0 + nrm(ks[19], (D_MODEL,), 0.05),
    }


def reference(x, rel_bias, g_mix, w_in, conv_w, conv_b, lru_wa, lru_ba, lru_wx, lru_bx, lru_lam,
              p_a, p_b, p_c, w_out, g_ffn, w_gu, w_down, g_final):
    for l in range(DEPTH):
        x = hybrid_layer(x, rel_bias, g_mix[l], w_in[l], conv_w[l], conv_b[l], lru_wa[l], lru_ba[l],
                         lru_wx[l], lru_bx[l], lru_lam[l], p_a[l], p_b[l], p_c[l], w_out[l],
                         g_ffn[l], w_gu[l], w_down[l])
    return rms_norm(x, g_final)
```

```python
import functools
import math

import numpy as np
import jax
import jax.numpy as jnp
from jax import lax
from jax.experimental import pallas as pl
from jax.experimental.pallas import tpu as pltpu

F32 = jnp.float32
BF16 = jnp.bfloat16

D_MODEL = 1024
HEAD_DIM = 64
SCALE = HEAD_DIM ** -0.5
NEG_INF = -1e30
EPS = 1e-6
A_GROUPS = ((128, 1), (512, 4), (2048, 16))
A_HPG = 4
A_HEADS = A_HPG * len(A_GROUPS)
A_WIDTH = A_HEADS * HEAD_DIM
A_OUT = A_HPG * HEAD_DIM
A_BLOCK = 128
LRU_WIDTH = D_MODEL // 2
LRU_BLOCKS = 8
CONV_WIDTH = 4
LRU_C = 8.0
C_HEADS = 8
C_WIDTH = C_HEADS * HEAD_DIM
MOBA_BLOCK = 256
MOBA_TOPK = 3
REL_BUCKETS = 32
REL_MAX_DIST = 2048
FFN_HIDDEN = 2816
IN_COLS = 3 * A_WIDTH + 2 * LRU_WIDTH + 3 * C_WIDTH + 3 * D_MODEL
V_C_OFF = 3 * A_WIDTH + 2 * LRU_WIDTH + 2 * C_WIDTH

LANE = 128
VMEM_LIMIT = 56 * 1024 * 1024

NT_DIMS = (((1,), (1,)), ((), ()))


def _cparams(sem, vmem=VMEM_LIMIT):
    return pltpu.CompilerParams(dimension_semantics=sem, vmem_limit_bytes=vmem)


def _const_spec(shape):
    nd = len(shape)
    return pl.BlockSpec(shape, lambda *_: (0,) * nd, pipeline_mode=pl.Buffered(1))


def _rel_bucket_np(dist):
    max_exact = REL_BUCKETS // 2
    d = np.maximum(dist, 0)
    df = np.maximum(d, 1).astype(np.float32)
    large = max_exact + (np.log(df / np.float32(max_exact)) / np.float32(math.log(REL_MAX_DIST / max_exact))
                         * np.float32(REL_BUCKETS - max_exact)).astype(np.int32)
    large = np.minimum(large, REL_BUCKETS - 1)
    return np.where(d < max_exact, d, large).astype(np.int32)


def _bucket_index_a():
    qi = np.arange(A_BLOCK)[:, None] + A_BLOCK
    kj = np.arange(2 * A_BLOCK)[None, :]
    delta = qi - kj
    mats = []
    for window, dil in A_GROUPS:
        band = (delta >= 0) & (delta <= window // dil)
        mats.append(np.where(band, _rel_bucket_np(delta * dil), -1))
    return np.stack(mats).astype(np.int32)


def _moba_far_blocks(nblk):
    for db in range(1, nblk + 1):
        lo = db * MOBA_BLOCK - (MOBA_BLOCK - 1)
        if np.all(_rel_bucket_np(np.arange(lo, nblk * MOBA_BLOCK)) == REL_BUCKETS - 1):
            return db
    return nblk


def _bucket_index_c(ndb):
    k = np.arange(MOBA_BLOCK)[:, None]
    q = np.arange(MOBA_BLOCK)[None, :]
    mats = []
    for db in range(ndb + 1):
        dist = db * MOBA_BLOCK + q - k
        b = _rel_bucket_np(dist)
        if db == 0:
            b = np.where(dist >= 0, b, -1)
        mats.append(b)
    return np.stack(mats).astype(np.int32)


def _bias_kernel(tab_ref, idx_ref, out_ref, *, head_off):
    h = pl.program_id(0) + head_off
    idx = idx_ref[0]
    out = jnp.full(idx.shape, NEG_INF, F32)
    for b in range(REL_BUCKETS):
        out = jnp.where(idx == b, tab_ref[b, h], out)
    out_ref[0, 0] = out


def _build_bias(rel_bias, idx, n_heads, head_off, heads_per_idx_group):
    n_mats = idx.shape[0] if heads_per_idx_group is None else 1
    r, c = idx.shape[-2:]
    if heads_per_idx_group is None:
        idx_map = lambda h, m: (m, 0, 0)
    else:
        idx_map = lambda h, m: (h // heads_per_idx_group, 0, 0)
    return pl.pallas_call(
        functools.partial(_bias_kernel, head_off=head_off),
        grid=(n_heads, n_mats),
        in_specs=[pl.BlockSpec(memory_space=pltpu.SMEM),
                  pl.BlockSpec((1, r, c), idx_map)],
        out_specs=pl.BlockSpec((1, 1, r, c), lambda h, m: (h, m, 0, 0)),
        out_shape=jax.ShapeDtypeStruct((n_heads, n_mats, r, c), F32),
        compiler_params=_cparams(("arbitrary", "arbitrary")),
        name="rel_bias_build",
    )(rel_bias, jnp.asarray(idx))


def _col_chunks(start, stop, width=512):
    out = []
    while start < stop:
        w = min(width, stop - start)
        out.append((start, w))
        start += w
    return out


def _in_proj_kernel(x_ref, g_ref, w_ref, wvt_ref, za_ref, zb_ref, zc_ref, vt_ref, zg_ref, h_scr, *, tm):
    x = x_ref[...]
    ms = jnp.mean(x * x, axis=-1, keepdims=True)
    h_scr[...] = (x * lax.rsqrt(ms + EPS) * g_ref[...]).astype(BF16)

    def seg(out_ref, base, width, scaled_cols, chunk=512):
        assert scaled_cols % chunk == 0
        for off, w in _col_chunks(0, width, chunk):
            r = jnp.dot(h_scr[...], w_ref[:, base + off:base + off + w], preferred_element_type=F32)
            if off < scaled_cols:
                r = r * SCALE
            out_ref[:, off:off + w] = r.astype(out_ref.dtype)

    wa, wb, wc = 3 * A_WIDTH, 2 * LRU_WIDTH, 2 * C_WIDTH
    seg(za_ref, 0, wa, A_WIDTH, A_WIDTH)
    seg(zb_ref, wa, wb, 0)
    seg(zc_ref, wa + wb, wc, C_WIDTH)
    seg(zg_ref, wa + wb + wc, 3 * D_MODEL, 0)
    for j in range(tm // MOBA_BLOCK):
        hj = h_scr[j * MOBA_BLOCK:(j + 1) * MOBA_BLOCK, :]
        vt_ref[0, j] = lax.dot_general(wvt_ref[...], hj, NT_DIMS, preferred_element_type=F32).astype(BF16)


def _in_proj(x2d, g, w_in, batch, seq, tm=512):
    n = x2d.shape[0]
    w_main = jnp.concatenate([w_in[:, :V_C_OFF], w_in[:, V_C_OFF + C_WIDTH:]], axis=1).astype(BF16)
    wvt = w_in[:, V_C_OFF:V_C_OFF + C_WIDTH].T.astype(BF16)
    wa, wb, wc, wg = 3 * A_WIDTH, 2 * LRU_WIDTH, 2 * C_WIDTH, 3 * D_MODEL
    tiles_per_seq = seq // tm
    bpt = tm // MOBA_BLOCK
    nblk = seq // MOBA_BLOCK
    row = lambda i: (i, 0)
    return pl.pallas_call(
        functools.partial(_in_proj_kernel, tm=tm),
        grid=(n // tm,),
        in_specs=[pl.BlockSpec((tm, D_MODEL), row),
                  _const_spec((1, D_MODEL)),
                  _const_spec(w_main.shape),
                  _const_spec(wvt.shape)],
        out_specs=[pl.BlockSpec((tm, wa), row),
                   pl.BlockSpec((tm, wb), row),
                   pl.BlockSpec((tm, wc), row),
                   pl.BlockSpec((1, bpt, C_WIDTH, MOBA_BLOCK),
                                lambda i: (i // tiles_per_seq, i % tiles_per_seq, 0, 0)),
                   pl.BlockSpec((tm, wg), row)],
        out_shape=[jax.ShapeDtypeStruct((n, wa), BF16),
                   jax.ShapeDtypeStruct((n, wb), BF16),
                   jax.ShapeDtypeStruct((n, wc), BF16),
                   jax.ShapeDtypeStruct((batch, nblk, C_WIDTH, MOBA_BLOCK), BF16),
                   jax.ShapeDtypeStruct((n, wg), BF16)],
        scratch_shapes=[pltpu.VMEM((tm, D_MODEL), BF16)],
        compiler_params=_cparams(("arbitrary",)),
        name="in_proj",
    )(x2d, g.reshape(1, D_MODEL), w_main, wvt)


def _mixer_a_kernel(q_ref, kc_ref, vc_ref, kp_ref, vp_ref, bias_ref, o_ref, lse_ref, *, nsub):
    t = pl.program_id(2)
    for j in range(nsub):
        rows = slice(j * A_BLOCK, (j + 1) * A_BLOCK)
        q = q_ref[0, rows, :]
        k_cur = kc_ref[0, rows, :]
        v_cur = vc_ref[0, rows, :]
        if j == 0:
            k_prev, v_prev = kp_ref[0], vp_ref[0]
        else:
            prev = slice((j - 1) * A_BLOCK, j * A_BLOCK)
            k_prev, v_prev = kc_ref[0, prev, :], vc_ref[0, prev, :]
        outs, lses = [], []
        for h in range(A_HPG):
            sl = slice(h * HEAD_DIM, (h + 1) * HEAD_DIM)
            qh = q[:, sl]
            s_p = lax.dot_general(qh, k_prev[:, sl], NT_DIMS, preferred_element_type=F32)
            s_c = lax.dot_general(qh, k_cur[:, sl], NT_DIMS, preferred_element_type=F32)
            s_p = s_p + bias_ref[h, 0, :, :A_BLOCK]
            s_c = s_c + bias_ref[h, 0, :, A_BLOCK:]
            if j == 0:
                s_p = jnp.where(t == 0, NEG_INF, s_p)
            m = jnp.maximum(jnp.max(s_p, axis=-1, keepdims=True), jnp.max(s_c, axis=-1, keepdims=True))
            p_p = jnp.exp(s_p - m)
            p_c = jnp.exp(s_c - m)
            den = jnp.sum(p_p, axis=-1, keepdims=True) + jnp.sum(p_c, axis=-1, keepdims=True)
            pv = (jnp.dot(p_p.astype(BF16), v_prev[:, sl], preferred_element_type=F32)
                  + jnp.dot(p_c.astype(BF16), v_cur[:, sl], preferred_element_type=F32))
            outs.append(pv / den)
            lses.append(jnp.broadcast_to(m + jnp.log(den), (A_BLOCK, HEAD_DIM)))
        o_ref[0, rows, :] = jnp.concatenate(outs, axis=-1).astype(o_ref.dtype)
        lse_ref[0, rows, :] = jnp.concatenate(lses, axis=-1)


def _mixer_a_group(za, bias_a, g, dil, batch, seq):
    sub_len = seq // dil
    assert sub_len % A_BLOCK == 0
    tq = min(sub_len, 512)
    nsub = tq // A_BLOCK
    wa = 3 * A_WIDTH
    ncol = wa // A_OUT
    z = za.reshape(batch, sub_len, dil * wa)
    qcol, kcol, vcol = g, A_WIDTH // A_OUT + g, 2 * A_WIDTH // A_OUT + g
    cur = lambda col: pl.BlockSpec((1, tq, A_OUT), lambda b, r, t: (b, t, r * ncol + col))
    prev = lambda col: pl.BlockSpec((1, A_BLOCK, A_OUT),
                                    lambda b, r, t: (b, jnp.maximum(t * nsub - 1, 0), r * ncol + col))
    out_spec = pl.BlockSpec((1, tq, A_OUT), lambda b, r, t: (b, t, r))
    o, lse = pl.pallas_call(
        functools.partial(_mixer_a_kernel, nsub=nsub),
        grid=(batch, dil, sub_len // tq),
        in_specs=[cur(qcol), cur(kcol), cur(vcol), prev(kcol), prev(vcol),
                  pl.BlockSpec((A_HPG, 1, A_BLOCK, 2 * A_BLOCK), lambda b, r, t: (g, 0, 0, 0))],
        out_specs=[out_spec, out_spec],
        out_shape=[jax.ShapeDtypeStruct((batch, sub_len, dil * A_OUT), BF16),
                   jax.ShapeDtypeStruct((batch, sub_len, dil * A_OUT), F32)],
        compiler_params=_cparams(("arbitrary", "arbitrary", "arbitrary")),
        name=f"mixer_a_d{dil}",
    )(z, z, z, z, z, bias_a)
    return o.reshape(batch * seq, A_OUT), lse.reshape(batch * seq, A_OUT)


SUBLANES = 8


def _mixer_b_kernel(z_ref, cw_ref, cb_ref, wax_ref, bax_ref, lam_ref, y_ref,
                    xs_scr, a_scr, b_scr, h_scr, *, ts):
    w = LRU_WIDTH

    @pl.when(pl.program_id(1) == 0)
    def _():
        xs_scr[0:SUBLANES, :] = jnp.zeros((SUBLANES, w), F32)
        h_scr[...] = jnp.zeros((1, w), F32)

    x = z_ref[0, :, 0:w].astype(F32)
    xs_scr[SUBLANES:SUBLANES + ts, :] = x
    xc = cb_ref[...]
    for i in range(CONV_WIDTH):
        off = SUBLANES - (CONV_WIDTH - 1) + i
        xc = xc + xs_scr[off:off + ts, :] * cw_ref[i:i + 1, :]
    xs_scr[0:SUBLANES, :] = xs_scr[ts:ts + SUBLANES, :]

    ra = jnp.dot(xc.astype(BF16), wax_ref[...], preferred_element_type=F32) + bax_ref[...]
    r = jax.nn.sigmoid(ra[:, :w])
    ig = jax.nn.sigmoid(ra[:, w:])
    nl = -lam_ref[...]
    softplus = jnp.maximum(nl, 0.0) + jnp.log1p(jnp.exp(-jnp.abs(nl)))
    log_a = -LRU_C * r * softplus
    a = jnp.exp(log_a)
    a_scr[...] = a
    b_scr[...] = jnp.sqrt(-jnp.tanh(log_a) * (a * a + 1.0)) * (ig * xc)

    row = lax.broadcasted_iota(jnp.int32, (SUBLANES, w), 0)

    def body(c, h):
        i = pl.multiple_of(c * SUBLANES, SUBLANES)
        a = a_scr[pl.ds(i, SUBLANES), :]
        b = b_scr[pl.ds(i, SUBLANES), :]
        for s in (1, 2, 4):
            keep = row >= s
            a_sh = jnp.where(keep, pltpu.roll(a, s, 0), 1.0)
            b_sh = jnp.where(keep, pltpu.roll(b, s, 0), 0.0)
            b = a * b_sh + b
            a = a * a_sh
        hs = a * h + b
        b_scr[pl.ds(i, SUBLANES), :] = hs
        return hs[SUBLANES - 1:SUBLANES, :]

    h_scr[...] = lax.fori_loop(0, ts // SUBLANES, body, h_scr[...])

    gb = z_ref[0, :, w:2 * w].astype(F32)
    cdf = 0.5 * (1.0 + jnp.tanh(math.sqrt(2.0 / math.pi) * (gb + 0.044715 * (gb * gb * gb))))
    y_ref[0] = (b_scr[...] * (gb * cdf)).astype(y_ref.dtype)


def _block_diag(wblocks):
    nb, di, do = wblocks.shape
    eye = jnp.eye(nb, dtype=wblocks.dtype)
    return (eye[:, None, :, None] * wblocks[:, :, None, :]).reshape(nb * di, nb * do)


def _mixer_b(zb, conv_w, conv_b, lru_wa, lru_ba, lru_wx, lru_bx, lru_lam, batch, seq, ts=512):
    w = LRU_WIDTH
    wax = jnp.concatenate([_block_diag(lru_wa), _block_diag(lru_wx)], axis=1).astype(BF16)
    bax = jnp.concatenate([lru_ba, lru_bx]).reshape(1, 2 * w)
    y = pl.pallas_call(
        functools.partial(_mixer_b_kernel, ts=ts),
        grid=(batch, seq // ts),
        in_specs=[pl.BlockSpec((1, ts, 2 * w), lambda b, t: (b, t, 0)),
                  _const_spec((CONV_WIDTH, w)), _const_spec((1, w)),
                  _const_spec((w, 2 * w)), _const_spec((1, 2 * w)), _const_spec((1, w))],
        out_specs=pl.BlockSpec((1, ts, w), lambda b, t: (b, t, 0)),
        out_shape=jax.ShapeDtypeStruct((batch, seq, w), BF16),
        scratch_shapes=[pltpu.VMEM((ts + SUBLANES, w), F32), pltpu.VMEM((ts, w), F32),
                        pltpu.VMEM((ts, w), F32), pltpu.VMEM((1, w), F32)],
        compiler_params=_cparams(("arbitrary", "arbitrary")),
        name="mixer_b",
    )(zb.reshape(batch, seq, 2 * w), conv_w, conv_b.reshape(1, w), wax, bax, lru_lam.reshape(1, w))
    return y.reshape(batch * seq, w)


HEAD_PAIR = LANE // HEAD_DIM


def _mixer_c_kernel(q_ref, k_ref, vt_ref, bias_ref, o_ref, kmean_scr, sel_scr, *, nblk, ndb):
    blk = MOBA_BLOCK
    qb = pl.program_id(2)

    @pl.when(qb == 0)
    def _():
        for n in range(nblk):
            kf = k_ref[0, n * blk:(n + 1) * blk, :].astype(F32)
            kmean_scr[n:n + 1, :] = jnp.sum(kf, axis=0, keepdims=True) * (1.0 / blk)

    q = q_ref[0]
    km = kmean_scr[...]
    km_hi = km.astype(BF16)
    km_lo = (km - km_hi.astype(F32)).astype(BF16)
    lane = lax.broadcasted_iota(jnp.int32, q.shape, 1)
    blk_id = lax.broadcasted_iota(jnp.int32, (nblk, blk), 0)
    past = blk_id < qb
    own = pl.multiple_of(qb * blk, blk)
    accs = []
    for hh in range(HEAD_PAIR):
        rows = slice(hh * HEAD_DIM, (hh + 1) * HEAD_DIM)
        qh = jnp.where((lane >= hh * HEAD_DIM) & (lane < (hh + 1) * HEAD_DIM), q, jnp.zeros_like(q))

        gate = (lax.dot_general(km_hi, qh, NT_DIMS, preferred_element_type=F32)
                + lax.dot_general(km_lo, qh, NT_DIMS, preferred_element_type=F32))
        gate = jnp.where(past, gate, NEG_INF)
        sel = jnp.zeros((nblk, blk), F32)
        for _ in range(min(MOBA_TOPK, nblk - 1)):
            top = jnp.max(gate, axis=0, keepdims=True)
            first = jnp.min(jnp.where(gate == top, blk_id, nblk), axis=0, keepdims=True)
            pick = blk_id == first
            sel = jnp.where(pick, 1.0, sel)
            gate = jnp.where(pick, -jnp.inf, gate)
        sel_scr[hh] = jnp.where(past, sel, 0.0)

        s = lax.dot_general(k_ref[0, pl.ds(own, blk), :], qh, NT_DIMS, preferred_element_type=F32)
        s = s + bias_ref[hh, 0]
        m = jnp.max(s, axis=0, keepdims=True)
        p = jnp.exp(s - m)
        l = jnp.sum(p, axis=0, keepdims=True)
        acc = jnp.dot(vt_ref[0, qb, rows, :], p.astype(BF16), preferred_element_type=F32)

        def body(kb, carry, hh=hh, rows=rows, qh=qh):
            m, l, acc = carry
            start = pl.multiple_of(kb * blk, blk)
            s = lax.dot_general(k_ref[0, pl.ds(start, blk), :], qh, NT_DIMS, preferred_element_type=F32)
            s = s + bias_ref[hh, jnp.minimum(qb - kb, ndb)]
            s = jnp.where(sel_scr[hh, pl.ds(kb, 1), :] > 0.0, s, NEG_INF)
            m_new = jnp.maximum(m, jnp.max(s, axis=0, keepdims=True))
            alpha = jnp.exp(m - m_new)
            p = jnp.exp(s - m_new)
            l = alpha * l + jnp.sum(p, axis=0, keepdims=True)
            acc = alpha * acc + jnp.dot(vt_ref[0, kb, rows, :], p.astype(BF16), preferred_element_type=F32)
            return m_new, l, acc

        m, l, acc = lax.fori_loop(0, qb, body, (m, l, acc))
        accs.append(acc / l)
    o_ref[0] = jnp.concatenate(accs, axis=0).T.astype(o_ref.dtype)


def _mixer_c(zc, vt, bias_c, batch, seq, ndb):
    blk = MOBA_BLOCK
    assert seq % blk == 0
    nblk = seq // blk
    npair = C_HEADS // HEAD_PAIR
    z = zc.reshape(batch, seq, 2 * C_WIDTH)
    o = pl.pallas_call(
        functools.partial(_mixer_c_kernel, nblk=nblk, ndb=ndb),
        grid=(batch, npair, nblk),
        in_specs=[pl.BlockSpec((1, blk, LANE), lambda b, hp, qb: (b, qb, hp)),
                  pl.BlockSpec((1, seq, LANE), lambda b, hp, qb: (b, 0, npair + hp)),
                  pl.BlockSpec((1, nblk, LANE, blk), lambda b, hp, qb: (b, 0, hp, 0)),
                  pl.BlockSpec((HEAD_PAIR, ndb + 1, blk, blk), lambda b, hp, qb: (hp, 0, 0, 0))],
        out_specs=pl.BlockSpec((1, blk, LANE), lambda b, hp, qb: (b, qb, hp)),
        out_shape=jax.ShapeDtypeStruct((batch, seq, C_WIDTH), BF16),
        scratch_shapes=[pltpu.VMEM((nblk, LANE), F32), pltpu.VMEM((HEAD_PAIR, nblk, blk), F32)],
        compiler_params=_cparams(("arbitrary", "arbitrary", "arbitrary")),
        name="mixer_c",
    )(z, z, vt, bias_c)
    return o.reshape(batch * seq, C_WIDTH)


def _merge_kernel(x_ref, o1_ref, o2_ref, o3_ref, l1_ref, l2_ref, l3_ref, ob_ref, oc_ref, zg_ref,
                  pa_ref, pb_ref, pc_ref, wo_ref, out_ref):
    d = D_MODEL
    l1, l2, l3 = l1_ref[...], l2_ref[...], l3_ref[...]
    m = jnp.maximum(jnp.maximum(l1, l2), l3)
    e1, e2, e3 = jnp.exp(l1 - m), jnp.exp(l2 - m), jnp.exp(l3 - m)
    o_a = (e1 * o1_ref[...].astype(F32) + e2 * o2_ref[...].astype(F32) + e3 * o3_ref[...].astype(F32))
    o_a = o_a / (e1 + e2 + e3)

    def branch(o, p_ref, k):
        gate = jax.nn.sigmoid(zg_ref[:, k * d:(k + 1) * d].astype(F32))
        return gate * jnp.dot(o, p_ref[...], preferred_element_type=F32)

    merged = branch(o_a.astype(BF16), pa_ref, 0) + branch(ob_ref[...], pb_ref, 1) + branch(oc_ref[...], pc_ref, 2)
    out_ref[...] = x_ref[...] + jnp.dot(merged.astype(BF16), wo_ref[...], preferred_element_type=F32)


def _merge(x2d, oa, lse, ob, oc, zg, p_a, p_b, p_c, w_out, tm=512):
    n = x2d.shape[0]
    row = lambda width: pl.BlockSpec((tm, width), lambda i: (i, 0))
    return pl.pallas_call(
        _merge_kernel,
        grid=(n // tm,),
        in_specs=[row(D_MODEL)] + [row(A_OUT)] * 6 + [row(LRU_WIDTH), row(C_WIDTH), row(3 * D_MODEL),
                  _const_spec((A_OUT, D_MODEL)), _const_spec((LRU_WIDTH, D_MODEL)),
                  _const_spec((C_WIDTH, D_MODEL)), _const_spec((D_MODEL, D_MODEL))],
        out_specs=row(D_MODEL),
        out_shape=jax.ShapeDtypeStruct((n, D_MODEL), F32),
        compiler_params=_cparams(("arbitrary",)),
        name="merge",
    )(x2d, *oa, *lse, ob, oc, zg, p_a.astype(BF16), p_b.astype(BF16), p_c.astype(BF16), w_out.astype(BF16))


def _rms(x, g):
    return x * lax.rsqrt(jnp.mean(x * x, axis=-1, keepdims=True) + EPS) * g


def _ffn_kernel(x_ref, g_ref, wgu_ref, wd_ref, gf_ref, out_ref, h_scr, act_scr, *, final_norm):
    x = x_ref[...]
    h_scr[...] = _rms(x, g_ref[...]).astype(BF16)
    for off, w in _col_chunks(0, FFN_HIDDEN, 256):
        gate = jnp.dot(h_scr[...], wgu_ref[:, off:off + w], preferred_element_type=F32)
        up = jnp.dot(h_scr[...], wgu_ref[:, FFN_HIDDEN + off:FFN_HIDDEN + off + w], preferred_element_type=F32)
        act_scr[:, off:off + w] = (gate * jax.nn.sigmoid(gate) * up).astype(BF16)
    y = x + jnp.dot(act_scr[...], wd_ref[...], preferred_element_type=F32)
    if final_norm:
        y = _rms(y, gf_ref[...])
    out_ref[...] = y


def _ffn(x2d, g_ffn, w_gu, w_down, g_final, final_norm, tm=512):
    n = x2d.shape[0]
    row = pl.BlockSpec((tm, D_MODEL), lambda i: (i, 0))
    return pl.pallas_call(
        functools.partial(_ffn_kernel, final_norm=final_norm),
        grid=(n // tm,),
        in_specs=[row, _const_spec((1, D_MODEL)), _const_spec((D_MODEL, 2 * FFN_HIDDEN)),
                  _const_spec((FFN_HIDDEN, D_MODEL)), _const_spec((1, D_MODEL))],
        out_specs=row,
        out_shape=jax.ShapeDtypeStruct((n, D_MODEL), F32),
        scratch_shapes=[pltpu.VMEM((tm, D_MODEL), BF16), pltpu.VMEM((tm, FFN_HIDDEN), BF16)],
        compiler_params=_cparams(("arbitrary",)),
        name="ffn",
    )(x2d, g_ffn.reshape(1, D_MODEL), w_gu.astype(BF16), w_down.astype(BF16), g_final.reshape(1, D_MODEL))


def kernel(x, rel_bias, g_mix, w_in, conv_w, conv_b, lru_wa, lru_ba, lru_wx, lru_bx, lru_lam,
           p_a, p_b, p_c, w_out, g_ffn, w_gu, w_down, g_final):
    batch, seq, d = x.shape
    assert d == D_MODEL and w_in.shape[-1] == IN_COLS
    depth = w_in.shape[0]
    nblk = seq // MOBA_BLOCK
    ndb = _moba_far_blocks(nblk)

    bias_a = _build_bias(rel_bias, _bucket_index_a(), A_HEADS, 0, A_HPG)
    bias_c = _build_bias(rel_bias, _bucket_index_c(ndb), C_HEADS, A_HEADS, None)

    x2d = x.reshape(batch * seq, d)
    for l in range(depth):
        za, zb, zc, vt, zg = _in_proj(x2d, g_mix[l], w_in[l], batch, seq)
        oa, lse = zip(*[_mixer_a_group(za, bias_a, g, dil, batch, seq)
                        for g, (_, dil) in enumerate(A_GROUPS)])
        ob = _mixer_b(zb, conv_w[l], conv_b[l], lru_wa[l], lru_ba[l], lru_wx[l], lru_bx[l], lru_lam[l],
                      batch, seq)
        oc = _mixer_c(zc, vt, bias_c, batch, seq, ndb)
        x2d = _merge(x2d, oa, lse, ob, oc, zg, p_a[l], p_b[l], p_c[l], w_out[l])
        x2d = _ffn(x2d, g_ffn[l], w_gu[l], w_down[l], g_final, final_norm=(l == depth - 1))
    return x2d.reshape(batch, seq, d)
```

```python
import functools
import math

import numpy as np
import jax
import jax.numpy as jnp
from jax import lax
from jax.experimental import pallas as pl
from jax.experimental.pallas import tpu as pltpu

F32 = jnp.float32
BF16 = jnp.bfloat16

D_MODEL = 1024
HEAD_DIM = 64
SCALE = HEAD_DIM ** -0.5
NEG_INF = -1e30
EPS = 1e-6
A_GROUPS = ((128, 1), (512, 4), (2048, 16))
A_HPG = 4
A_HEADS = A_HPG * len(A_GROUPS)
A_WIDTH = A_HEADS * HEAD_DIM
A_OUT = A_HPG * HEAD_DIM
A_BLOCK = 128
LRU_WIDTH = D_MODEL // 2
LRU_BLOCKS = 8
CONV_WIDTH = 4
LRU_C = 8.0
C_HEADS = 8
C_WIDTH = C_HEADS * HEAD_DIM
MOBA_BLOCK = 256
MOBA_TOPK = 3
REL_BUCKETS = 32
REL_MAX_DIST = 2048
FFN_HIDDEN = 2816
IN_COLS = 3 * A_WIDTH + 2 * LRU_WIDTH + 3 * C_WIDTH + 3 * D_MODEL
V_C_OFF = 3 * A_WIDTH + 2 * LRU_WIDTH + 2 * C_WIDTH

LANE = 128
VMEM_LIMIT = 56 * 1024 * 1024

NT_DIMS = (((1,), (1,)), ((), ()))


def _cparams(sem, vmem=VMEM_LIMIT):
    return pltpu.CompilerParams(dimension_semantics=sem, vmem_limit_bytes=vmem)


def _const_spec(shape):
    nd = len(shape)
    return pl.BlockSpec(shape, lambda *_: (0,) * nd, pipeline_mode=pl.Buffered(1))


def _rel_bucket_np(dist):
    max_exact = REL_BUCKETS // 2
    d = np.maximum(dist, 0)
    df = np.maximum(d, 1).astype(np.float32)
    large = max_exact + (np.log(df / np.float32(max_exact)) / np.float32(math.log(REL_MAX_DIST / max_exact))
                         * np.float32(REL_BUCKETS - max_exact)).astype(np.int32)
    large = np.minimum(large, REL_BUCKETS - 1)
    return np.where(d < max_exact, d, large).astype(np.int32)


def _bucket_index_a():
    qi = np.arange(A_BLOCK)[:, None] + A_BLOCK
    kj = np.arange(2 * A_BLOCK)[None, :]
    delta = qi - kj
    mats = []
    for window, dil in A_GROUPS:
        band = (delta >= 0) & (delta <= window // dil)
        mats.append(np.where(band, _rel_bucket_np(delta * dil), -1))
    return np.stack(mats).astype(np.int32)


def _moba_far_blocks(nblk):
    for db in range(1, nblk + 1):
        lo = db * MOBA_BLOCK - (MOBA_BLOCK - 1)
        if np.all(_rel_bucket_np(np.arange(lo, nblk * MOBA_BLOCK)) == REL_BUCKETS - 1):
            return db
    return nblk


def _bucket_index_c(ndb):
    k = np.arange(MOBA_BLOCK)[:, None]
    q = np.arange(MOBA_BLOCK)[None, :]
    mats = []
    for db in range(ndb + 1):
        dist = db * MOBA_BLOCK + q - k
        b = _rel_bucket_np(dist)
        if db == 0:
            b = np.where(dist >= 0, b, -1)
        mats.append(b)
    return np.stack(mats).astype(np.int32)


def _bias_kernel(tab_ref, idx_ref, out_ref, *, head_off):
    h = pl.program_id(0) + head_off
    idx = idx_ref[0]
    out = jnp.full(idx.shape, NEG_INF, F32)
    for b in range(REL_BUCKETS):
        out = jnp.where(idx == b, tab_ref[b, h], out)
    out_ref[0, 0] = out


def _build_bias(rel_bias, idx, n_heads, head_off, heads_per_idx_group):
    n_mats = idx.shape[0] if heads_per_idx_group is None else 1
    r, c = idx.shape[-2:]
    if heads_per_idx_group is None:
        idx_map = lambda h, m: (m, 0, 0)
    else:
        idx_map = lambda h, m: (h // heads_per_idx_group, 0, 0)
    return pl.pallas_call(
        functools.partial(_bias_kernel, head_off=head_off),
        grid=(n_heads, n_mats),
        in_specs=[pl.BlockSpec(memory_space=pltpu.SMEM),
                  pl.BlockSpec((1, r, c), idx_map)],
        out_specs=pl.BlockSpec((1, 1, r, c), lambda h, m: (h, m, 0, 0)),
        out_shape=jax.ShapeDtypeStruct((n_heads, n_mats, r, c), F32),
        compiler_params=_cparams(("arbitrary", "arbitrary")),
        name="rel_bias_build",
    )(rel_bias, jnp.asarray(idx))


def _col_chunks(start, stop, width=512):
    out = []
    while start < stop:
        w = min(width, stop - start)
        out.append((start, w))
        start += w
    return out


def _in_proj_kernel(x_ref, g_ref, w_ref, wvt_ref, za_ref, zb_ref, zc_ref, vt_ref, zg_ref, h_scr, *, tm):
    x = x_ref[...]
    ms = jnp.mean(x * x, axis=-1, keepdims=True)
    h_scr[...] = (x * lax.rsqrt(ms + EPS) * g_ref[...]).astype(BF16)

    def seg(out_ref, base, width, scaled_cols, chunk=512):
        assert scaled_cols % chunk == 0
        for off, w in _col_chunks(0, width, chunk):
            r = jnp.dot(h_scr[...], w_ref[:, base + off:base + off + w], preferred_element_type=F32)
            if off < scaled_cols:
                r = r * SCALE
            out_ref[:, off:off + w] = r.astype(out_ref.dtype)

    wa, wb, wc = 3 * A_WIDTH, 2 * LRU_WIDTH, 2 * C_WIDTH
    seg(za_ref, 0, wa, A_WIDTH, A_WIDTH)
    seg(zb_ref, wa, wb, 0)
    seg(zc_ref, wa + wb, wc, C_WIDTH)
    seg(zg_ref, wa + wb + wc, 3 * D_MODEL, 0)
    for j in range(tm // MOBA_BLOCK):
        hj = h_scr[j * MOBA_BLOCK:(j + 1) * MOBA_BLOCK, :]
        vt_ref[0, j] = lax.dot_general(wvt_ref[...], hj, NT_DIMS, preferred_element_type=F32).astype(BF16)


def _in_proj(x2d, g, w_in, batch, seq, tm=512):
    n = x2d.shape[0]
    w_main = jnp.concatenate([w_in[:, :V_C_OFF], w_in[:, V_C_OFF + C_WIDTH:]], axis=1).astype(BF16)
    wvt = w_in[:, V_C_OFF:V_C_OFF + C_WIDTH].T.astype(BF16)
    wa, wb, wc, wg = 3 * A_WIDTH, 2 * LRU_WIDTH, 2 * C_WIDTH, 3 * D_MODEL
    tiles_per_seq = seq // tm
    bpt = tm // MOBA_BLOCK
    nblk = seq // MOBA_BLOCK
    row = lambda i: (i, 0)
    return pl.pallas_call(
        functools.partial(_in_proj_kernel, tm=tm),
        grid=(n // tm,),
        in_specs=[pl.BlockSpec((tm, D_MODEL), row),
                  _const_spec((1, D_MODEL)),
                  _const_spec(w_main.shape),
                  _const_spec(wvt.shape)],
        out_specs=[pl.BlockSpec((tm, wa), row),
                   pl.BlockSpec((tm, wb), row),
                   pl.BlockSpec((tm, wc), row),
                   pl.BlockSpec((1, bpt, C_WIDTH, MOBA_BLOCK),
                                lambda i: (i // tiles_per_seq, i % tiles_per_seq, 0, 0)),
                   pl.BlockSpec((tm, wg), row)],
        out_shape=[jax.ShapeDtypeStruct((n, wa), BF16),
                   jax.ShapeDtypeStruct((n, wb), BF16),
                   jax.ShapeDtypeStruct((n, wc), BF16),
                   jax.ShapeDtypeStruct((batch, nblk, C_WIDTH, MOBA_BLOCK), BF16),
                   jax.ShapeDtypeStruct((n, wg), BF16)],
        scratch_shapes=[pltpu.VMEM((tm, D_MODEL), BF16)],
        compiler_params=_cparams(("arbitrary",)),
        name="in_proj",
    )(x2d, g.reshape(1, D_MODEL), w_main, wvt)


def _mixer_a_kernel(q_ref, kc_ref, vc_ref, kp_ref, vp_ref, bias_ref, o_ref, lse_ref, *, nsub):
    t = pl.program_id(2)
    for j in range(nsub):
        rows = slice(j * A_BLOCK, (j + 1) * A_BLOCK)
        q = q_ref[0, rows, :]
        k_cur = kc_ref[0, rows, :]
        v_cur = vc_ref[0, rows, :]
        if j == 0:
            k_prev, v_prev = kp_ref[0], vp_ref[0]
        else:
            prev = slice((j - 1) * A_BLOCK, j * A_BLOCK)
            k_prev, v_prev = kc_ref[0, prev, :], vc_ref[0, prev, :]
        outs, lses = [], []
        for h in range(A_HPG):
            sl = slice(h * HEAD_DIM, (h + 1) * HEAD_DIM)
            qh = q[:, sl]
            s_p = lax.dot_general(qh, k_prev[:, sl], NT_DIMS, preferred_element_type=F32)
            s_c = lax.dot_general(qh, k_cur[:, sl], NT_DIMS, preferred_element_type=F32)
            s_p = s_p + bias_ref[h, 0, :, :A_BLOCK]
            s_c = s_c + bias_ref[h, 0, :, A_BLOCK:]
            if j == 0:
                s_p = jnp.where(t == 0, NEG_INF, s_p)
            m = jnp.maximum(jnp.max(s_p, axis=-1, keepdims=True), jnp.max(s_c, axis=-1, keepdims=True))
            p_p = jnp.exp(s_p - m)
            p_c = jnp.exp(s_c - m)
            den = jnp.sum(p_p, axis=-1, keepdims=True) + jnp.sum(p_c, axis=-1, keepdims=True)
            pv = (jnp.dot(p_p.astype(BF16), v_prev[:, sl], preferred_element_type=F32)
                  + jnp.dot(p_c.astype(BF16), v_cur[:, sl], preferred_element_type=F32))
            outs.append(pv / den)
            lses.append(jnp.broadcast_to(m + jnp.log(den), (A_BLOCK, HEAD_DIM)))
        o_ref[0, rows, :] = jnp.concatenate(outs, axis=-1).astype(o_ref.dtype)
        lse_ref[0, rows, :] = jnp.concatenate(lses, axis=-1)


def _mixer_a_group(za, bias_a, g, dil, batch, seq):
    sub_len = seq // dil
    assert sub_len % A_BLOCK == 0
    tq = min(sub_len, 512)
    nsub = tq // A_BLOCK
    wa = 3 * A_WIDTH
    ncol = wa // A_OUT
    z = za.reshape(batch, sub_len, dil * wa)
    qcol, kcol, vcol = g, A_WIDTH // A_OUT + g, 2 * A_WIDTH // A_OUT + g
    cur = lambda col: pl.BlockSpec((1, tq, A_OUT), lambda b, r, t: (b, t, r * ncol + col))
    prev = lambda col: pl.BlockSpec((1, A_BLOCK, A_OUT),
                                    lambda b, r, t: (b, jnp.maximum(t * nsub - 1, 0), r * ncol + col))
    out_spec = pl.BlockSpec((1, tq, A_OUT), lambda b, r, t: (b, t, r))
    o, lse = pl.pallas_call(
        functools.partial(_mixer_a_kernel, nsub=nsub),
        grid=(batch, dil, sub_len // tq),
        in_specs=[cur(qcol), cur(kcol), cur(vcol), prev(kcol), prev(vcol),
                  pl.BlockSpec((A_HPG, 1, A_BLOCK, 2 * A_BLOCK), lambda b, r, t: (g, 0, 0, 0))],
        out_specs=[out_spec, out_spec],
        out_shape=[jax.ShapeDtypeStruct((batch, sub_len, dil * A_OUT), BF16),
                   jax.ShapeDtypeStruct((batch, sub_len, dil * A_OUT), F32)],
        compiler_params=_cparams(("arbitrary", "arbitrary", "arbitrary")),
        name=f"mixer_a_d{dil}",
    )(z, z, z, z, z, bias_a)
    return o.reshape(batch * seq, A_OUT), lse.reshape(batch * seq, A_OUT)


SUBLANES = 8


def _mixer_b_kernel(z_ref, cw_ref, cb_ref, wax_ref, bax_ref, lam_ref, y_ref,
                    xs_scr, a_scr, b_scr, h_scr, *, ts):
    w = LRU_WIDTH

    @pl.when(pl.program_id(1) == 0)
    def _():
        xs_scr[0:SUBLANES, :] = jnp.zeros((SUBLANES, w), F32)
        h_scr[...] = jnp.zeros((1, w), F32)

    x = z_ref[0, :, 0:w].astype(F32)
    xs_scr[SUBLANES:SUBLANES + ts, :] = x
    xc = cb_ref[...]
    for i in range(CONV_WIDTH):
        off = SUBLANES - (CONV_WIDTH - 1) + i
        xc = xc + xs_scr[off:off + ts, :] * cw_ref[i:i + 1, :]
    xs_scr[0:SUBLANES, :] = xs_scr[ts:ts + SUBLANES, :]

    ra = jnp.dot(xc.astype(BF16), wax_ref[...], preferred_element_type=F32) + bax_ref[...]
    r = jax.nn.sigmoid(ra[:, :w])
    ig = jax.nn.sigmoid(ra[:, w:])
    nl = -lam_ref[...]
    softplus = jnp.maximum(nl, 0.0) + jnp.log1p(jnp.exp(-jnp.abs(nl)))
    log_a = -LRU_C * r * softplus
    a = jnp.exp(log_a)
    a_scr[...] = a
    b_scr[...] = jnp.sqrt(-jnp.tanh(log_a) * (a * a + 1.0)) * (ig * xc)

    row = lax.broadcasted_iota(jnp.int32, (SUBLANES, w), 0)

    def body(c, h):
        i = pl.multiple_of(c * SUBLANES, SUBLANES)
        a = a_scr[pl.ds(i, SUBLANES), :]
        b = b_scr[pl.ds(i, SUBLANES), :]
        for s in (1, 2, 4):
            keep = row >= s
            a_sh = jnp.where(keep, pltpu.roll(a, s, 0), 1.0)
            b_sh = jnp.where(keep, pltpu.roll(b, s, 0), 0.0)
            b = a * b_sh + b
            a = a * a_sh
        hs = a * h + b
        b_scr[pl.ds(i, SUBLANES), :] = hs
        return hs[SUBLANES - 1:SUBLANES, :]

    h_scr[...] = lax.fori_loop(0, ts // SUBLANES, body, h_scr[...])

    gb = z_ref[0, :, w:2 * w].astype(F32)
    cdf = 0.5 * (1.0 + jnp.tanh(math.sqrt(2.0 / math.pi) * (gb + 0.044715 * (gb * gb * gb))))
    y_ref[0] = (b_scr[...] * (gb * cdf)).astype(y_ref.dtype)


def _block_diag(wblocks):
    nb, di, do = wblocks.shape
    eye = jnp.eye(nb, dtype=wblocks.dtype)
    return (eye[:, None, :, None] * wblocks[:, :, None, :]).reshape(nb * di, nb * do)


def _mixer_b(zb, conv_w, conv_b, lru_wa, lru_ba, lru_wx, lru_bx, lru_lam, batch, seq, ts=512):
    w = LRU_WIDTH
    wax = jnp.concatenate([_block_diag(lru_wa), _block_diag(lru_wx)], axis=1).astype(BF16)
    bax = jnp.concatenate([lru_ba, lru_bx]).reshape(1, 2 * w)
    y = pl.pallas_call(
        functools.partial(_mixer_b_kernel, ts=ts),
        grid=(batch, seq // ts),
        in_specs=[pl.BlockSpec((1, ts, 2 * w), lambda b, t: (b, t, 0)),
                  _const_spec((CONV_WIDTH, w)), _const_spec((1, w)),
                  _const_spec((w, 2 * w)), _const_spec((1, 2 * w)), _const_spec((1, w))],
        out_specs=pl.BlockSpec((1, ts, w), lambda b, t: (b, t, 0)),
        out_shape=jax.ShapeDtypeStruct((batch, seq, w), BF16),
        scratch_shapes=[pltpu.VMEM((ts + SUBLANES, w), F32), pltpu.VMEM((ts, w), F32),
                        pltpu.VMEM((ts, w), F32), pltpu.VMEM((1, w), F32)],
        compiler_params=_cparams(("arbitrary", "arbitrary")),
        name="mixer_b",
    )(zb.reshape(batch, seq, 2 * w), conv_w, conv_b.reshape(1, w), wax, bax, lru_lam.reshape(1, w))
    return y.reshape(batch * seq, w)


HEAD_PAIR = LANE // HEAD_DIM


def _mixer_c_kernel(q_ref, k_ref, vt_ref, bias_ref, o_ref, kmean_scr, sel_scr, s_scr, *, nblk, ndb):
    blk = MOBA_BLOCK
    qb = pl.program_id(2)

    @pl.when(qb == 0)
    def _():
        for n in range(nblk):
            kf = k_ref[0, n * blk:(n + 1) * blk, :].astype(F32)
            kmean_scr[n:n + 1, :] = jnp.sum(kf, axis=0, keepdims=True) * (1.0 / blk)

    q = q_ref[0]
    km = kmean_scr[...]
    km_hi = km.astype(BF16)
    km_lo = (km - km_hi.astype(F32)).astype(BF16)
    lane = lax.broadcasted_iota(jnp.int32, q.shape, 1)
    blk_id = lax.broadcasted_iota(jnp.int32, (nblk, blk), 0)
    past = blk_id < qb
    qhs = []
    for hh in range(HEAD_PAIR):
        qh = jnp.where((lane >= hh * HEAD_DIM) & (lane < (hh + 1) * HEAD_DIM), q, jnp.zeros_like(q))
        qhs.append(qh)

        gate = (lax.dot_general(km_hi, qh, NT_DIMS, preferred_element_type=F32)
                + lax.dot_general(km_lo, qh, NT_DIMS, preferred_element_type=F32))
        gate = jnp.where(past, gate, NEG_INF)
        sel = jnp.zeros((nblk, blk), F32)
        for _ in range(min(MOBA_TOPK, nblk - 1)):
            top = jnp.max(gate, axis=0, keepdims=True)
            first = jnp.min(jnp.where(gate == top, blk_id, nblk), axis=0, keepdims=True)
            pick = blk_id == first
            sel = jnp.where(pick, 1.0, sel)
            gate = jnp.where(pick, -jnp.inf, gate)
        sel_scr[hh] = jnp.where(past, sel, jnp.where(blk_id == qb, 1.0, 0.0))

    npairs = lax.shift_right_logical(qb + 2, 1)
    fold = lambda t: t.reshape(blk // SUBLANES, SUBLANES, blk)

    def scores(i, mx):
        start = pl.multiple_of(i * (2 * blk), 2 * blk)
        out = []
        for hh in range(HEAD_PAIR):
            s = lax.dot_general(k_ref[0, pl.ds(start, 2 * blk), :], qhs[hh], NT_DIMS, preferred_element_type=F32)
            m8 = mx[hh]
            for u in range(2):
                kb = 2 * i + u
                su = s[u * blk:(u + 1) * blk] + bias_ref[hh, jnp.clip(qb - kb, 0, ndb)]
                su = jnp.where(sel_scr[hh, pl.ds(kb, 1), :] > 0.0, su, NEG_INF)
                s_scr[hh, i, u * blk:(u + 1) * blk, :] = su
                m8 = jnp.maximum(m8, jnp.max(fold(su), axis=0))
            out.append(m8)
        return tuple(out)

    mx = lax.fori_loop(0, npairs, scores, (jnp.full((SUBLANES, blk), NEG_INF, F32),) * HEAD_PAIR)
    ms = [jnp.max(m8, axis=0, keepdims=True) for m8 in mx]

    def weigh(i, carry):
        out = []
        for hh in range(HEAD_PAIR):
            rows = slice(hh * HEAD_DIM, (hh + 1) * HEAD_DIM)
            l8, acc = carry[2 * hh:2 * hh + 2]
            for u in range(2):
                p = jnp.exp(s_scr[hh, i, u * blk:(u + 1) * blk, :] - ms[hh])
                l8 = l8 + jnp.sum(fold(p), axis=0)
                acc = acc + jnp.dot(vt_ref[0, 2 * i + u, rows, :], p.astype(BF16), preferred_element_type=F32)
            out += [l8, acc]
        return tuple(out)

    init = (jnp.zeros((SUBLANES, blk), F32), jnp.zeros((HEAD_DIM, blk), F32)) * HEAD_PAIR
    res = lax.fori_loop(0, npairs, weigh, init)
    out_t = jnp.concatenate([res[2 * hh + 1] / jnp.sum(res[2 * hh], axis=0, keepdims=True)
                             for hh in range(HEAD_PAIR)], axis=0)
    o_ref[0] = out_t.T.astype(o_ref.dtype)


def _mixer_c(zc, vt, bias_c, batch, seq, ndb):
    blk = MOBA_BLOCK
    assert seq % (2 * blk) == 0
    nblk = seq // blk
    npair = C_HEADS // HEAD_PAIR
    z = zc.reshape(batch, seq, 2 * C_WIDTH)
    o = pl.pallas_call(
        functools.partial(_mixer_c_kernel, nblk=nblk, ndb=ndb),
        grid=(batch, npair, nblk),
        in_specs=[pl.BlockSpec((1, blk, LANE), lambda b, hp, qb: (b, qb, hp)),
                  pl.BlockSpec((1, seq, LANE), lambda b, hp, qb: (b, 0, npair + hp)),
                  pl.BlockSpec((1, nblk, LANE, blk), lambda b, hp, qb: (b, 0, hp, 0)),
                  pl.BlockSpec((HEAD_PAIR, ndb + 1, blk, blk), lambda b, hp, qb: (hp, 0, 0, 0))],
        out_specs=pl.BlockSpec((1, blk, LANE), lambda b, hp, qb: (b, qb, hp)),
        out_shape=jax.ShapeDtypeStruct((batch, seq, C_WIDTH), BF16),
        scratch_shapes=[pltpu.VMEM((nblk, LANE), F32), pltpu.VMEM((HEAD_PAIR, nblk, blk), F32),
                        pltpu.VMEM((HEAD_PAIR, nblk // 2, 2 * blk, blk), F32)],
        compiler_params=_cparams(("arbitrary", "arbitrary", "arbitrary")),
        name="mixer_c",
    )(z, z, vt, bias_c)
    return o.reshape(batch * seq, C_WIDTH)


def _merge_kernel(x_ref, o1_ref, o2_ref, o3_ref, l1_ref, l2_ref, l3_ref, ob_ref, oc_ref, zg_ref,
                  pa_ref, pb_ref, pc_ref, wo_ref, out_ref):
    d = D_MODEL
    l1, l2, l3 = l1_ref[...], l2_ref[...], l3_ref[...]
    m = jnp.maximum(jnp.maximum(l1, l2), l3)
    e1, e2, e3 = jnp.exp(l1 - m), jnp.exp(l2 - m), jnp.exp(l3 - m)
    o_a = (e1 * o1_ref[...].astype(F32) + e2 * o2_ref[...].astype(F32) + e3 * o3_ref[...].astype(F32))
    o_a = o_a / (e1 + e2 + e3)

    def branch(o, p_ref, k):
        gate = jax.nn.sigmoid(zg_ref[:, k * d:(k + 1) * d].astype(F32))
        return gate * jnp.dot(o, p_ref[...], preferred_element_type=F32)

    merged = branch(o_a.astype(BF16), pa_ref, 0) + branch(ob_ref[...], pb_ref, 1) + branch(oc_ref[...], pc_ref, 2)
    out_ref[...] = x_ref[...] + jnp.dot(merged.astype(BF16), wo_ref[...], preferred_element_type=F32)


def _merge(x2d, oa, lse, ob, oc, zg, p_a, p_b, p_c, w_out, tm=512):
    n = x2d.shape[0]
    row = lambda width: pl.BlockSpec((tm, width), lambda i: (i, 0))
    return pl.pallas_call(
        _merge_kernel,
        grid=(n // tm,),
        in_specs=[row(D_MODEL)] + [row(A_OUT)] * 6 + [row(LRU_WIDTH), row(C_WIDTH), row(3 * D_MODEL),
                  _const_spec((A_OUT, D_MODEL)), _const_spec((LRU_WIDTH, D_MODEL)),
                  _const_spec((C_WIDTH, D_MODEL)), _const_spec((D_MODEL, D_MODEL))],
        out_specs=row(D_MODEL),
        out_shape=jax.ShapeDtypeStruct((n, D_MODEL), F32),
        compiler_params=_cparams(("arbitrary",)),
        name="merge",
    )(x2d, *oa, *lse, ob, oc, zg, p_a.astype(BF16), p_b.astype(BF16), p_c.astype(BF16), w_out.astype(BF16))


def _rms(x, g):
    return x * lax.rsqrt(jnp.mean(x * x, axis=-1, keepdims=True) + EPS) * g


def _ffn_kernel(x_ref, g_ref, wgu_ref, wd_ref, gf_ref, out_ref, h_scr, act_scr, *, final_norm):
    x = x_ref[...]
    h_scr[...] = _rms(x, g_ref[...]).astype(BF16)
    for off, w in _col_chunks(0, FFN_HIDDEN, 256):
        gate = jnp.dot(h_scr[...], wgu_ref[:, off:off + w], preferred_element_type=F32)
        up = jnp.dot(h_scr[...], wgu_ref[:, FFN_HIDDEN + off:FFN_HIDDEN + off + w], preferred_element_type=F32)
        act_scr[:, off:off + w] = (gate * jax.nn.sigmoid(gate) * up).astype(BF16)
    y = x + jnp.dot(act_scr[...], wd_ref[...], preferred_element_type=F32)
    if final_norm:
        y = _rms(y, gf_ref[...])
    out_ref[...] = y


def _ffn(x2d, g_ffn, w_gu, w_down, g_final, final_norm, tm=512):
    n = x2d.shape[0]
    row = pl.BlockSpec((tm, D_MODEL), lambda i: (i, 0))
    return pl.pallas_call(
        functools.partial(_ffn_kernel, final_norm=final_norm),
        grid=(n // tm,),
        in_specs=[row, _const_spec((1, D_MODEL)), _const_spec((D_MODEL, 2 * FFN_HIDDEN)),
                  _const_spec((FFN_HIDDEN, D_MODEL)), _const_spec((1, D_MODEL))],
        out_specs=row,
        out_shape=jax.ShapeDtypeStruct((n, D_MODEL), F32),
        scratch_shapes=[pltpu.VMEM((tm, D_MODEL), BF16), pltpu.VMEM((tm, FFN_HIDDEN), BF16)],
        compiler_params=_cparams(("arbitrary",)),
        name="ffn",
    )(x2d, g_ffn.reshape(1, D_MODEL), w_gu.astype(BF16), w_down.astype(BF16), g_final.reshape(1, D_MODEL))


def kernel(x, rel_bias, g_mix, w_in, conv_w, conv_b, lru_wa, lru_ba, lru_wx, lru_bx, lru_lam,
           p_a, p_b, p_c, w_out, g_ffn, w_gu, w_down, g_final):
    batch, seq, d = x.shape
    assert d == D_MODEL and w_in.shape[-1] == IN_COLS
    depth = w_in.shape[0]
    nblk = seq // MOBA_BLOCK
    ndb = _moba_far_blocks(nblk)

    bias_a = _build_bias(rel_bias, _bucket_index_a(), A_HEADS, 0, A_HPG)
    bias_c = _build_bias(rel_bias, _bucket_index_c(ndb), C_HEADS, A_HEADS, None)

    x2d = x.reshape(batch * seq, d)
    for l in range(depth):
        za, zb, zc, vt, zg = _in_proj(x2d, g_mix[l], w_in[l], batch, seq)
        oa, lse = zip(*[_mixer_a_group(za, bias_a, g, dil, batch, seq)
                        for g, (_, dil) in enumerate(A_GROUPS)])
        ob = _mixer_b(zb, conv_w[l], conv_b[l], lru_wa[l], lru_ba[l], lru_wx[l], lru_bx[l], lru_lam[l],
                      batch, seq)
        oc = _mixer_c(zc, vt, bias_c, batch, seq, ndb)
        x2d = _merge(x2d, oa, lse, ob, oc, zg, p_a[l], p_b[l], p_c[l], w_out[l])
        x2d = _ffn(x2d, g_ffn[l], w_gu[l], w_down[l], g_final, final_norm=(l == depth - 1))
    return x2d.reshape(batch, seq, d)
```

```python
import functools
import math

import numpy as np
import jax
import jax.numpy as jnp
from jax import lax
from jax.experimental import pallas as pl
from jax.experimental.pallas import tpu as pltpu

F32 = jnp.float32
BF16 = jnp.bfloat16

D_MODEL = 1024
HEAD_DIM = 64
SCALE = HEAD_DIM ** -0.5
NEG_INF = -1e30
EPS = 1e-6
A_GROUPS = ((128, 1), (512, 4), (2048, 16))
A_HPG = 4
A_HEADS = A_HPG * len(A_GROUPS)
A_WIDTH = A_HEADS * HEAD_DIM
A_OUT = A_HPG * HEAD_DIM
A_BLOCK = 128
LRU_WIDTH = D_MODEL // 2
LRU_BLOCKS = 8
CONV_WIDTH = 4
LRU_C = 8.0
C_HEADS = 8
C_WIDTH = C_HEADS * HEAD_DIM
MOBA_BLOCK = 256
MOBA_TOPK = 3
REL_BUCKETS = 32
REL_MAX_DIST = 2048
FFN_HIDDEN = 2816
IN_COLS = 3 * A_WIDTH + 2 * LRU_WIDTH + 3 * C_WIDTH + 3 * D_MODEL
V_C_OFF = 3 * A_WIDTH + 2 * LRU_WIDTH + 2 * C_WIDTH

LANE = 128
VMEM_LIMIT = 56 * 1024 * 1024

NT_DIMS = (((1,), (1,)), ((), ()))


def _cparams(sem, vmem=VMEM_LIMIT):
    return pltpu.CompilerParams(dimension_semantics=sem, vmem_limit_bytes=vmem)


def _const_spec(shape):
    nd = len(shape)
    return pl.BlockSpec(shape, lambda *_: (0,) * nd, pipeline_mode=pl.Buffered(1))


def _rel_bucket_np(dist):
    max_exact = REL_BUCKETS // 2
    d = np.maximum(dist, 0)
    df = np.maximum(d, 1).astype(np.float32)
    large = max_exact + (np.log(df / np.float32(max_exact)) / np.float32(math.log(REL_MAX_DIST / max_exact))
                         * np.float32(REL_BUCKETS - max_exact)).astype(np.int32)
    large = np.minimum(large, REL_BUCKETS - 1)
    return np.where(d < max_exact, d, large).astype(np.int32)


def _bucket_index_a():
    qi = np.arange(A_BLOCK)[None, :] + A_BLOCK
    kj = np.arange(2 * A_BLOCK)[:, None]
    delta = qi - kj
    mats = []
    for window, dil in A_GROUPS:
        band = (delta >= 0) & (delta <= window // dil)
        mats.append(np.where(band, _rel_bucket_np(delta * dil), -1))
    return np.stack(mats).astype(np.int32)


def _moba_far_blocks(nblk):
    for db in range(1, nblk + 1):
        lo = db * MOBA_BLOCK - (MOBA_BLOCK - 1)
        if np.all(_rel_bucket_np(np.arange(lo, nblk * MOBA_BLOCK)) == REL_BUCKETS - 1):
            return db
    return nblk


def _bucket_index_c(ndb):
    k = np.arange(MOBA_BLOCK)[:, None]
    q = np.arange(MOBA_BLOCK)[None, :]
    mats = []
    for db in range(ndb + 1):
        dist = db * MOBA_BLOCK + q - k
        b = _rel_bucket_np(dist)
        if db == 0:
            b = np.where(dist >= 0, b, -1)
        mats.append(b)
    return np.stack(mats).astype(np.int32)


def _bias_kernel(tab_ref, idx_ref, out_ref, *, head_off):
    h = pl.program_id(0) + head_off
    idx = idx_ref[0]
    out = jnp.full(idx.shape, NEG_INF, F32)
    for b in range(REL_BUCKETS):
        out = jnp.where(idx == b, tab_ref[b, h], out)
    out_ref[0, 0] = out


def _build_bias(rel_bias, idx, n_heads, head_off, heads_per_idx_group):
    n_mats = idx.shape[0] if heads_per_idx_group is None else 1
    r, c = idx.shape[-2:]
    if heads_per_idx_group is None:
        idx_map = lambda h, m: (m, 0, 0)
    else:
        idx_map = lambda h, m: (h // heads_per_idx_group, 0, 0)
    return pl.pallas_call(
        functools.partial(_bias_kernel, head_off=head_off),
        grid=(n_heads, n_mats),
        in_specs=[pl.BlockSpec(memory_space=pltpu.SMEM),
                  pl.BlockSpec((1, r, c), idx_map)],
        out_specs=pl.BlockSpec((1, 1, r, c), lambda h, m: (h, m, 0, 0)),
        out_shape=jax.ShapeDtypeStruct((n_heads, n_mats, r, c), F32),
        compiler_params=_cparams(("arbitrary", "arbitrary")),
        name="rel_bias_build",
    )(rel_bias, jnp.asarray(idx))


def _col_chunks(start, stop, width=512):
    out = []
    while start < stop:
        w = min(width, stop - start)
        out.append((start, w))
        start += w
    return out


def _in_proj_kernel(x_ref, g_ref, w_ref, wvt_ref, za0_ref, za1_ref, za2_ref, zb_ref, zc_ref, vt_ref, zg_ref,
                    h_scr, dil_scr, *, tm):
    x = x_ref[...]
    ms = jnp.mean(x * x, axis=-1, keepdims=True)
    h_scr[...] = (x * lax.rsqrt(ms + EPS) * g_ref[...]).astype(BF16)

    def proj(base, width):
        return jnp.dot(h_scr[...], w_ref[:, base:base + width], preferred_element_type=F32)

    def seg(out_ref, base, width, scaled_cols, chunk=512):
        assert scaled_cols % chunk == 0
        for off, w in _col_chunks(0, width, chunk):
            r = proj(base + off, w)
            if off < scaled_cols:
                r = r * SCALE
            out_ref[:, off:off + w] = r.astype(out_ref.dtype)

    gw = 3 * A_OUT
    for g, (za_ref, (_, dil)) in enumerate(zip((za0_ref, za1_ref, za2_ref), A_GROUPS)):
        q = proj(g * gw, A_OUT) * SCALE
        kv = proj(g * gw + A_OUT, 2 * A_OUT)
        if dil == 1:
            za_ref[0, 0, :, :A_OUT] = q.astype(BF16)
            za_ref[0, 0, :, A_OUT:] = kv.astype(BF16)
        else:
            qkv = (q[:, :LANE], q[:, LANE:]) + tuple(kv[:, c * LANE:(c + 1) * LANE] for c in range(4))
            for c, part in enumerate(qkv):
                dil_scr[c] = part
            for r in range(dil):
                for c in range(len(qkv)):
                    za_ref[0, r, :, c * LANE:(c + 1) * LANE] = (
                        dil_scr[c, pl.ds(r, tm // dil, stride=dil), :].astype(BF16))

    wa, wb, wc = 3 * A_WIDTH, 2 * LRU_WIDTH, 2 * C_WIDTH
    seg(zb_ref, wa, wb, 0)
    seg(zc_ref, wa + wb, wc, C_WIDTH)
    seg(zg_ref, wa + wb + wc, 3 * D_MODEL, 0)
    for j in range(tm // MOBA_BLOCK):
        hj = h_scr[j * MOBA_BLOCK:(j + 1) * MOBA_BLOCK, :]
        vt_ref[0, j] = lax.dot_general(wvt_ref[...], hj, NT_DIMS, preferred_element_type=F32).astype(BF16)


def _in_proj(x2d, g, w_in, batch, seq, tm=512):
    n = x2d.shape[0]
    grp = lambda k: jnp.concatenate([w_in[:, s * A_WIDTH + k * A_OUT:s * A_WIDTH + (k + 1) * A_OUT]
                                     for s in range(3)], axis=1)
    w_main = jnp.concatenate([grp(k) for k in range(len(A_GROUPS))]
                             + [w_in[:, 3 * A_WIDTH:V_C_OFF], w_in[:, V_C_OFF + C_WIDTH:]], axis=1).astype(BF16)
    wvt = w_in[:, V_C_OFF:V_C_OFF + C_WIDTH].T.astype(BF16)
    wb, wc, wg = 2 * LRU_WIDTH, 2 * C_WIDTH, 3 * D_MODEL
    gw = 3 * A_OUT
    tiles_per_seq = seq // tm
    bpt = tm // MOBA_BLOCK
    nblk = seq // MOBA_BLOCK
    row = lambda i: (i, 0)
    seq_tile = lambda i: (i // tiles_per_seq, 0, i % tiles_per_seq, 0)
    za_specs = [pl.BlockSpec((1, dil, tm // dil, gw), seq_tile) for _, dil in A_GROUPS]
    za_shapes = [jax.ShapeDtypeStruct((batch, dil, seq // dil, gw), BF16) for _, dil in A_GROUPS]
    return pl.pallas_call(
        functools.partial(_in_proj_kernel, tm=tm),
        grid=(n // tm,),
        in_specs=[pl.BlockSpec((tm, D_MODEL), row),
                  _const_spec((1, D_MODEL)),
                  _const_spec(w_main.shape),
                  _const_spec(wvt.shape)],
        out_specs=za_specs + [pl.BlockSpec((tm, wb), row),
                              pl.BlockSpec((tm, wc), row),
                              pl.BlockSpec((1, bpt, C_WIDTH, MOBA_BLOCK),
                                           lambda i: (i // tiles_per_seq, i % tiles_per_seq, 0, 0)),
                              pl.BlockSpec((tm, wg), row)],
        out_shape=za_shapes + [jax.ShapeDtypeStruct((n, wb), BF16),
                               jax.ShapeDtypeStruct((n, wc), BF16),
                               jax.ShapeDtypeStruct((batch, nblk, C_WIDTH, MOBA_BLOCK), BF16),
                               jax.ShapeDtypeStruct((n, wg), BF16)],
        scratch_shapes=[pltpu.VMEM((tm, D_MODEL), BF16), pltpu.VMEM((gw // LANE, tm, LANE), F32)],
        compiler_params=_cparams(("arbitrary",)),
        name="in_proj",
    )(x2d, g.reshape(1, D_MODEL), w_main, wvt)


A_TOKENS_PER_STEP = 2048


def _mixer_a_kernel(z_ref, zp_ref, bias_ref, o_ref, lse_ref, *, dil, nsub):
    t = pl.program_id(1)
    blk, width = A_BLOCK, A_OUT
    kcols, vcols = slice(width, 2 * width), slice(2 * width, 3 * width)
    lane_head = lax.broadcasted_iota(jnp.int32, (blk, width), 1) // HEAD_DIM
    in_prev = (lax.broadcasted_iota(jnp.int32, (2 * blk, blk), 0) < blk).astype(F32)
    no_prev = in_prev * jnp.where(t == 0, NEG_INF, 0.0)
    transpose_v = lambda v: v.astype(F32).T.astype(BF16)

    for r in range(dil):
        k_prev = zp_ref[0, r, :, kcols]
        vt_prev = transpose_v(zp_ref[0, r, :, vcols])
        for j in range(nsub):
            rows = slice(j * blk, (j + 1) * blk)
            q = z_ref[0, r, rows, 0:width]
            k_cur = z_ref[0, r, rows, kcols]
            vt_cur = transpose_v(z_ref[0, r, rows, vcols])
            k_win = jnp.concatenate([k_prev, k_cur], axis=0)
            vt_win = jnp.concatenate([vt_prev, vt_cur], axis=1)
            outs, lses = [], []
            for h in range(A_HPG):
                qh = jnp.where(lane_head == h, q, jnp.zeros_like(q))
                s = lax.dot_general(k_win, qh, NT_DIMS, preferred_element_type=F32) + bias_ref[h, 0]
                if j == 0:
                    s = s + no_prev
                m = jnp.max(s, axis=0, keepdims=True)
                p = jnp.exp(s - m)
                den = jnp.sum(p, axis=0, keepdims=True)
                pv = jnp.dot(vt_win[h * HEAD_DIM:(h + 1) * HEAD_DIM, :], p.astype(BF16),
                             preferred_element_type=F32)
                outs.append(pv / den)
                lses.append(jnp.broadcast_to(m + jnp.log(den), (HEAD_DIM, blk)))
            o_t = jnp.concatenate(outs, axis=0).T
            lse_t = jnp.concatenate(lses, axis=0).T
            dst = rows if dil == 1 else pl.ds(j * blk * dil + r, blk, stride=dil)
            for c in range(width // LANE):
                o_ref[c, dst, :] = o_t[:, c * LANE:(c + 1) * LANE]
                lse_ref[c, dst, :] = lse_t[:, c * LANE:(c + 1) * LANE]
            k_prev, vt_prev = k_cur, vt_cur


def _mixer_a_group(za, bias_a, g, dil, batch, seq):
    sub_len = seq // dil
    assert sub_len % A_BLOCK == 0
    tq = min(sub_len, A_TOKENS_PER_STEP // dil)
    assert tq % A_BLOCK == 0
    nsub = tq // A_BLOCK
    gw = 3 * A_OUT
    out_spec = pl.BlockSpec((A_OUT // LANE, tq * dil, LANE), lambda b, t: (0, b * (sub_len // tq) + t, 0))
    return pl.pallas_call(
        functools.partial(_mixer_a_kernel, dil=dil, nsub=nsub),
        grid=(batch, sub_len // tq),
        in_specs=[pl.BlockSpec((1, dil, tq, gw), lambda b, t: (b, 0, t, 0)),
                  pl.BlockSpec((1, dil, A_BLOCK, gw), lambda b, t: (b, 0, jnp.maximum(t * nsub - 1, 0), 0)),
                  pl.BlockSpec((A_HPG, 1, 2 * A_BLOCK, A_BLOCK), lambda b, t: (g, 0, 0, 0))],
        out_specs=[out_spec, out_spec],
        out_shape=[jax.ShapeDtypeStruct((A_OUT // LANE, batch * seq, LANE), F32)] * 2,
        compiler_params=_cparams(("arbitrary", "arbitrary")),
        name=f"mixer_a_d{dil}",
    )(za, za, bias_a)


SUBLANES = 8


def _mixer_b_kernel(z_ref, cw_ref, cb_ref, wax_ref, bax_ref, lam_ref, y_ref,
                    xs_scr, a_scr, b_scr, h_scr, *, ts):
    w = LRU_WIDTH

    @pl.when(pl.program_id(1) == 0)
    def _():
        xs_scr[0:SUBLANES, :] = jnp.zeros((SUBLANES, w), F32)
        h_scr[...] = jnp.zeros((1, w), F32)

    x = z_ref[0, :, 0:w].astype(F32)
    xs_scr[SUBLANES:SUBLANES + ts, :] = x
    xc = cb_ref[...]
    for i in range(CONV_WIDTH):
        off = SUBLANES - (CONV_WIDTH - 1) + i
        xc = xc + xs_scr[off:off + ts, :] * cw_ref[i:i + 1, :]
    xs_scr[0:SUBLANES, :] = xs_scr[ts:ts + SUBLANES, :]

    ra = jnp.dot(xc.astype(BF16), wax_ref[...], preferred_element_type=F32) + bax_ref[...]
    r = jax.nn.sigmoid(ra[:, :w])
    ig = jax.nn.sigmoid(ra[:, w:])
    nl = -lam_ref[...]
    softplus = jnp.maximum(nl, 0.0) + jnp.log1p(jnp.exp(-jnp.abs(nl)))
    log_a = -LRU_C * r * softplus
    a = jnp.exp(log_a)
    a_scr[...] = a
    b_scr[...] = jnp.sqrt(-jnp.tanh(log_a) * (a * a + 1.0)) * (ig * xc)

    row = lax.broadcasted_iota(jnp.int32, (SUBLANES, w), 0)

    def body(c, h):
        i = pl.multiple_of(c * SUBLANES, SUBLANES)
        a = a_scr[pl.ds(i, SUBLANES), :]
        b = b_scr[pl.ds(i, SUBLANES), :]
        for s in (1, 2, 4):
            keep = row >= s
            a_sh = jnp.where(keep, pltpu.roll(a, s, 0), 1.0)
            b_sh = jnp.where(keep, pltpu.roll(b, s, 0), 0.0)
            b = a * b_sh + b
            a = a * a_sh
        hs = a * h + b
        b_scr[pl.ds(i, SUBLANES), :] = hs
        return hs[SUBLANES - 1:SUBLANES, :]

    h_scr[...] = lax.fori_loop(0, ts // SUBLANES, body, h_scr[...])

    gb = z_ref[0, :, w:2 * w].astype(F32)
    cdf = 0.5 * (1.0 + jnp.tanh(math.sqrt(2.0 / math.pi) * (gb + 0.044715 * (gb * gb * gb))))
    y_ref[0] = (b_scr[...] * (gb * cdf)).astype(y_ref.dtype)


def _block_diag(wblocks):
    nb, di, do = wblocks.shape
    eye = jnp.eye(nb, dtype=wblocks.dtype)
    return (eye[:, None, :, None] * wblocks[:, :, None, :]).reshape(nb * di, nb * do)


def _mixer_b(zb, conv_w, conv_b, lru_wa, lru_ba, lru_wx, lru_bx, lru_lam, batch, seq, ts=512):
    w = LRU_WIDTH
    wax = jnp.concatenate([_block_diag(lru_wa), _block_diag(lru_wx)], axis=1).astype(BF16)
    bax = jnp.concatenate([lru_ba, lru_bx]).reshape(1, 2 * w)
    y = pl.pallas_call(
        functools.partial(_mixer_b_kernel, ts=ts),
        grid=(batch, seq // ts),
        in_specs=[pl.BlockSpec((1, ts, 2 * w), lambda b, t: (b, t, 0)),
                  _const_spec((CONV_WIDTH, w)), _const_spec((1, w)),
                  _const_spec((w, 2 * w)), _const_spec((1, 2 * w)), _const_spec((1, w))],
        out_specs=pl.BlockSpec((1, ts, w), lambda b, t: (b, t, 0)),
        out_shape=jax.ShapeDtypeStruct((batch, seq, w), BF16),
        scratch_shapes=[pltpu.VMEM((ts + SUBLANES, w), F32), pltpu.VMEM((ts, w), F32),
                        pltpu.VMEM((ts, w), F32), pltpu.VMEM((1, w), F32)],
        compiler_params=_cparams(("arbitrary", "arbitrary")),
        name="mixer_b",
    )(zb.reshape(batch, seq, 2 * w), conv_w, conv_b.reshape(1, w), wax, bax, lru_lam.reshape(1, w))
    return y.reshape(batch * seq, w)


HEAD_PAIR = LANE // HEAD_DIM


def _mixer_c_kernel(q_ref, k_ref, vt_ref, bias_ref, o_ref, kmean_scr, sel_scr, s_scr, *, nblk, ndb):
    blk = MOBA_BLOCK
    qb = pl.program_id(2)

    @pl.when(qb == 0)
    def _():
        for n in range(nblk):
            kf = k_ref[0, n * blk:(n + 1) * blk, :].astype(F32)
            kmean_scr[n:n + 1, :] = jnp.sum(kf, axis=0, keepdims=True) * (1.0 / blk)

    q = q_ref[0]
    km = kmean_scr[...]
    km_hi = km.astype(BF16)
    km_lo = (km - km_hi.astype(F32)).astype(BF16)
    lane = lax.broadcasted_iota(jnp.int32, q.shape, 1)
    blk_id = lax.broadcasted_iota(jnp.int32, (nblk, blk), 0)
    past = blk_id < qb
    qhs = []
    for hh in range(HEAD_PAIR):
        qh = jnp.where((lane >= hh * HEAD_DIM) & (lane < (hh + 1) * HEAD_DIM), q, jnp.zeros_like(q))
        qhs.append(qh)

        gate = (lax.dot_general(km_hi, qh, NT_DIMS, preferred_element_type=F32)
                + lax.dot_general(km_lo, qh, NT_DIMS, preferred_element_type=F32))
        gate = jnp.where(past, gate, NEG_INF)
        sel = jnp.zeros((nblk, blk), F32)
        for _ in range(min(MOBA_TOPK, nblk - 1)):
            top = jnp.max(gate, axis=0, keepdims=True)
            first = jnp.min(jnp.where(gate == top, blk_id, nblk), axis=0, keepdims=True)
            pick = blk_id == first
            sel = jnp.where(pick, 1.0, sel)
            gate = jnp.where(pick, -jnp.inf, gate)
        sel_scr[hh] = jnp.where(past, sel, jnp.where(blk_id == qb, 1.0, 0.0))

    npairs = lax.shift_right_logical(qb + 2, 1)
    fold = lambda t: t.reshape(blk // SUBLANES, SUBLANES, blk)

    def scores(i, mx):
        start = pl.multiple_of(i * (2 * blk), 2 * blk)
        out = []
        for hh in range(HEAD_PAIR):
            s = lax.dot_general(k_ref[0, pl.ds(start, 2 * blk), :], qhs[hh], NT_DIMS, preferred_element_type=F32)
            m8 = mx[hh]
            for u in range(2):
                kb = 2 * i + u
                su = s[u * blk:(u + 1) * blk] + bias_ref[hh, jnp.clip(qb - kb, 0, ndb)]
                su = jnp.where(sel_scr[hh, pl.ds(kb, 1), :] > 0.0, su, NEG_INF)
                s_scr[hh, i, u * blk:(u + 1) * blk, :] = su
                m8 = jnp.maximum(m8, jnp.max(fold(su), axis=0))
            out.append(m8)
        return tuple(out)

    mx = lax.fori_loop(0, npairs, scores, (jnp.full((SUBLANES, blk), NEG_INF, F32),) * HEAD_PAIR)
    ms = [jnp.max(m8, axis=0, keepdims=True) for m8 in mx]

    def weigh(i, carry):
        out = []
        for hh in range(HEAD_PAIR):
            rows = slice(hh * HEAD_DIM, (hh + 1) * HEAD_DIM)
            l8, acc = carry[2 * hh:2 * hh + 2]
            for u in range(2):
                p = jnp.exp(s_scr[hh, i, u * blk:(u + 1) * blk, :] - ms[hh])
                l8 = l8 + jnp.sum(fold(p), axis=0)
                acc = acc + jnp.dot(vt_ref[0, 2 * i + u, rows, :], p.astype(BF16), preferred_element_type=F32)
            out += [l8, acc]
        return tuple(out)

    init = (jnp.zeros((SUBLANES, blk), F32), jnp.zeros((HEAD_DIM, blk), F32)) * HEAD_PAIR
    res = lax.fori_loop(0, npairs, weigh, init)
    out_t = jnp.concatenate([res[2 * hh + 1] / jnp.sum(res[2 * hh], axis=0, keepdims=True)
                             for hh in range(HEAD_PAIR)], axis=0)
    o_ref[0] = out_t.T.astype(o_ref.dtype)


def _mixer_c(zc, vt, bias_c, batch, seq, ndb):
    blk = MOBA_BLOCK
    assert seq % (2 * blk) == 0
    nblk = seq // blk
    npair = C_HEADS // HEAD_PAIR
    z = zc.reshape(batch, seq, 2 * C_WIDTH)
    o = pl.pallas_call(
        functools.partial(_mixer_c_kernel, nblk=nblk, ndb=ndb),
        grid=(batch, npair, nblk),
        in_specs=[pl.BlockSpec((1, blk, LANE), lambda b, hp, qb: (b, qb, hp)),
                  pl.BlockSpec((1, seq, LANE), lambda b, hp, qb: (b, 0, npair + hp)),
                  pl.BlockSpec((1, nblk, LANE, blk), lambda b, hp, qb: (b, 0, hp, 0)),
                  pl.BlockSpec((HEAD_PAIR, ndb + 1, blk, blk), lambda b, hp, qb: (hp, 0, 0, 0))],
        out_specs=pl.BlockSpec((1, blk, LANE), lambda b, hp, qb: (b, qb, hp)),
        out_shape=jax.ShapeDtypeStruct((batch, seq, C_WIDTH), BF16),
        scratch_shapes=[pltpu.VMEM((nblk, LANE), F32), pltpu.VMEM((HEAD_PAIR, nblk, blk), F32),
                        pltpu.VMEM((HEAD_PAIR, nblk // 2, 2 * blk, blk), F32)],
        compiler_params=_cparams(("arbitrary", "arbitrary", "arbitrary")),
        name="mixer_c",
    )(z, z, vt, bias_c)
    return o.reshape(batch * seq, C_WIDTH)


def _merge_kernel(x_ref, o1_ref, o2_ref, o3_ref, l1_ref, l2_ref, l3_ref, ob_ref, oc_ref, zg_ref,
                  pa_ref, pb_ref, pc_ref, wo_ref, out_ref):
    d = D_MODEL
    halves = lambda ref: jnp.concatenate([ref[c] for c in range(ref.shape[0])], axis=-1)
    l1, l2, l3 = halves(l1_ref), halves(l2_ref), halves(l3_ref)
    m = jnp.maximum(jnp.maximum(l1, l2), l3)
    e1, e2, e3 = jnp.exp(l1 - m), jnp.exp(l2 - m), jnp.exp(l3 - m)
    o_a = (e1 * halves(o1_ref) + e2 * halves(o2_ref) + e3 * halves(o3_ref)) / (e1 + e2 + e3)

    def branch(o, p_ref, k):
        gate = jax.nn.sigmoid(zg_ref[:, k * d:(k + 1) * d].astype(F32))
        return gate * jnp.dot(o, p_ref[...], preferred_element_type=F32)

    merged = branch(o_a.astype(BF16), pa_ref, 0) + branch(ob_ref[...], pb_ref, 1) + branch(oc_ref[...], pc_ref, 2)
    out_ref[...] = x_ref[...] + jnp.dot(merged.astype(BF16), wo_ref[...], preferred_element_type=F32)


def _merge(x2d, oa, lse, ob, oc, zg, p_a, p_b, p_c, w_out, tm=512):
    n = x2d.shape[0]
    row = lambda width: pl.BlockSpec((tm, width), lambda i: (i, 0))
    return pl.pallas_call(
        _merge_kernel,
        grid=(n // tm,),
        in_specs=[row(D_MODEL)] + [pl.BlockSpec((A_OUT // LANE, tm, LANE), lambda i: (0, i, 0))] * 6
                 + [row(LRU_WIDTH), row(C_WIDTH), row(3 * D_MODEL),
                  _const_spec((A_OUT, D_MODEL)), _const_spec((LRU_WIDTH, D_MODEL)),
                  _const_spec((C_WIDTH, D_MODEL)), _const_spec((D_MODEL, D_MODEL))],
        out_specs=row(D_MODEL),
        out_shape=jax.ShapeDtypeStruct((n, D_MODEL), F32),
        compiler_params=_cparams(("arbitrary",)),
        name="merge",
    )(x2d, *oa, *lse, ob, oc, zg, p_a.astype(BF16), p_b.astype(BF16), p_c.astype(BF16), w_out.astype(BF16))


def _rms(x, g):
    return x * lax.rsqrt(jnp.mean(x * x, axis=-1, keepdims=True) + EPS) * g


def _ffn_kernel(x_ref, g_ref, wgu_ref, wd_ref, gf_ref, out_ref, h_scr, act_scr, *, final_norm):
    x = x_ref[...]
    h_scr[...] = _rms(x, g_ref[...]).astype(BF16)
    for off, w in _col_chunks(0, FFN_HIDDEN, 256):
        gate = jnp.dot(h_scr[...], wgu_ref[:, off:off + w], preferred_element_type=F32)
        up = jnp.dot(h_scr[...], wgu_ref[:, FFN_HIDDEN + off:FFN_HIDDEN + off + w], preferred_element_type=F32)
        act_scr[:, off:off + w] = (gate * jax.nn.sigmoid(gate) * up).astype(BF16)
    y = x + jnp.dot(act_scr[...], wd_ref[...], preferred_element_type=F32)
    if final_norm:
        y = _rms(y, gf_ref[...])
    out_ref[...] = y


def _ffn(x2d, g_ffn, w_gu, w_down, g_final, final_norm, tm=512):
    n = x2d.shape[0]
    row = pl.BlockSpec((tm, D_MODEL), lambda i: (i, 0))
    return pl.pallas_call(
        functools.partial(_ffn_kernel, final_norm=final_norm),
        grid=(n // tm,),
        in_specs=[row, _const_spec((1, D_MODEL)), _const_spec((D_MODEL, 2 * FFN_HIDDEN)),
                  _const_spec((FFN_HIDDEN, D_MODEL)), _const_spec((1, D_MODEL))],
        out_specs=row,
        out_shape=jax.ShapeDtypeStruct((n, D_MODEL), F32),
        scratch_shapes=[pltpu.VMEM((tm, D_MODEL), BF16), pltpu.VMEM((tm, FFN_HIDDEN), BF16)],
        compiler_params=_cparams(("arbitrary",)),
        name="ffn",
    )(x2d, g_ffn.reshape(1, D_MODEL), w_gu.astype(BF16), w_down.astype(BF16), g_final.reshape(1, D_MODEL))


def kernel(x, rel_bias, g_mix, w_in, conv_w, conv_b, lru_wa, lru_ba, lru_wx, lru_bx, lru_lam,
           p_a, p_b, p_c, w_out, g_ffn, w_gu, w_down, g_final):
    batch, seq, d = x.shape
    assert d == D_MODEL and w_in.shape[-1] == IN_COLS
    depth = w_in.shape[0]
    nblk = seq // MOBA_BLOCK
    ndb = _moba_far_blocks(nblk)

    bias_a = _build_bias(rel_bias, _bucket_index_a(), A_HEADS, 0, A_HPG)
    bias_c = _build_bias(rel_bias, _bucket_index_c(ndb), C_HEADS, A_HEADS, None)

    x2d = x.reshape(batch * seq, d)
    for l in range(depth):
        *za, zb, zc, vt, zg = _in_proj(x2d, g_mix[l], w_in[l], batch, seq)
        oa, lse = zip(*[_mixer_a_group(za[g], bias_a, g, dil, batch, seq)
                        for g, (_, dil) in enumerate(A_GROUPS)])
        ob = _mixer_b(zb, conv_w[l], conv_b[l], lru_wa[l], lru_ba[l], lru_wx[l], lru_bx[l], lru_lam[l],
                      batch, seq)
        oc = _mixer_c(zc, vt, bias_c, batch, seq, ndb)
        x2d = _merge(x2d, oa, lse, ob, oc, zg, p_a[l], p_b[l], p_c[l], w_out[l])
        x2d = _ffn(x2d, g_ffn[l], w_gu[l], w_down[l], g_final, final_norm=(l == depth - 1))
    return x2d.reshape(batch, seq, d)
```

```python
import functools
import math

import numpy as np
import jax
import jax.numpy as jnp
from jax import lax
from jax.experimental import pallas as pl
from jax.experimental.pallas import tpu as pltpu

F32 = jnp.float32
BF16 = jnp.bfloat16

D_MODEL = 1024
HEAD_DIM = 64
SCALE = HEAD_DIM ** -0.5
NEG_INF = -1e30
EPS = 1e-6
A_GROUPS = ((128, 1), (512, 4), (2048, 16))
A_HPG = 4
A_HEADS = A_HPG * len(A_GROUPS)
A_WIDTH = A_HEADS * HEAD_DIM
A_OUT = A_HPG * HEAD_DIM
A_BLOCK = 128
LRU_WIDTH = D_MODEL // 2
LRU_BLOCKS = 8
CONV_WIDTH = 4
LRU_C = 8.0
C_HEADS = 8
C_WIDTH = C_HEADS * HEAD_DIM
MOBA_BLOCK = 256
MOBA_TOPK = 3
REL_BUCKETS = 32
REL_MAX_DIST = 2048
FFN_HIDDEN = 2816
IN_COLS = 3 * A_WIDTH + 2 * LRU_WIDTH + 3 * C_WIDTH + 3 * D_MODEL
V_C_OFF = 3 * A_WIDTH + 2 * LRU_WIDTH + 2 * C_WIDTH

LANE = 128
VMEM_LIMIT = 56 * 1024 * 1024

NT_DIMS = (((1,), (1,)), ((), ()))


def _cparams(sem, vmem=VMEM_LIMIT):
    return pltpu.CompilerParams(dimension_semantics=sem, vmem_limit_bytes=vmem)


def _const_spec(shape):
    nd = len(shape)
    return pl.BlockSpec(shape, lambda *_: (0,) * nd, pipeline_mode=pl.Buffered(1))


def _rel_bucket_np(dist):
    max_exact = REL_BUCKETS // 2
    d = np.maximum(dist, 0)
    df = np.maximum(d, 1).astype(np.float32)
    large = max_exact + (np.log(df / np.float32(max_exact)) / np.float32(math.log(REL_MAX_DIST / max_exact))
                         * np.float32(REL_BUCKETS - max_exact)).astype(np.int32)
    large = np.minimum(large, REL_BUCKETS - 1)
    return np.where(d < max_exact, d, large).astype(np.int32)


def _bucket_index_a():
    qi = np.arange(A_BLOCK)[None, :] + A_BLOCK
    kj = np.arange(2 * A_BLOCK)[:, None]
    delta = qi - kj
    mats = []
    for window, dil in A_GROUPS:
        band = (delta >= 0) & (delta <= window // dil)
        mats.append(np.where(band, _rel_bucket_np(delta * dil), -1))
    return np.stack(mats).astype(np.int32)


def _moba_far_blocks(nblk):
    for db in range(1, nblk + 1):
        lo = db * MOBA_BLOCK - (MOBA_BLOCK - 1)
        if np.all(_rel_bucket_np(np.arange(lo, nblk * MOBA_BLOCK)) == REL_BUCKETS - 1):
            return db
    return nblk


def _bucket_index_c(ndb):
    k = np.arange(MOBA_BLOCK)[:, None]
    q = np.arange(MOBA_BLOCK)[None, :]
    mats = []
    for db in range(ndb + 1):
        dist = db * MOBA_BLOCK + q - k
        b = _rel_bucket_np(dist)
        if db == 0:
            b = np.where(dist >= 0, b, -1)
        mats.append(b)
    return np.stack(mats).astype(np.int32)


def _bias_kernel(tab_ref, idx_ref, out_ref, *, head_off):
    h = pl.program_id(0) + head_off
    idx = idx_ref[0]
    out = jnp.full(idx.shape, NEG_INF, F32)
    for b in range(REL_BUCKETS):
        out = jnp.where(idx == b, tab_ref[b, h], out)
    out_ref[0, 0] = out


def _build_bias(rel_bias, idx, n_heads, head_off, heads_per_idx_group):
    n_mats = idx.shape[0] if heads_per_idx_group is None else 1
    r, c = idx.shape[-2:]
    if heads_per_idx_group is None:
        idx_map = lambda h, m: (m, 0, 0)
    else:
        idx_map = lambda h, m: (h // heads_per_idx_group, 0, 0)
    return pl.pallas_call(
        functools.partial(_bias_kernel, head_off=head_off),
        grid=(n_heads, n_mats),
        in_specs=[pl.BlockSpec(memory_space=pltpu.SMEM),
                  pl.BlockSpec((1, r, c), idx_map)],
        out_specs=pl.BlockSpec((1, 1, r, c), lambda h, m: (h, m, 0, 0)),
        out_shape=jax.ShapeDtypeStruct((n_heads, n_mats, r, c), F32),
        compiler_params=_cparams(("arbitrary", "arbitrary")),
        name="rel_bias_build",
    )(rel_bias, jnp.asarray(idx))


def _col_chunks(start, stop, width=512):
    out = []
    while start < stop:
        w = min(width, stop - start)
        out.append((start, w))
        start += w
    return out


def _in_proj_kernel(x_ref, g_ref, w_ref, wvt_ref, za0_ref, za1_ref, za2_ref, zb_ref, zc_ref, vt_ref, zg_ref,
                    h_scr, dil_scr, *, tm):
    x = x_ref[...]
    ms = jnp.mean(x * x, axis=-1, keepdims=True)
    h_scr[...] = (x * lax.rsqrt(ms + EPS) * g_ref[...]).astype(BF16)

    def proj(base, width):
        return jnp.dot(h_scr[...], w_ref[:, base:base + width], preferred_element_type=F32)

    def seg(out_ref, base, width, scaled_cols, chunk=512):
        assert scaled_cols % chunk == 0
        for off, w in _col_chunks(0, width, chunk):
            r = proj(base + off, w)
            if off < scaled_cols:
                r = r * SCALE
            out_ref[:, off:off + w] = r.astype(out_ref.dtype)

    gw = 3 * A_OUT
    for g, (za_ref, (_, dil)) in enumerate(zip((za0_ref, za1_ref, za2_ref), A_GROUPS)):
        q = proj(g * gw, A_OUT) * SCALE
        kv = proj(g * gw + A_OUT, 2 * A_OUT)
        if dil == 1:
            za_ref[0, 0, :, :A_OUT] = q.astype(BF16)
            za_ref[0, 0, :, A_OUT:] = kv.astype(BF16)
        else:
            qkv = (q[:, :LANE], q[:, LANE:]) + tuple(kv[:, c * LANE:(c + 1) * LANE] for c in range(4))
            for c, part in enumerate(qkv):
                dil_scr[c] = part
            for r in range(dil):
                for c in range(len(qkv)):
                    za_ref[0, r, :, c * LANE:(c + 1) * LANE] = (
                        dil_scr[c, pl.ds(r, tm // dil, stride=dil), :].astype(BF16))

    wa, wb, wc = 3 * A_WIDTH, 2 * LRU_WIDTH, 2 * C_WIDTH
    seg(zb_ref, wa, wb, 0)
    seg(zc_ref, wa + wb, wc, C_WIDTH)
    seg(zg_ref, wa + wb + wc, 3 * D_MODEL, 0)
    vt_ref[0] = lax.dot_general(wvt_ref[...], h_scr[...], NT_DIMS, preferred_element_type=F32).astype(BF16)


def _in_proj(x2d, g, w_in, batch, seq, tm=512):
    n = x2d.shape[0]
    grp = lambda k: jnp.concatenate([w_in[:, s * A_WIDTH + k * A_OUT:s * A_WIDTH + (k + 1) * A_OUT]
                                     for s in range(3)], axis=1)
    w_main = jnp.concatenate([grp(k) for k in range(len(A_GROUPS))]
                             + [w_in[:, 3 * A_WIDTH:V_C_OFF], w_in[:, V_C_OFF + C_WIDTH:]], axis=1).astype(BF16)
    wvt = w_in[:, V_C_OFF:V_C_OFF + C_WIDTH].T.astype(BF16)
    wb, wc, wg = 2 * LRU_WIDTH, 2 * C_WIDTH, 3 * D_MODEL
    gw = 3 * A_OUT
    tiles_per_seq = seq // tm
    row = lambda i: (i, 0)
    seq_tile = lambda i: (i // tiles_per_seq, 0, i % tiles_per_seq, 0)
    za_specs = [pl.BlockSpec((1, dil, tm // dil, gw), seq_tile) for _, dil in A_GROUPS]
    za_shapes = [jax.ShapeDtypeStruct((batch, dil, seq // dil, gw), BF16) for _, dil in A_GROUPS]
    return pl.pallas_call(
        functools.partial(_in_proj_kernel, tm=tm),
        grid=(n // tm,),
        in_specs=[pl.BlockSpec((tm, D_MODEL), row),
                  _const_spec((1, D_MODEL)),
                  _const_spec(w_main.shape),
                  _const_spec(wvt.shape)],
        out_specs=za_specs + [pl.BlockSpec((tm, wb), row),
                              pl.BlockSpec((tm, wc), row),
                              pl.BlockSpec((1, C_WIDTH, tm),
                                           lambda i: (i // tiles_per_seq, 0, i % tiles_per_seq)),
                              pl.BlockSpec((tm, wg), row)],
        out_shape=za_shapes + [jax.ShapeDtypeStruct((n, wb), BF16),
                               jax.ShapeDtypeStruct((n, wc), BF16),
                               jax.ShapeDtypeStruct((batch, C_WIDTH, seq), BF16),
                               jax.ShapeDtypeStruct((n, wg), BF16)],
        scratch_shapes=[pltpu.VMEM((tm, D_MODEL), BF16), pltpu.VMEM((gw // LANE, tm, LANE), F32)],
        compiler_params=_cparams(("arbitrary",)),
        name="in_proj",
    )(x2d, g.reshape(1, D_MODEL), w_main, wvt)


A_TOKENS_PER_STEP = 2048


def _mixer_a_kernel(z_ref, zp_ref, bias_ref, o_ref, lse_ref, *, dil, nsub):
    t = pl.program_id(1)
    blk, width = A_BLOCK, A_OUT
    kcols, vcols = slice(width, 2 * width), slice(2 * width, 3 * width)
    lane_head = lax.broadcasted_iota(jnp.int32, (blk, width), 1) // HEAD_DIM
    in_prev = (lax.broadcasted_iota(jnp.int32, (2 * blk, blk), 0) < blk).astype(F32)
    no_prev = in_prev * jnp.where(t == 0, NEG_INF, 0.0)
    transpose_v = lambda v: v.astype(F32).T.astype(BF16)

    for r in range(dil):
        k_prev = zp_ref[0, r, :, kcols]
        vt_prev = transpose_v(zp_ref[0, r, :, vcols])
        for j in range(nsub):
            rows = slice(j * blk, (j + 1) * blk)
            q = z_ref[0, r, rows, 0:width]
            k_cur = z_ref[0, r, rows, kcols]
            vt_cur = transpose_v(z_ref[0, r, rows, vcols])
            k_win = jnp.concatenate([k_prev, k_cur], axis=0)
            vt_win = jnp.concatenate([vt_prev, vt_cur], axis=1)
            outs, lses = [], []
            for h in range(A_HPG):
                qh = jnp.where(lane_head == h, q, jnp.zeros_like(q))
                s = lax.dot_general(k_win, qh, NT_DIMS, preferred_element_type=F32) + bias_ref[h, 0]
                if j == 0:
                    s = s + no_prev
                m = jnp.max(s, axis=0, keepdims=True)
                p = jnp.exp(s - m)
                den = jnp.sum(p, axis=0, keepdims=True)
                pv = jnp.dot(vt_win[h * HEAD_DIM:(h + 1) * HEAD_DIM, :], p.astype(BF16),
                             preferred_element_type=F32)
                outs.append(pv / den)
                lses.append(jnp.broadcast_to(m + jnp.log(den), (HEAD_DIM, blk)))
            o_t = jnp.concatenate(outs, axis=0).T
            lse_t = jnp.concatenate(lses, axis=0).T
            dst = rows if dil == 1 else pl.ds(j * blk * dil + r, blk, stride=dil)
            for c in range(width // LANE):
                o_ref[c, dst, :] = o_t[:, c * LANE:(c + 1) * LANE]
                lse_ref[c, dst, :] = lse_t[:, c * LANE:(c + 1) * LANE]
            k_prev, vt_prev = k_cur, vt_cur


def _mixer_a_group(za, bias_a, g, dil, batch, seq):
    sub_len = seq // dil
    assert sub_len % A_BLOCK == 0
    tq = min(sub_len, A_TOKENS_PER_STEP // dil)
    assert tq % A_BLOCK == 0
    nsub = tq // A_BLOCK
    gw = 3 * A_OUT
    out_spec = pl.BlockSpec((A_OUT // LANE, tq * dil, LANE), lambda b, t: (0, b * (sub_len // tq) + t, 0))
    return pl.pallas_call(
        functools.partial(_mixer_a_kernel, dil=dil, nsub=nsub),
        grid=(batch, sub_len // tq),
        in_specs=[pl.BlockSpec((1, dil, tq, gw), lambda b, t: (b, 0, t, 0)),
                  pl.BlockSpec((1, dil, A_BLOCK, gw), lambda b, t: (b, 0, jnp.maximum(t * nsub - 1, 0), 0)),
                  pl.BlockSpec((A_HPG, 1, 2 * A_BLOCK, A_BLOCK), lambda b, t: (g, 0, 0, 0))],
        out_specs=[out_spec, out_spec],
        out_shape=[jax.ShapeDtypeStruct((A_OUT // LANE, batch * seq, LANE), F32)] * 2,
        compiler_params=_cparams(("arbitrary", "arbitrary")),
        name=f"mixer_a_d{dil}",
    )(za, za, bias_a)


SUBLANES = 8


def _mixer_b_kernel(z_ref, cw_ref, cb_ref, wax_ref, bax_ref, lam_ref, y_ref,
                    xs_scr, a_scr, b_scr, h_scr, *, ts):
    w = LRU_WIDTH

    @pl.when(pl.program_id(1) == 0)
    def _():
        xs_scr[0:SUBLANES, :] = jnp.zeros((SUBLANES, w), F32)
        h_scr[...] = jnp.zeros((1, w), F32)

    x = z_ref[0, :, 0:w].astype(F32)
    xs_scr[SUBLANES:SUBLANES + ts, :] = x
    xc = cb_ref[...]
    for i in range(CONV_WIDTH):
        off = SUBLANES - (CONV_WIDTH - 1) + i
        xc = xc + xs_scr[off:off + ts, :] * cw_ref[i:i + 1, :]
    xs_scr[0:SUBLANES, :] = xs_scr[ts:ts + SUBLANES, :]

    ra = jnp.dot(xc.astype(BF16), wax_ref[...], preferred_element_type=F32) + bax_ref[...]
    r = jax.nn.sigmoid(ra[:, :w])
    ig = jax.nn.sigmoid(ra[:, w:])
    nl = -lam_ref[...]
    softplus = jnp.maximum(nl, 0.0) + jnp.log1p(jnp.exp(-jnp.abs(nl)))
    log_a = -LRU_C * r * softplus
    a = jnp.exp(log_a)
    a_scr[...] = a
    b_scr[...] = jnp.sqrt(-jnp.tanh(log_a) * (a * a + 1.0)) * (ig * xc)

    row = lax.broadcasted_iota(jnp.int32, (SUBLANES, w), 0)

    def body(c, h):
        i = pl.multiple_of(c * SUBLANES, SUBLANES)
        a = a_scr[pl.ds(i, SUBLANES), :]
        b = b_scr[pl.ds(i, SUBLANES), :]
        for s in (1, 2, 4):
            keep = row >= s
            a_sh = jnp.where(keep, pltpu.roll(a, s, 0), 1.0)
            b_sh = jnp.where(keep, pltpu.roll(b, s, 0), 0.0)
            b = a * b_sh + b
            a = a * a_sh
        hs = a * h + b
        b_scr[pl.ds(i, SUBLANES), :] = hs
        return hs[SUBLANES - 1:SUBLANES, :]

    h_scr[...] = lax.fori_loop(0, ts // SUBLANES, body, h_scr[...])

    gb = z_ref[0, :, w:2 * w].astype(F32)
    cdf = 0.5 * (1.0 + jnp.tanh(math.sqrt(2.0 / math.pi) * (gb + 0.044715 * (gb * gb * gb))))
    y_ref[0] = (b_scr[...] * (gb * cdf)).astype(y_ref.dtype)


def _block_diag(wblocks):
    nb, di, do = wblocks.shape
    eye = jnp.eye(nb, dtype=wblocks.dtype)
    return (eye[:, None, :, None] * wblocks[:, :, None, :]).reshape(nb * di, nb * do)


def _mixer_b(zb, conv_w, conv_b, lru_wa, lru_ba, lru_wx, lru_bx, lru_lam, batch, seq, ts=512):
    w = LRU_WIDTH
    wax = jnp.concatenate([_block_diag(lru_wa), _block_diag(lru_wx)], axis=1).astype(BF16)
    bax = jnp.concatenate([lru_ba, lru_bx]).reshape(1, 2 * w)
    y = pl.pallas_call(
        functools.partial(_mixer_b_kernel, ts=ts),
        grid=(batch, seq // ts),
        in_specs=[pl.BlockSpec((1, ts, 2 * w), lambda b, t: (b, t, 0)),
                  _const_spec((CONV_WIDTH, w)), _const_spec((1, w)),
                  _const_spec((w, 2 * w)), _const_spec((1, 2 * w)), _const_spec((1, w))],
        out_specs=pl.BlockSpec((1, ts, w), lambda b, t: (b, t, 0)),
        out_shape=jax.ShapeDtypeStruct((batch, seq, w), BF16),
        scratch_shapes=[pltpu.VMEM((ts + SUBLANES, w), F32), pltpu.VMEM((ts, w), F32),
                        pltpu.VMEM((ts, w), F32), pltpu.VMEM((1, w), F32)],
        compiler_params=_cparams(("arbitrary", "arbitrary")),
        name="mixer_b",
    )(zb.reshape(batch, seq, 2 * w), conv_w, conv_b.reshape(1, w), wax, bax, lru_lam.reshape(1, w))
    return y.reshape(batch * seq, w)


HEAD_PAIR = LANE // HEAD_DIM


def _mixer_c_prologue(q_ref, k_ref, qm_scr, sel_scr, *, nblk):
    blk = MOBA_BLOCK
    seq = nblk * blk
    kmean = jnp.concatenate(
        [jnp.sum(k_ref[0, n * blk:(n + 1) * blk, :].astype(F32), axis=0, keepdims=True) for n in range(nblk)],
        axis=0) * (1.0 / blk)
    km_hi = kmean.astype(BF16)
    km_lo = (kmean - km_hi.astype(F32)).astype(BF16)
    q = q_ref[0]
    lane = lax.broadcasted_iota(jnp.int32, q.shape, 1)
    blk_id = lax.broadcasted_iota(jnp.int32, (nblk, seq), 0)
    q_blk = lax.broadcasted_iota(jnp.int32, (nblk, seq), 1) // blk
    past = blk_id < q_blk
    for hh in range(HEAD_PAIR):
        qm = jnp.where((lane >= hh * HEAD_DIM) & (lane < (hh + 1) * HEAD_DIM), q, jnp.zeros_like(q))
        qm_scr[hh] = qm
        gate = (lax.dot_general(km_hi, qm, NT_DIMS, preferred_element_type=F32)
                + lax.dot_general(km_lo, qm, NT_DIMS, preferred_element_type=F32))
        gate = jnp.where(past, gate, NEG_INF)
        sel = jnp.zeros((nblk, seq), F32)
        for _ in range(min(MOBA_TOPK, nblk - 1)):
            top = jnp.max(gate, axis=0, keepdims=True)
            first = jnp.min(jnp.where(gate == top, blk_id, nblk), axis=0, keepdims=True)
            pick = blk_id == first
            sel = jnp.where(pick, 1.0, sel)
            gate = jnp.where(pick, -jnp.inf, gate)
        sel_scr[hh] = jnp.where(past, sel, 0.0)


def _mixer_c_block(c, k_ref, vt_ref, bias_ref, o_ref, qm_scr, sel_scr, s_scr, p_scr, *, ndb):
    blk = MOBA_BLOCK
    nk = (c + 1) * blk
    qcols = slice(c * blk, (c + 1) * blk)
    fold = lambda t: t.reshape(blk // SUBLANES, SUBLANES, blk)
    outs = []
    for hh in range(HEAD_PAIR):
        qm = qm_scr[hh, qcols, :]
        picked = [sel_scr[hh, kb:kb + 1, qcols] > 0.0 for kb in range(c)]
        m8 = jnp.full((SUBLANES, blk), NEG_INF, F32)
        for k0 in range(0, c + 1, 2):
            nb = min(2, c + 1 - k0)
            s = lax.dot_general(k_ref[0, k0 * blk:(k0 + nb) * blk, :], qm, NT_DIMS, preferred_element_type=F32)
            for kb in range(k0, k0 + nb):
                su = s[(kb - k0) * blk:(kb - k0 + 1) * blk] + bias_ref[hh, min(c - kb, ndb)]
                s_scr[hh, kb * blk:(kb + 1) * blk, :] = su
                t8 = jnp.max(fold(su), axis=0)
                m8 = jnp.maximum(m8, t8 if kb == c else jnp.where(picked[kb], t8, NEG_INF))
        m = jnp.max(m8, axis=0, keepdims=True)
        l8 = jnp.zeros((SUBLANES, blk), F32)
        for kb in range(c + 1):
            mk = m if kb == c else jnp.where(picked[kb], m, -NEG_INF)
            p = jnp.exp(s_scr[hh, kb * blk:(kb + 1) * blk, :] - mk)
            l8 = l8 + jnp.sum(fold(p), axis=0)
            p_scr[hh, kb * blk:(kb + 1) * blk, :] = p.astype(BF16)
        pv = jnp.dot(vt_ref[0, hh * HEAD_DIM:(hh + 1) * HEAD_DIM, :nk], p_scr[hh, :nk, :],
                     preferred_element_type=F32)
        outs.append(pv / jnp.sum(l8, axis=0, keepdims=True))
    o_ref[0] = jnp.concatenate(outs, axis=0).T.astype(o_ref.dtype)


def _mixer_c_kernel(q_ref, k_ref, vt_ref, bias_ref, o_ref, qm_scr, sel_scr, s_scr, p_scr, *, nblk, ndb):
    qb = pl.program_id(2)

    @pl.when(qb == 0)
    def _():
        _mixer_c_prologue(q_ref, k_ref, qm_scr, sel_scr, nblk=nblk)

    for c in range(nblk):
        pl.when(qb == c)(functools.partial(_mixer_c_block, c, k_ref, vt_ref, bias_ref, o_ref,
                                           qm_scr, sel_scr, s_scr, p_scr, ndb=ndb))


def _mixer_c(zc, vt, bias_c, batch, seq, ndb):
    blk = MOBA_BLOCK
    assert seq % blk == 0
    nblk = seq // blk
    npair = C_HEADS // HEAD_PAIR
    z = zc.reshape(batch, seq, 2 * C_WIDTH)
    o = pl.pallas_call(
        functools.partial(_mixer_c_kernel, nblk=nblk, ndb=ndb),
        grid=(batch, npair, nblk),
        in_specs=[pl.BlockSpec((1, seq, LANE), lambda b, hp, qb: (b, 0, hp)),
                  pl.BlockSpec((1, seq, LANE), lambda b, hp, qb: (b, 0, npair + hp)),
                  pl.BlockSpec((1, LANE, seq), lambda b, hp, qb: (b, hp, 0)),
                  pl.BlockSpec((HEAD_PAIR, ndb + 1, blk, blk), lambda b, hp, qb: (hp, 0, 0, 0))],
        out_specs=pl.BlockSpec((1, blk, LANE), lambda b, hp, qb: (b, qb, hp)),
        out_shape=jax.ShapeDtypeStruct((batch, seq, C_WIDTH), BF16),
        scratch_shapes=[pltpu.VMEM((HEAD_PAIR, seq, LANE), BF16), pltpu.VMEM((HEAD_PAIR, nblk, seq), F32),
                        pltpu.VMEM((HEAD_PAIR, seq, blk), F32), pltpu.VMEM((HEAD_PAIR, seq, blk), BF16)],
        compiler_params=_cparams(("arbitrary", "arbitrary", "arbitrary")),
        name="mixer_c",
    )(z, z, vt, bias_c)
    return o.reshape(batch * seq, C_WIDTH)


def _merge_kernel(x_ref, o1_ref, o2_ref, o3_ref, l1_ref, l2_ref, l3_ref, ob_ref, oc_ref, zg_ref,
                  pa_ref, pb_ref, pc_ref, wo_ref, out_ref):
    d = D_MODEL
    halves = lambda ref: jnp.concatenate([ref[c] for c in range(ref.shape[0])], axis=-1)
    l1, l2, l3 = halves(l1_ref), halves(l2_ref), halves(l3_ref)
    m = jnp.maximum(jnp.maximum(l1, l2), l3)
    e1, e2, e3 = jnp.exp(l1 - m), jnp.exp(l2 - m), jnp.exp(l3 - m)
    o_a = (e1 * halves(o1_ref) + e2 * halves(o2_ref) + e3 * halves(o3_ref)) / (e1 + e2 + e3)

    def branch(o, p_ref, k):
        gate = jax.nn.sigmoid(zg_ref[:, k * d:(k + 1) * d].astype(F32))
        return gate * jnp.dot(o, p_ref[...], preferred_element_type=F32)

    merged = branch(o_a.astype(BF16), pa_ref, 0) + branch(ob_ref[...], pb_ref, 1) + branch(oc_ref[...], pc_ref, 2)
    out_ref[...] = x_ref[...] + jnp.dot(merged.astype(BF16), wo_ref[...], preferred_element_type=F32)


def _merge(x2d, oa, lse, ob, oc, zg, p_a, p_b, p_c, w_out, tm=512):
    n = x2d.shape[0]
    row = lambda width: pl.BlockSpec((tm, width), lambda i: (i, 0))
    return pl.pallas_call(
        _merge_kernel,
        grid=(n // tm,),
        in_specs=[row(D_MODEL)] + [pl.BlockSpec((A_OUT // LANE, tm, LANE), lambda i: (0, i, 0))] * 6
                 + [row(LRU_WIDTH), row(C_WIDTH), row(3 * D_MODEL),
                  _const_spec((A_OUT, D_MODEL)), _const_spec((LRU_WIDTH, D_MODEL)),
                  _const_spec((C_WIDTH, D_MODEL)), _const_spec((D_MODEL, D_MODEL))],
        out_specs=row(D_MODEL),
        out_shape=jax.ShapeDtypeStruct((n, D_MODEL), F32),
        compiler_params=_cparams(("arbitrary",)),
        name="merge",
    )(x2d, *oa, *lse, ob, oc, zg, p_a.astype(BF16), p_b.astype(BF16), p_c.astype(BF16), w_out.astype(BF16))


def _rms(x, g):
    return x * lax.rsqrt(jnp.mean(x * x, axis=-1, keepdims=True) + EPS) * g


def _ffn_kernel(x_ref, g_ref, wgu_ref, wd_ref, gf_ref, out_ref, h_scr, act_scr, *, final_norm):
    x = x_ref[...]
    h_scr[...] = _rms(x, g_ref[...]).astype(BF16)
    for off, w in _col_chunks(0, FFN_HIDDEN, 256):
        gate = jnp.dot(h_scr[...], wgu_ref[:, off:off + w], preferred_element_type=F32)
        up = jnp.dot(h_scr[...], wgu_ref[:, FFN_HIDDEN + off:FFN_HIDDEN + off + w], preferred_element_type=F32)
        act_scr[:, off:off + w] = (gate * jax.nn.sigmoid(gate) * up).astype(BF16)
    y = x + jnp.dot(act_scr[...], wd_ref[...], preferred_element_type=F32)
    if final_norm:
        y = _rms(y, gf_ref[...])
    out_ref[...] = y


def _ffn(x2d, g_ffn, w_gu, w_down, g_final, final_norm, tm=512):
    n = x2d.shape[0]
    row = pl.BlockSpec((tm, D_MODEL), lambda i: (i, 0))
    return pl.pallas_call(
        functools.partial(_ffn_kernel, final_norm=final_norm),
        grid=(n // tm,),
        in_specs=[row, _const_spec((1, D_MODEL)), _const_spec((D_MODEL, 2 * FFN_HIDDEN)),
                  _const_spec((FFN_HIDDEN, D_MODEL)), _const_spec((1, D_MODEL))],
        out_specs=row,
        out_shape=jax.ShapeDtypeStruct((n, D_MODEL), F32),
        scratch_shapes=[pltpu.VMEM((tm, D_MODEL), BF16), pltpu.VMEM((tm, FFN_HIDDEN), BF16)],
        compiler_params=_cparams(("arbitrary",)),
        name="ffn",
    )(x2d, g_ffn.reshape(1, D_MODEL), w_gu.astype(BF16), w_down.astype(BF16), g_final.reshape(1, D_MODEL))


def kernel(x, rel_bias, g_mix, w_in, conv_w, conv_b, lru_wa, lru_ba, lru_wx, lru_bx, lru_lam,
           p_a, p_b, p_c, w_out, g_ffn, w_gu, w_down, g_final):
    batch, seq, d = x.shape
    assert d == D_MODEL and w_in.shape[-1] == IN_COLS
    depth = w_in.shape[0]
    nblk = seq // MOBA_BLOCK
    ndb = _moba_far_blocks(nblk)

    bias_a = _build_bias(rel_bias, _bucket_index_a(), A_HEADS, 0, A_HPG)
    bias_c = _build_bias(rel_bias, _bucket_index_c(ndb), C_HEADS, A_HEADS, None)

    x2d = x.reshape(batch * seq, d)
    for l in range(depth):
        *za, zb, zc, vt, zg = _in_proj(x2d, g_mix[l], w_in[l], batch, seq)
        oa, lse = zip(*[_mixer_a_group(za[g], bias_a, g, dil, batch, seq)
                        for g, (_, dil) in enumerate(A_GROUPS)])
        ob = _mixer_b(zb, conv_w[l], conv_b[l], lru_wa[l], lru_ba[l], lru_wx[l], lru_bx[l], lru_lam[l],
                      batch, seq)
        oc = _mixer_c(zc, vt, bias_c, batch, seq, ndb)
        x2d = _merge(x2d, oa, lse, ob, oc, zg, p_a[l], p_b[l], p_c[l], w_out[l])
        x2d = _ffn(x2d, g_ffn[l], w_gu[l], w_down[l], g_final, final_norm=(l == depth - 1))
    return x2d.reshape(batch, seq, d)
```

```python
import functools
import math

import numpy as np
import jax
import jax.numpy as jnp
from jax import lax
from jax.experimental import pallas as pl
from jax.experimental.pallas import tpu as pltpu

F32 = jnp.float32
BF16 = jnp.bfloat16

D_MODEL = 1024
HEAD_DIM = 64
SCALE = HEAD_DIM ** -0.5
NEG_INF = -1e30
EPS = 1e-6
A_GROUPS = ((128, 1), (512, 4), (2048, 16))
A_HPG = 4
A_HEADS = A_HPG * len(A_GROUPS)
A_WIDTH = A_HEADS * HEAD_DIM
A_OUT = A_HPG * HEAD_DIM
A_BLOCK = 128
LRU_WIDTH = D_MODEL // 2
LRU_BLOCKS = 8
CONV_WIDTH = 4
LRU_C = 8.0
C_HEADS = 8
C_WIDTH = C_HEADS * HEAD_DIM
MOBA_BLOCK = 256
MOBA_TOPK = 3
REL_BUCKETS = 32
REL_MAX_DIST = 2048
FFN_HIDDEN = 2816
IN_COLS = 3 * A_WIDTH + 2 * LRU_WIDTH + 3 * C_WIDTH + 3 * D_MODEL
V_C_OFF = 3 * A_WIDTH + 2 * LRU_WIDTH + 2 * C_WIDTH

LANE = 128
V_ROWS = HEAD_DIM + 16
VMEM_LIMIT = 56 * 1024 * 1024

NT_DIMS = (((1,), (1,)), ((), ()))


def _cparams(sem, vmem=VMEM_LIMIT):
    return pltpu.CompilerParams(dimension_semantics=sem, vmem_limit_bytes=vmem)


def _const_spec(shape):
    nd = len(shape)
    return pl.BlockSpec(shape, lambda *_: (0,) * nd, pipeline_mode=pl.Buffered(1))


def _rel_bucket_np(dist):
    max_exact = REL_BUCKETS // 2
    d = np.maximum(dist, 0)
    df = np.maximum(d, 1).astype(np.float32)
    large = max_exact + (np.log(df / np.float32(max_exact)) / np.float32(math.log(REL_MAX_DIST / max_exact))
                         * np.float32(REL_BUCKETS - max_exact)).astype(np.int32)
    large = np.minimum(large, REL_BUCKETS - 1)
    return np.where(d < max_exact, d, large).astype(np.int32)


def _bucket_index_a():
    qi = np.arange(A_BLOCK)[None, :] + A_BLOCK
    kj = np.arange(2 * A_BLOCK)[:, None]
    delta = qi - kj
    mats = []
    for window, dil in A_GROUPS:
        band = (delta >= 0) & (delta <= window // dil)
        mats.append(np.where(band, _rel_bucket_np(delta * dil), -1))
    return np.stack(mats).astype(np.int32)


def _moba_far_blocks(nblk):
    for db in range(1, nblk + 1):
        lo = db * MOBA_BLOCK - (MOBA_BLOCK - 1)
        if np.all(_rel_bucket_np(np.arange(lo, nblk * MOBA_BLOCK)) == REL_BUCKETS - 1):
            return db
    return nblk


def _bucket_index_c(ndb):
    k = np.arange(MOBA_BLOCK)[:, None]
    q = np.arange(MOBA_BLOCK)[None, :]
    mats = []
    for db in range(ndb + 1):
        dist = db * MOBA_BLOCK + q - k
        b = _rel_bucket_np(dist)
        if db == 0:
            b = np.where(dist >= 0, b, -1)
        mats.append(b)
    return np.stack(mats).astype(np.int32)


def _bias_kernel(tab_ref, idx_ref, out_ref, *, head_off):
    h = pl.program_id(0) + head_off
    idx = idx_ref[0]
    out = jnp.full(idx.shape, NEG_INF, F32)
    for b in range(REL_BUCKETS):
        out = jnp.where(idx == b, tab_ref[b, h], out)
    out_ref[0, 0] = out


def _build_bias(rel_bias, idx, n_heads, head_off, heads_per_idx_group):
    n_mats = idx.shape[0] if heads_per_idx_group is None else 1
    r, c = idx.shape[-2:]
    if heads_per_idx_group is None:
        idx_map = lambda h, m: (m, 0, 0)
    else:
        idx_map = lambda h, m: (h // heads_per_idx_group, 0, 0)
    return pl.pallas_call(
        functools.partial(_bias_kernel, head_off=head_off),
        grid=(n_heads, n_mats),
        in_specs=[pl.BlockSpec(memory_space=pltpu.SMEM),
                  pl.BlockSpec((1, r, c), idx_map)],
        out_specs=pl.BlockSpec((1, 1, r, c), lambda h, m: (h, m, 0, 0)),
        out_shape=jax.ShapeDtypeStruct((n_heads, n_mats, r, c), F32),
        compiler_params=_cparams(("arbitrary", "arbitrary")),
        name="rel_bias_build",
    )(rel_bias, jnp.asarray(idx))


def _col_chunks(start, stop, width=512):
    out = []
    while start < stop:
        w = min(width, stop - start)
        out.append((start, w))
        start += w
    return out


def _in_proj_kernel(x_ref, g_ref, w_ref, wvt_ref, za0_ref, za1_ref, za2_ref, zb_ref, zc_ref, vt_ref, zg_ref,
                    h_scr, dil_scr, *, tm):
    x = x_ref[...]
    ms = jnp.mean(x * x, axis=-1, keepdims=True)
    h_scr[...] = (x * lax.rsqrt(ms + EPS) * g_ref[...]).astype(BF16)

    def proj(base, width):
        return jnp.dot(h_scr[...], w_ref[:, base:base + width], preferred_element_type=F32)

    def seg(out_ref, base, width, scaled_cols, chunk=512):
        assert scaled_cols % chunk == 0
        for off, w in _col_chunks(0, width, chunk):
            r = proj(base + off, w)
            if off < scaled_cols:
                r = r * SCALE
            out_ref[:, off:off + w] = r.astype(out_ref.dtype)

    gw = 3 * A_OUT
    for g, (za_ref, (_, dil)) in enumerate(zip((za0_ref, za1_ref, za2_ref), A_GROUPS)):
        q = proj(g * gw, A_OUT) * SCALE
        kv = proj(g * gw + A_OUT, 2 * A_OUT)
        if dil == 1:
            za_ref[0, 0, :, :A_OUT] = q.astype(BF16)
            za_ref[0, 0, :, A_OUT:] = kv.astype(BF16)
        else:
            qkv = (q[:, :LANE], q[:, LANE:]) + tuple(kv[:, c * LANE:(c + 1) * LANE] for c in range(4))
            for c, part in enumerate(qkv):
                dil_scr[c] = part
            for r in range(dil):
                for c in range(len(qkv)):
                    za_ref[0, r, :, c * LANE:(c + 1) * LANE] = (
                        dil_scr[c, pl.ds(r, tm // dil, stride=dil), :].astype(BF16))

    wa, wb, wc = 3 * A_WIDTH, 2 * LRU_WIDTH, 2 * C_WIDTH
    seg(zb_ref, wa, wb, 0)
    seg(zc_ref, wa + wb, wc, C_WIDTH)
    seg(zg_ref, wa + wb + wc, 3 * D_MODEL, 0)
    vt = lax.dot_general(wvt_ref[...], h_scr[...], NT_DIMS, preferred_element_type=F32).astype(BF16)
    for h in range(C_HEADS):
        vt_ref[0, h * V_ROWS:h * V_ROWS + HEAD_DIM, :] = vt[h * HEAD_DIM:(h + 1) * HEAD_DIM]
        vt_ref[0, h * V_ROWS + HEAD_DIM:(h + 1) * V_ROWS, :] = jnp.ones((V_ROWS - HEAD_DIM, tm), BF16)


def _in_proj(x2d, g, w_in, batch, seq, tm=512):
    n = x2d.shape[0]
    grp = lambda k: jnp.concatenate([w_in[:, s * A_WIDTH + k * A_OUT:s * A_WIDTH + (k + 1) * A_OUT]
                                     for s in range(3)], axis=1)
    w_main = jnp.concatenate([grp(k) for k in range(len(A_GROUPS))]
                             + [w_in[:, 3 * A_WIDTH:V_C_OFF], w_in[:, V_C_OFF + C_WIDTH:]], axis=1).astype(BF16)
    wvt = w_in[:, V_C_OFF:V_C_OFF + C_WIDTH].T.astype(BF16)
    wb, wc, wg = 2 * LRU_WIDTH, 2 * C_WIDTH, 3 * D_MODEL
    gw = 3 * A_OUT
    tiles_per_seq = seq // tm
    row = lambda i: (i, 0)
    seq_tile = lambda i: (i // tiles_per_seq, 0, i % tiles_per_seq, 0)
    za_specs = [pl.BlockSpec((1, dil, tm // dil, gw), seq_tile) for _, dil in A_GROUPS]
    za_shapes = [jax.ShapeDtypeStruct((batch, dil, seq // dil, gw), BF16) for _, dil in A_GROUPS]
    return pl.pallas_call(
        functools.partial(_in_proj_kernel, tm=tm),
        grid=(n // tm,),
        in_specs=[pl.BlockSpec((tm, D_MODEL), row),
                  _const_spec((1, D_MODEL)),
                  _const_spec(w_main.shape),
                  _const_spec(wvt.shape)],
        out_specs=za_specs + [pl.BlockSpec((tm, wb), row),
                              pl.BlockSpec((tm, wc), row),
                              pl.BlockSpec((1, C_HEADS * V_ROWS, tm),
                                           lambda i: (i // tiles_per_seq, 0, i % tiles_per_seq)),
                              pl.BlockSpec((tm, wg), row)],
        out_shape=za_shapes + [jax.ShapeDtypeStruct((n, wb), BF16),
                               jax.ShapeDtypeStruct((n, wc), BF16),
                               jax.ShapeDtypeStruct((batch, C_HEADS * V_ROWS, seq), BF16),
                               jax.ShapeDtypeStruct((n, wg), BF16)],
        scratch_shapes=[pltpu.VMEM((tm, D_MODEL), BF16), pltpu.VMEM((gw // LANE, tm, LANE), F32)],
        compiler_params=_cparams(("arbitrary",)),
        name="in_proj",
    )(x2d, g.reshape(1, D_MODEL), w_main, wvt)


A_TOKENS_PER_STEP = 2048


def _mixer_a_kernel(z_ref, zp_ref, bias_ref, o_ref, lse_ref, *, dil, nsub):
    t = pl.program_id(1)
    blk, width = A_BLOCK, A_OUT
    kcols, vcols = slice(width, 2 * width), slice(2 * width, 3 * width)
    lane_head = lax.broadcasted_iota(jnp.int32, (blk, width), 1) // HEAD_DIM
    in_prev = (lax.broadcasted_iota(jnp.int32, (2 * blk, blk), 0) < blk).astype(F32)
    no_prev = in_prev * jnp.where(t == 0, NEG_INF, 0.0)
    transpose_v = lambda v: v.astype(F32).T.astype(BF16)

    for r in range(dil):
        k_prev = zp_ref[0, r, :, kcols]
        vt_prev = transpose_v(zp_ref[0, r, :, vcols])
        for j in range(nsub):
            rows = slice(j * blk, (j + 1) * blk)
            q = z_ref[0, r, rows, 0:width]
            k_cur = z_ref[0, r, rows, kcols]
            vt_cur = transpose_v(z_ref[0, r, rows, vcols])
            k_win = jnp.concatenate([k_prev, k_cur], axis=0)
            vt_win = jnp.concatenate([vt_prev, vt_cur], axis=1)
            outs, lses = [], []
            for h in range(A_HPG):
                qh = jnp.where(lane_head == h, q, jnp.zeros_like(q))
                s = lax.dot_general(k_win, qh, NT_DIMS, preferred_element_type=F32) + bias_ref[h, 0]
                if j == 0:
                    s = s + no_prev
                m = jnp.max(s, axis=0, keepdims=True)
                p = jnp.exp(s - m)
                den = jnp.sum(p, axis=0, keepdims=True)
                pv = jnp.dot(vt_win[h * HEAD_DIM:(h + 1) * HEAD_DIM, :], p.astype(BF16),
                             preferred_element_type=F32)
                outs.append(pv / den)
                lses.append(jnp.broadcast_to(m + jnp.log(den), (HEAD_DIM, blk)))
            o_t = jnp.concatenate(outs, axis=0).T
            lse_t = jnp.concatenate(lses, axis=0).T
            dst = rows if dil == 1 else pl.ds(j * blk * dil + r, blk, stride=dil)
            for c in range(width // LANE):
                o_ref[c, dst, :] = o_t[:, c * LANE:(c + 1) * LANE]
                lse_ref[c, dst, :] = lse_t[:, c * LANE:(c + 1) * LANE]
            k_prev, vt_prev = k_cur, vt_cur


def _mixer_a_group(za, bias_a, g, dil, batch, seq):
    sub_len = seq // dil
    assert sub_len % A_BLOCK == 0
    tq = min(sub_len, A_TOKENS_PER_STEP // dil)
    assert tq % A_BLOCK == 0
    nsub = tq // A_BLOCK
    gw = 3 * A_OUT
    out_spec = pl.BlockSpec((A_OUT // LANE, tq * dil, LANE), lambda b, t: (0, b * (sub_len // tq) + t, 0))
    return pl.pallas_call(
        functools.partial(_mixer_a_kernel, dil=dil, nsub=nsub),
        grid=(batch, sub_len // tq),
        in_specs=[pl.BlockSpec((1, dil, tq, gw), lambda b, t: (b, 0, t, 0)),
                  pl.BlockSpec((1, dil, A_BLOCK, gw), lambda b, t: (b, 0, jnp.maximum(t * nsub - 1, 0), 0)),
                  pl.BlockSpec((A_HPG, 1, 2 * A_BLOCK, A_BLOCK), lambda b, t: (g, 0, 0, 0))],
        out_specs=[out_spec, out_spec],
        out_shape=[jax.ShapeDtypeStruct((A_OUT // LANE, batch * seq, LANE), F32)] * 2,
        compiler_params=_cparams(("arbitrary", "arbitrary")),
        name=f"mixer_a_d{dil}",
    )(za, za, bias_a)


SUBLANES = 8


def _mixer_b_kernel(z_ref, cw_ref, cb_ref, wax_ref, bax_ref, lam_ref, y_ref,
                    xs_scr, a_scr, b_scr, h_scr, *, ts):
    w = LRU_WIDTH

    @pl.when(pl.program_id(1) == 0)
    def _():
        xs_scr[0:SUBLANES, :] = jnp.zeros((SUBLANES, w), F32)
        h_scr[...] = jnp.zeros((1, w), F32)

    x = z_ref[0, :, 0:w].astype(F32)
    xs_scr[SUBLANES:SUBLANES + ts, :] = x
    xc = cb_ref[...]
    for i in range(CONV_WIDTH):
        off = SUBLANES - (CONV_WIDTH - 1) + i
        xc = xc + xs_scr[off:off + ts, :] * cw_ref[i:i + 1, :]
    xs_scr[0:SUBLANES, :] = xs_scr[ts:ts + SUBLANES, :]

    ra = jnp.dot(xc.astype(BF16), wax_ref[...], preferred_element_type=F32) + bax_ref[...]
    r = jax.nn.sigmoid(ra[:, :w])
    ig = jax.nn.sigmoid(ra[:, w:])
    nl = -lam_ref[...]
    softplus = jnp.maximum(nl, 0.0) + jnp.log1p(jnp.exp(-jnp.abs(nl)))
    log_a = -LRU_C * r * softplus
    a = jnp.exp(log_a)
    a_scr[...] = a
    b_scr[...] = jnp.sqrt(-jnp.tanh(log_a) * (a * a + 1.0)) * (ig * xc)

    row = lax.broadcasted_iota(jnp.int32, (SUBLANES, w), 0)

    def body(c, h):
        i = pl.multiple_of(c * SUBLANES, SUBLANES)
        a = a_scr[pl.ds(i, SUBLANES), :]
        b = b_scr[pl.ds(i, SUBLANES), :]
        for s in (1, 2, 4):
            keep = row >= s
            a_sh = jnp.where(keep, pltpu.roll(a, s, 0), 1.0)
            b_sh = jnp.where(keep, pltpu.roll(b, s, 0), 0.0)
            b = a * b_sh + b
            a = a * a_sh
        hs = a * h + b
        b_scr[pl.ds(i, SUBLANES), :] = hs
        return hs[SUBLANES - 1:SUBLANES, :]

    h_scr[...] = lax.fori_loop(0, ts // SUBLANES, body, h_scr[...])

    gb = z_ref[0, :, w:2 * w].astype(F32)
    cdf = 0.5 * (1.0 + jnp.tanh(math.sqrt(2.0 / math.pi) * (gb + 0.044715 * (gb * gb * gb))))
    y_ref[0] = (b_scr[...] * (gb * cdf)).astype(y_ref.dtype)


def _block_diag(wblocks):
    nb, di, do = wblocks.shape
    eye = jnp.eye(nb, dtype=wblocks.dtype)
    return (eye[:, None, :, None] * wblocks[:, :, None, :]).reshape(nb * di, nb * do)


def _mixer_b(zb, conv_w, conv_b, lru_wa, lru_ba, lru_wx, lru_bx, lru_lam, batch, seq, ts=512):
    w = LRU_WIDTH
    wax = jnp.concatenate([_block_diag(lru_wa), _block_diag(lru_wx)], axis=1).astype(BF16)
    bax = jnp.concatenate([lru_ba, lru_bx]).reshape(1, 2 * w)
    y = pl.pallas_call(
        functools.partial(_mixer_b_kernel, ts=ts),
        grid=(batch, seq // ts),
        in_specs=[pl.BlockSpec((1, ts, 2 * w), lambda b, t: (b, t, 0)),
                  _const_spec((CONV_WIDTH, w)), _const_spec((1, w)),
                  _const_spec((w, 2 * w)), _const_spec((1, 2 * w)), _const_spec((1, w))],
        out_specs=pl.BlockSpec((1, ts, w), lambda b, t: (b, t, 0)),
        out_shape=jax.ShapeDtypeStruct((batch, seq, w), BF16),
        scratch_shapes=[pltpu.VMEM((ts + SUBLANES, w), F32), pltpu.VMEM((ts, w), F32),
                        pltpu.VMEM((ts, w), F32), pltpu.VMEM((1, w), F32)],
        compiler_params=_cparams(("arbitrary", "arbitrary")),
        name="mixer_b",
    )(zb.reshape(batch, seq, 2 * w), conv_w, conv_b.reshape(1, w), wax, bax, lru_lam.reshape(1, w))
    return y.reshape(batch * seq, w)


HEAD_PAIR = LANE // HEAD_DIM


def _mixer_c_prologue(q_ref, k_ref, qm_scr, sel_scr, *, nblk):
    blk = MOBA_BLOCK
    seq = nblk * blk
    kmean = jnp.concatenate(
        [jnp.sum(k_ref[0, n * blk:(n + 1) * blk, :].astype(F32), axis=0, keepdims=True) for n in range(nblk)],
        axis=0) * (1.0 / blk)
    km_hi = kmean.astype(BF16)
    km_lo = (kmean - km_hi.astype(F32)).astype(BF16)
    q = q_ref[0]
    lane = lax.broadcasted_iota(jnp.int32, q.shape, 1)
    blk_id = lax.broadcasted_iota(jnp.int32, (nblk, seq), 0)
    q_blk = lax.broadcasted_iota(jnp.int32, (nblk, seq), 1) // blk
    past = blk_id < q_blk
    for hh in range(HEAD_PAIR):
        qm = jnp.where((lane >= hh * HEAD_DIM) & (lane < (hh + 1) * HEAD_DIM), q, jnp.zeros_like(q))
        qm_scr[hh] = qm
        gate = (lax.dot_general(km_hi, qm, NT_DIMS, preferred_element_type=F32)
                + lax.dot_general(km_lo, qm, NT_DIMS, preferred_element_type=F32))
        gate = jnp.where(past, gate, NEG_INF)
        sel = jnp.zeros((nblk, seq), F32)
        for _ in range(min(MOBA_TOPK, nblk - 1)):
            top = jnp.max(gate, axis=0, keepdims=True)
            first = jnp.min(jnp.where(gate == top, blk_id, nblk), axis=0, keepdims=True)
            pick = blk_id == first
            sel = jnp.where(pick, 1.0, sel)
            gate = jnp.where(pick, -jnp.inf, gate)
        sel_scr[hh] = jnp.where(past, sel, 0.0)


def _mixer_c_block(c, k_ref, vt_ref, bias_ref, o_ref, qm_scr, sel_scr, s_scr, p_scr, *, ndb):
    blk = MOBA_BLOCK
    nk = (c + 1) * blk
    qcols = slice(c * blk, (c + 1) * blk)
    fold = lambda t: t.reshape(blk // SUBLANES, SUBLANES, blk)
    heads = range(HEAD_PAIR)
    picked = [[sel_scr[hh, kb:kb + 1, qcols] > 0.0 for kb in range(c)] for hh in heads]

    def logits(hh):
        qm = qm_scr[hh, qcols, :]
        m8 = jnp.full((SUBLANES, blk), NEG_INF, F32)
        for k0 in range(0, c + 1, 2):
            nb = min(2, c + 1 - k0)
            s = lax.dot_general(k_ref[0, k0 * blk:(k0 + nb) * blk, :], qm, NT_DIMS, preferred_element_type=F32)
            for kb in range(k0, k0 + nb):
                su = s[(kb - k0) * blk:(kb - k0 + 1) * blk] + bias_ref[hh, min(c - kb, ndb)]
                s_scr[hh, kb * blk:(kb + 1) * blk, :] = su
                t8 = jnp.max(fold(su), axis=0)
                m8 = jnp.maximum(m8, t8 if kb == c else jnp.where(picked[hh][kb], t8, NEG_INF))
        return jnp.max(m8, axis=0, keepdims=True)

    def weights(hh, m):
        for kb in range(c + 1):
            mk = m if kb == c else jnp.where(picked[hh][kb], m, -NEG_INF)
            x = s_scr[hh, kb * blk:(kb + 1) * blk, :] - mk
            p_scr[hh, kb * blk:(kb + 1) * blk, :] = jnp.exp(x.astype(BF16))

    def weighted_v(hh):
        pv = jnp.dot(vt_ref[0, hh * V_ROWS:(hh + 1) * V_ROWS, :nk], p_scr[hh, :nk, :],
                     preferred_element_type=F32)
        return pv[:HEAD_DIM] / pv[HEAD_DIM:HEAD_DIM + 1]

    ms = [logits(hh) for hh in heads]
    outs = []
    for hh in heads:
        weights(hh, ms[hh])
        outs.append(weighted_v(hh))
    o_ref[0] = jnp.concatenate(outs, axis=0).T.astype(o_ref.dtype)


def _mixer_c_kernel(q_ref, k_ref, vt_ref, bias_ref, o_ref, qm_scr, sel_scr, s_scr, p_scr, *, nblk, ndb):
    qb = pl.program_id(2)

    @pl.when(qb == 0)
    def _():
        _mixer_c_prologue(q_ref, k_ref, qm_scr, sel_scr, nblk=nblk)

    for c in range(nblk):
        pl.when(qb == c)(functools.partial(_mixer_c_block, c, k_ref, vt_ref, bias_ref, o_ref,
                                           qm_scr, sel_scr, s_scr, p_scr, ndb=ndb))


def _mixer_c(zc, vt, bias_c, batch, seq, ndb):
    blk = MOBA_BLOCK
    assert seq % blk == 0
    nblk = seq // blk
    npair = C_HEADS // HEAD_PAIR
    z = zc.reshape(batch, seq, 2 * C_WIDTH)
    o = pl.pallas_call(
        functools.partial(_mixer_c_kernel, nblk=nblk, ndb=ndb),
        grid=(batch, npair, nblk),
        in_specs=[pl.BlockSpec((1, seq, LANE), lambda b, hp, qb: (b, 0, hp)),
                  pl.BlockSpec((1, seq, LANE), lambda b, hp, qb: (b, 0, npair + hp)),
                  pl.BlockSpec((1, HEAD_PAIR * V_ROWS, seq), lambda b, hp, qb: (b, hp, 0)),
                  pl.BlockSpec((HEAD_PAIR, ndb + 1, blk, blk), lambda b, hp, qb: (hp, 0, 0, 0))],
        out_specs=pl.BlockSpec((1, blk, LANE), lambda b, hp, qb: (b, qb, hp)),
        out_shape=jax.ShapeDtypeStruct((batch, seq, C_WIDTH), BF16),
        scratch_shapes=[pltpu.VMEM((HEAD_PAIR, seq, LANE), BF16), pltpu.VMEM((HEAD_PAIR, nblk, seq), F32),
                        pltpu.VMEM((HEAD_PAIR, seq, blk), F32), pltpu.VMEM((HEAD_PAIR, seq, blk), BF16)],
        compiler_params=_cparams(("arbitrary", "arbitrary", "arbitrary")),
        name="mixer_c",
    )(z, z, vt, bias_c)
    return o.reshape(batch * seq, C_WIDTH)


def _merge_kernel(x_ref, o1_ref, o2_ref, o3_ref, l1_ref, l2_ref, l3_ref, ob_ref, oc_ref, zg_ref,
                  pa_ref, pb_ref, pc_ref, wo_ref, out_ref):
    d = D_MODEL
    halves = lambda ref: jnp.concatenate([ref[c] for c in range(ref.shape[0])], axis=-1)
    l1, l2, l3 = halves(l1_ref), halves(l2_ref), halves(l3_ref)
    m = jnp.maximum(jnp.maximum(l1, l2), l3)
    e1, e2, e3 = jnp.exp(l1 - m), jnp.exp(l2 - m), jnp.exp(l3 - m)
    o_a = (e1 * halves(o1_ref) + e2 * halves(o2_ref) + e3 * halves(o3_ref)) / (e1 + e2 + e3)

    def branch(o, p_ref, k):
        gate = jax.nn.sigmoid(zg_ref[:, k * d:(k + 1) * d].astype(F32))
        return gate * jnp.dot(o, p_ref[...], preferred_element_type=F32)

    merged = branch(o_a.astype(BF16), pa_ref, 0) + branch(ob_ref[...], pb_ref, 1) + branch(oc_ref[...], pc_ref, 2)
    out_ref[...] = x_ref[...] + jnp.dot(merged.astype(BF16), wo_ref[...], preferred_element_type=F32)


def _merge(x2d, oa, lse, ob, oc, zg, p_a, p_b, p_c, w_out, tm=512):
    n = x2d.shape[0]
    row = lambda width: pl.BlockSpec((tm, width), lambda i: (i, 0))
    return pl.pallas_call(
        _merge_kernel,
        grid=(n // tm,),
        in_specs=[row(D_MODEL)] + [pl.BlockSpec((A_OUT // LANE, tm, LANE), lambda i: (0, i, 0))] * 6
                 + [row(LRU_WIDTH), row(C_WIDTH), row(3 * D_MODEL),
                  _const_spec((A_OUT, D_MODEL)), _const_spec((LRU_WIDTH, D_MODEL)),
                  _const_spec((C_WIDTH, D_MODEL)), _const_spec((D_MODEL, D_MODEL))],
        out_specs=row(D_MODEL),
        out_shape=jax.ShapeDtypeStruct((n, D_MODEL), F32),
        compiler_params=_cparams(("arbitrary",)),
        name="merge",
    )(x2d, *oa, *lse, ob, oc, zg, p_a.astype(BF16), p_b.astype(BF16), p_c.astype(BF16), w_out.astype(BF16))


def _rms(x, g):
    return x * lax.rsqrt(jnp.mean(x * x, axis=-1, keepdims=True) + EPS) * g


def _ffn_kernel(x_ref, g_ref, wgu_ref, wd_ref, gf_ref, out_ref, h_scr, act_scr, *, final_norm):
    x = x_ref[...]
    h_scr[...] = _rms(x, g_ref[...]).astype(BF16)
    for off, w in _col_chunks(0, FFN_HIDDEN, 256):
        gate = jnp.dot(h_scr[...], wgu_ref[:, off:off + w], preferred_element_type=F32)
        up = jnp.dot(h_scr[...], wgu_ref[:, FFN_HIDDEN + off:FFN_HIDDEN + off + w], preferred_element_type=F32)
        act_scr[:, off:off + w] = (gate * jax.nn.sigmoid(gate) * up).astype(BF16)
    y = x + jnp.dot(act_scr[...], wd_ref[...], preferred_element_type=F32)
    if final_norm:
        y = _rms(y, gf_ref[...])
    out_ref[...] = y


def _ffn(x2d, g_ffn, w_gu, w_down, g_final, final_norm, tm=512):
    n = x2d.shape[0]
    row = pl.BlockSpec((tm, D_MODEL), lambda i: (i, 0))
    return pl.pallas_call(
        functools.partial(_ffn_kernel, final_norm=final_norm),
        grid=(n // tm,),
        in_specs=[row, _const_spec((1, D_MODEL)), _const_spec((D_MODEL, 2 * FFN_HIDDEN)),
                  _const_spec((FFN_HIDDEN, D_MODEL)), _const_spec((1, D_MODEL))],
        out_specs=row,
        out_shape=jax.ShapeDtypeStruct((n, D_MODEL), F32),
        scratch_shapes=[pltpu.VMEM((tm, D_MODEL), BF16), pltpu.VMEM((tm, FFN_HIDDEN), BF16)],
        compiler_params=_cparams(("arbitrary",)),
        name="ffn",
    )(x2d, g_ffn.reshape(1, D_MODEL), w_gu.astype(BF16), w_down.astype(BF16), g_final.reshape(1, D_MODEL))


def kernel(x, rel_bias, g_mix, w_in, conv_w, conv_b, lru_wa, lru_ba, lru_wx, lru_bx, lru_lam,
           p_a, p_b, p_c, w_out, g_ffn, w_gu, w_down, g_final):
    batch, seq, d = x.shape
    assert d == D_MODEL and w_in.shape[-1] == IN_COLS
    depth = w_in.shape[0]
    nblk = seq // MOBA_BLOCK
    ndb = _moba_far_blocks(nblk)

    bias_a = _build_bias(rel_bias, _bucket_index_a(), A_HEADS, 0, A_HPG)
    bias_c = _build_bias(rel_bias, _bucket_index_c(ndb), C_HEADS, A_HEADS, None)

    x2d = x.reshape(batch * seq, d)
    for l in range(depth):
        *za, zb, zc, vt, zg = _in_proj(x2d, g_mix[l], w_in[l], batch, seq)
        oa, lse = zip(*[_mixer_a_group(za[g], bias_a, g, dil, batch, seq)
                        for g, (_, dil) in enumerate(A_GROUPS)])
        ob = _mixer_b(zb, conv_w[l], conv_b[l], lru_wa[l], lru_ba[l], lru_wx[l], lru_bx[l], lru_lam[l],
                      batch, seq)
        oc = _mixer_c(zc, vt, bias_c, batch, seq, ndb)
        x2d = _merge(x2d, oa, lse, ob, oc, zg, p_a[l], p_b[l], p_c[l], w_out[l])
        x2d = _ffn(x2d, g_ffn[l], w_gu[l], w_down[l], g_final, final_norm=(l == depth - 1))
    return x2d.reshape(batch, seq, d)
```

```python
import functools
import math

import numpy as np
import jax
import jax.numpy as jnp
from jax import lax
from jax.experimental import pallas as pl
from jax.experimental.pallas import tpu as pltpu

F32 = jnp.float32
BF16 = jnp.bfloat16

D_MODEL = 1024
HEAD_DIM = 64
SCALE = HEAD_DIM ** -0.5
NEG_INF = -1e30
EPS = 1e-6
A_GROUPS = ((128, 1), (512, 4), (2048, 16))
A_HPG = 4
A_HEADS = A_HPG * len(A_GROUPS)
A_WIDTH = A_HEADS * HEAD_DIM
A_OUT = A_HPG * HEAD_DIM
A_BLOCK = 128
LRU_WIDTH = D_MODEL // 2
LRU_BLOCKS = 8
CONV_WIDTH = 4
LRU_C = 8.0
C_HEADS = 8
C_WIDTH = C_HEADS * HEAD_DIM
MOBA_BLOCK = 256
MOBA_TOPK = 3
REL_BUCKETS = 32
REL_MAX_DIST = 2048
FFN_HIDDEN = 2816
IN_COLS = 3 * A_WIDTH + 2 * LRU_WIDTH + 3 * C_WIDTH + 3 * D_MODEL
V_C_OFF = 3 * A_WIDTH + 2 * LRU_WIDTH + 2 * C_WIDTH

LANE = 128
V_ROWS = HEAD_DIM + 16
VMEM_LIMIT = 56 * 1024 * 1024

NT_DIMS = (((1,), (1,)), ((), ()))


def _cparams(sem, vmem=VMEM_LIMIT):
    return pltpu.CompilerParams(dimension_semantics=sem, vmem_limit_bytes=vmem)


def _const_spec(shape):
    nd = len(shape)
    return pl.BlockSpec(shape, lambda *_: (0,) * nd, pipeline_mode=pl.Buffered(1))


def _rel_bucket_np(dist):
    max_exact = REL_BUCKETS // 2
    d = np.maximum(dist, 0)
    df = np.maximum(d, 1).astype(np.float32)
    large = max_exact + (np.log(df / np.float32(max_exact)) / np.float32(math.log(REL_MAX_DIST / max_exact))
                         * np.float32(REL_BUCKETS - max_exact)).astype(np.int32)
    large = np.minimum(large, REL_BUCKETS - 1)
    return np.where(d < max_exact, d, large).astype(np.int32)


def _bucket_index_a():
    qi = np.arange(A_BLOCK)[None, :] + A_BLOCK
    kj = np.arange(2 * A_BLOCK)[:, None]
    delta = qi - kj
    mats = []
    for window, dil in A_GROUPS:
        band = (delta >= 0) & (delta <= window // dil)
        mats.append(np.where(band, _rel_bucket_np(delta * dil), -1))
    return np.stack(mats).astype(np.int32)


def _moba_far_blocks(nblk):
    for db in range(1, nblk + 1):
        lo = db * MOBA_BLOCK - (MOBA_BLOCK - 1)
        if np.all(_rel_bucket_np(np.arange(lo, nblk * MOBA_BLOCK)) == REL_BUCKETS - 1):
            return db
    return nblk


def _bucket_index_c(ndb):
    k = np.arange(MOBA_BLOCK)[:, None]
    q = np.arange(MOBA_BLOCK)[None, :]
    mats = []
    for db in range(ndb + 1):
        dist = db * MOBA_BLOCK + q - k
        b = _rel_bucket_np(dist)
        if db == 0:
            b = np.where(dist >= 0, b, -1)
        mats.append(b)
    return np.stack(mats).astype(np.int32)


def _bias_kernel(tab_ref, idx_ref, out_ref, *, head_off):
    h = pl.program_id(0) + head_off
    idx = idx_ref[0]
    out = jnp.full(idx.shape, NEG_INF, F32)
    for b in range(REL_BUCKETS):
        out = jnp.where(idx == b, tab_ref[b, h], out)
    out_ref[0, 0] = out


def _build_bias(rel_bias, idx, n_heads, head_off, heads_per_idx_group):
    n_mats = idx.shape[0] if heads_per_idx_group is None else 1
    r, c = idx.shape[-2:]
    if heads_per_idx_group is None:
        idx_map = lambda h, m: (m, 0, 0)
    else:
        idx_map = lambda h, m: (h // heads_per_idx_group, 0, 0)
    return pl.pallas_call(
        functools.partial(_bias_kernel, head_off=head_off),
        grid=(n_heads, n_mats),
        in_specs=[pl.BlockSpec(memory_space=pltpu.SMEM),
                  pl.BlockSpec((1, r, c), idx_map)],
        out_specs=pl.BlockSpec((1, 1, r, c), lambda h, m: (h, m, 0, 0)),
        out_shape=jax.ShapeDtypeStruct((n_heads, n_mats, r, c), F32),
        compiler_params=_cparams(("arbitrary", "arbitrary")),
        name="rel_bias_build",
    )(rel_bias, jnp.asarray(idx))


def _col_chunks(start, stop, width=512):
    out = []
    while start < stop:
        w = min(width, stop - start)
        out.append((start, w))
        start += w
    return out


def _in_proj_kernel(x_ref, g_ref, w_ref, wvt_ref, za0_ref, za1_ref, za2_ref, zb_ref, zc_ref, vt_ref, zg_ref,
                    h_scr, dil_scr, *, tm):
    x = x_ref[...]
    ms = jnp.mean(x * x, axis=-1, keepdims=True)
    h_scr[...] = (x * lax.rsqrt(ms + EPS) * g_ref[...]).astype(BF16)

    def proj(base, width):
        return jnp.dot(h_scr[...], w_ref[:, base:base + width], preferred_element_type=F32)

    def seg(out_ref, base, width, scaled_cols, chunk=512):
        assert scaled_cols % chunk == 0
        for off, w in _col_chunks(0, width, chunk):
            r = proj(base + off, w)
            if off < scaled_cols:
                r = r * SCALE
            out_ref[:, off:off + w] = r.astype(out_ref.dtype)

    gw = 3 * A_OUT
    for g, (za_ref, (_, dil)) in enumerate(zip((za0_ref, za1_ref, za2_ref), A_GROUPS)):
        q = proj(g * gw, A_OUT) * SCALE
        kv = proj(g * gw + A_OUT, 2 * A_OUT)
        if dil == 1:
            za_ref[0, 0, :, :A_OUT] = q.astype(BF16)
            za_ref[0, 0, :, A_OUT:] = kv.astype(BF16)
        else:
            qkv = (q[:, :LANE], q[:, LANE:]) + tuple(kv[:, c * LANE:(c + 1) * LANE] for c in range(4))
            for c, part in enumerate(qkv):
                dil_scr[c] = part
            for r in range(dil):
                for c in range(len(qkv)):
                    za_ref[0, r, :, c * LANE:(c + 1) * LANE] = (
                        dil_scr[c, pl.ds(r, tm // dil, stride=dil), :].astype(BF16))

    wa, wb, wc = 3 * A_WIDTH, 2 * LRU_WIDTH, 2 * C_WIDTH
    seg(zb_ref, wa, wb, 0)
    seg(zc_ref, wa + wb, wc, C_WIDTH)
    seg(zg_ref, wa + wb + wc, 3 * D_MODEL, 0)
    vt = lax.dot_general(wvt_ref[...], h_scr[...], NT_DIMS, preferred_element_type=F32).astype(BF16)
    for h in range(C_HEADS):
        vt_ref[0, h * V_ROWS:h * V_ROWS + HEAD_DIM, :] = vt[h * HEAD_DIM:(h + 1) * HEAD_DIM]
        vt_ref[0, h * V_ROWS + HEAD_DIM:(h + 1) * V_ROWS, :] = jnp.ones((V_ROWS - HEAD_DIM, tm), BF16)


def _in_proj(x2d, g, w_in, batch, seq, tm=512):
    n = x2d.shape[0]
    grp = lambda k: jnp.concatenate([w_in[:, s * A_WIDTH + k * A_OUT:s * A_WIDTH + (k + 1) * A_OUT]
                                     for s in range(3)], axis=1)
    w_main = jnp.concatenate([grp(k) for k in range(len(A_GROUPS))]
                             + [w_in[:, 3 * A_WIDTH:V_C_OFF], w_in[:, V_C_OFF + C_WIDTH:]], axis=1).astype(BF16)
    wvt = w_in[:, V_C_OFF:V_C_OFF + C_WIDTH].T.astype(BF16)
    wb, wc, wg = 2 * LRU_WIDTH, 2 * C_WIDTH, 3 * D_MODEL
    gw = 3 * A_OUT
    tiles_per_seq = seq // tm
    row = lambda i: (i, 0)
    seq_tile = lambda i: (i // tiles_per_seq, 0, i % tiles_per_seq, 0)
    za_specs = [pl.BlockSpec((1, dil, tm // dil, gw), seq_tile) for _, dil in A_GROUPS]
    za_shapes = [jax.ShapeDtypeStruct((batch, dil, seq // dil, gw), BF16) for _, dil in A_GROUPS]
    return pl.pallas_call(
        functools.partial(_in_proj_kernel, tm=tm),
        grid=(n // tm,),
        in_specs=[pl.BlockSpec((tm, D_MODEL), row),
                  _const_spec((1, D_MODEL)),
                  _const_spec(w_main.shape),
                  _const_spec(wvt.shape)],
        out_specs=za_specs + [pl.BlockSpec((tm, wb), row),
                              pl.BlockSpec((tm, wc), row),
                              pl.BlockSpec((1, C_HEADS * V_ROWS, tm),
                                           lambda i: (i // tiles_per_seq, 0, i % tiles_per_seq)),
                              pl.BlockSpec((tm, wg), row)],
        out_shape=za_shapes + [jax.ShapeDtypeStruct((n, wb), BF16),
                               jax.ShapeDtypeStruct((n, wc), BF16),
                               jax.ShapeDtypeStruct((batch, C_HEADS * V_ROWS, seq), BF16),
                               jax.ShapeDtypeStruct((n, wg), BF16)],
        scratch_shapes=[pltpu.VMEM((tm, D_MODEL), BF16), pltpu.VMEM((gw // LANE, tm, LANE), F32)],
        compiler_params=_cparams(("arbitrary",)),
        name="in_proj",
    )(x2d, g.reshape(1, D_MODEL), w_main, wvt)


A_TOKENS_PER_STEP = 2048


def _mixer_a_kernel(z_ref, zp_ref, bias_ref, o_ref, lse_ref, *, dil, nsub):
    t = pl.program_id(1)
    blk, width = A_BLOCK, A_OUT
    kcols, vcols = slice(width, 2 * width), slice(2 * width, 3 * width)
    lane_head = lax.broadcasted_iota(jnp.int32, (blk, width), 1) // HEAD_DIM
    in_prev = (lax.broadcasted_iota(jnp.int32, (2 * blk, 2 * blk), 0) < blk).astype(F32)
    no_prev = in_prev * jnp.where(t == 0, NEG_INF, 0.0)
    transpose_v = lambda v: v.astype(F32).T.astype(BF16)

    ones_rows = jnp.ones((V_ROWS - HEAD_DIM, 2 * blk), BF16)
    windows = {}

    def window(r, j):
        if (r, j) not in windows:
            rows = slice(j * blk, (j + 1) * blk)
            if j == 0:
                k_prev, vt_prev = zp_ref[0, r, :, kcols], transpose_v(zp_ref[0, r, :, vcols])
            else:
                _, k_before, vt_before = window(r, j - 1)
                k_prev, vt_prev = k_before[blk:], vt_before[:, blk:]
            k_win = jnp.concatenate([k_prev, z_ref[0, r, rows, kcols]], axis=0)
            vt_win = jnp.concatenate([vt_prev, transpose_v(z_ref[0, r, rows, vcols])], axis=1)
            windows[(r, j)] = (z_ref[0, r, rows, 0:width], k_win, vt_win)
        return windows[(r, j)]

    def logits(r, j, hp):
        q, k_win, _ = window(r, j)
        heads = (2 * hp, 2 * hp + 1)
        q2 = jnp.concatenate([jnp.where(lane_head == h, q, jnp.zeros_like(q)) for h in heads], axis=0)
        bias2 = jnp.concatenate([bias_ref[h, 0] for h in heads], axis=1)
        s = lax.dot_general(k_win, q2, NT_DIMS, preferred_element_type=F32) + bias2
        return s + no_prev if j == 0 else s

    def attend(r, j, hp, s):
        vt_win = window(r, j)[2]
        m = jnp.max(s, axis=0, keepdims=True)
        p = jnp.exp((s - m).astype(BF16))
        vt_ext = jnp.concatenate([part for h in (2 * hp, 2 * hp + 1)
                                  for part in (vt_win[h * HEAD_DIM:(h + 1) * HEAD_DIM, :], ones_rows)], axis=0)
        pv = jnp.dot(vt_ext, p, preferred_element_type=F32)
        res = []
        for u in range(2):
            cols = slice(u * blk, (u + 1) * blk)
            den = pv[u * V_ROWS + HEAD_DIM:u * V_ROWS + HEAD_DIM + 1, cols]
            res.append((pv[u * V_ROWS:u * V_ROWS + HEAD_DIM, cols] / den,
                        jnp.broadcast_to(m[:, cols] + jnp.log(den), (HEAD_DIM, blk))))
        return res

    n_hp = A_HPG // 2
    units = [(r, j, hp) for r in range(dil) for j in range(nsub) for hp in range(n_hp)]
    s_next = logits(*units[0])
    outs, lses = [], []
    for i, (r, j, hp) in enumerate(units):
        s_cur = s_next
        if i + 1 < len(units):
            s_next = logits(*units[i + 1])
        for o, lse in attend(r, j, hp, s_cur):
            outs.append(o)
            lses.append(lse)
        if hp == n_hp - 1:
            o_t = jnp.concatenate(outs, axis=0).T
            lse_t = jnp.concatenate(lses, axis=0).T
            outs, lses = [], []
            dst = slice(j * blk, (j + 1) * blk) if dil == 1 else pl.ds(j * blk * dil + r, blk, stride=dil)
            for c in range(width // LANE):
                o_ref[c, dst, :] = o_t[:, c * LANE:(c + 1) * LANE]
                lse_ref[c, dst, :] = lse_t[:, c * LANE:(c + 1) * LANE]


def _mixer_a_group(za, bias_a, g, dil, batch, seq):
    sub_len = seq // dil
    assert sub_len % A_BLOCK == 0
    tq = min(sub_len, A_TOKENS_PER_STEP // dil)
    assert tq % A_BLOCK == 0
    nsub = tq // A_BLOCK
    gw = 3 * A_OUT
    out_spec = pl.BlockSpec((A_OUT // LANE, tq * dil, LANE), lambda b, t: (0, b * (sub_len // tq) + t, 0))
    return pl.pallas_call(
        functools.partial(_mixer_a_kernel, dil=dil, nsub=nsub),
        grid=(batch, sub_len // tq),
        in_specs=[pl.BlockSpec((1, dil, tq, gw), lambda b, t: (b, 0, t, 0)),
                  pl.BlockSpec((1, dil, A_BLOCK, gw), lambda b, t: (b, 0, jnp.maximum(t * nsub - 1, 0), 0)),
                  pl.BlockSpec((A_HPG, 1, 2 * A_BLOCK, A_BLOCK), lambda b, t: (g, 0, 0, 0))],
        out_specs=[out_spec, out_spec],
        out_shape=[jax.ShapeDtypeStruct((A_OUT // LANE, batch * seq, LANE), F32)] * 2,
        compiler_params=_cparams(("arbitrary", "arbitrary")),
        name=f"mixer_a_d{dil}",
    )(za, za, bias_a)


SUBLANES = 8


def _mixer_b_kernel(z_ref, cw_ref, cb_ref, wax_ref, bax_ref, lam_ref, y_ref,
                    xs_scr, a_scr, b_scr, h_scr, *, ts):
    w = LRU_WIDTH

    @pl.when(pl.program_id(1) == 0)
    def _():
        xs_scr[0:SUBLANES, :] = jnp.zeros((SUBLANES, w), F32)
        h_scr[...] = jnp.zeros((1, w), F32)

    x = z_ref[0, :, 0:w].astype(F32)
    xs_scr[SUBLANES:SUBLANES + ts, :] = x
    xc = cb_ref[...]
    for i in range(CONV_WIDTH):
        off = SUBLANES - (CONV_WIDTH - 1) + i
        xc = xc + xs_scr[off:off + ts, :] * cw_ref[i:i + 1, :]
    xs_scr[0:SUBLANES, :] = xs_scr[ts:ts + SUBLANES, :]

    ra = jnp.dot(xc.astype(BF16), wax_ref[...], preferred_element_type=F32) + bax_ref[...]
    r = jax.nn.sigmoid(ra[:, :w])
    ig = jax.nn.sigmoid(ra[:, w:])
    nl = -lam_ref[...]
    softplus = jnp.maximum(nl, 0.0) + jnp.log1p(jnp.exp(-jnp.abs(nl)))
    log_a = -LRU_C * r * softplus
    a = jnp.exp(log_a)
    a_scr[...] = a
    b_scr[...] = jnp.sqrt(-jnp.tanh(log_a) * (a * a + 1.0)) * (ig * xc)

    row = lax.broadcasted_iota(jnp.int32, (SUBLANES, w), 0)

    def body(c, h):
        i = pl.multiple_of(c * SUBLANES, SUBLANES)
        a = a_scr[pl.ds(i, SUBLANES), :]
        b = b_scr[pl.ds(i, SUBLANES), :]
        for s in (1, 2, 4):
            keep = row >= s
            a_sh = jnp.where(keep, pltpu.roll(a, s, 0), 1.0)
            b_sh = jnp.where(keep, pltpu.roll(b, s, 0), 0.0)
            b = a * b_sh + b
            a = a * a_sh
        hs = a * h + b
        b_scr[pl.ds(i, SUBLANES), :] = hs
        return hs[SUBLANES - 1:SUBLANES, :]

    h_scr[...] = lax.fori_loop(0, ts // SUBLANES, body, h_scr[...])

    gb = z_ref[0, :, w:2 * w].astype(F32)
    cdf = 0.5 * (1.0 + jnp.tanh(math.sqrt(2.0 / math.pi) * (gb + 0.044715 * (gb * gb * gb))))
    y_ref[0] = (b_scr[...] * (gb * cdf)).astype(y_ref.dtype)


def _block_diag(wblocks):
    nb, di, do = wblocks.shape
    eye = jnp.eye(nb, dtype=wblocks.dtype)
    return (eye[:, None, :, None] * wblocks[:, :, None, :]).reshape(nb * di, nb * do)


def _mixer_b(zb, conv_w, conv_b, lru_wa, lru_ba, lru_wx, lru_bx, lru_lam, batch, seq, ts=512):
    w = LRU_WIDTH
    wax = jnp.concatenate([_block_diag(lru_wa), _block_diag(lru_wx)], axis=1).astype(BF16)
    bax = jnp.concatenate([lru_ba, lru_bx]).reshape(1, 2 * w)
    y = pl.pallas_call(
        functools.partial(_mixer_b_kernel, ts=ts),
        grid=(batch, seq // ts),
        in_specs=[pl.BlockSpec((1, ts, 2 * w), lambda b, t: (b, t, 0)),
                  _const_spec((CONV_WIDTH, w)), _const_spec((1, w)),
                  _const_spec((w, 2 * w)), _const_spec((1, 2 * w)), _const_spec((1, w))],
        out_specs=pl.BlockSpec((1, ts, w), lambda b, t: (b, t, 0)),
        out_shape=jax.ShapeDtypeStruct((batch, seq, w), BF16),
        scratch_shapes=[pltpu.VMEM((ts + SUBLANES, w), F32), pltpu.VMEM((ts, w), F32),
                        pltpu.VMEM((ts, w), F32), pltpu.VMEM((1, w), F32)],
        compiler_params=_cparams(("arbitrary", "arbitrary")),
        name="mixer_b",
    )(zb.reshape(batch, seq, 2 * w), conv_w, conv_b.reshape(1, w), wax, bax, lru_lam.reshape(1, w))
    return y.reshape(batch * seq, w)


HEAD_PAIR = LANE // HEAD_DIM


def _mixer_c_prologue(q_ref, k_ref, qm_scr, sel_scr, *, nblk):
    blk = MOBA_BLOCK
    seq = nblk * blk
    kmean = jnp.concatenate(
        [jnp.sum(k_ref[0, n * blk:(n + 1) * blk, :].astype(F32), axis=0, keepdims=True) for n in range(nblk)],
        axis=0) * (1.0 / blk)
    km_hi = kmean.astype(BF16)
    km_lo = (kmean - km_hi.astype(F32)).astype(BF16)
    q = q_ref[0]
    lane = lax.broadcasted_iota(jnp.int32, q.shape, 1)
    blk_id = lax.broadcasted_iota(jnp.int32, (nblk, seq), 0)
    q_blk = lax.broadcasted_iota(jnp.int32, (nblk, seq), 1) // blk
    past = blk_id < q_blk
    for hh in range(HEAD_PAIR):
        qm = jnp.where((lane >= hh * HEAD_DIM) & (lane < (hh + 1) * HEAD_DIM), q, jnp.zeros_like(q))
        qm_scr[hh] = qm
        gate = (lax.dot_general(km_hi, qm, NT_DIMS, preferred_element_type=F32)
                + lax.dot_general(km_lo, qm, NT_DIMS, preferred_element_type=F32))
        gate = jnp.where(past, gate, NEG_INF)
        sel = jnp.zeros((nblk, seq), F32)
        for _ in range(min(MOBA_TOPK, nblk - 1)):
            top = jnp.max(gate, axis=0, keepdims=True)
            first = jnp.min(jnp.where(gate == top, blk_id, nblk), axis=0, keepdims=True)
            pick = blk_id == first
            sel = jnp.where(pick, 1.0, sel)
            gate = jnp.where(pick, -jnp.inf, gate)
        sel_scr[hh] = jnp.where(past, sel, 0.0)


def _mixer_c_block(c, k_ref, vt_ref, bias_ref, o_ref, qm_scr, sel_scr, s_scr, p_scr, *, ndb):
    blk = MOBA_BLOCK
    nk = (c + 1) * blk
    qcols = slice(c * blk, (c + 1) * blk)
    fold = lambda t: t.reshape(blk // SUBLANES, SUBLANES, blk)
    heads = range(HEAD_PAIR)
    picked = [[sel_scr[hh, kb:kb + 1, qcols] > 0.0 for kb in range(c)] for hh in heads]

    def logits(hh):
        qm = qm_scr[hh, qcols, :]
        m8 = jnp.full((SUBLANES, blk), NEG_INF, F32)
        for k0 in range(0, c + 1, 2):
            nb = min(2, c + 1 - k0)
            s = lax.dot_general(k_ref[0, k0 * blk:(k0 + nb) * blk, :], qm, NT_DIMS, preferred_element_type=F32)
            for kb in range(k0, k0 + nb):
                su = s[(kb - k0) * blk:(kb - k0 + 1) * blk] + bias_ref[hh, min(c - kb, ndb)]
                s_scr[hh, kb * blk:(kb + 1) * blk, :] = su
                t8 = jnp.max(fold(su), axis=0)
                m8 = jnp.maximum(m8, t8 if kb == c else jnp.where(picked[hh][kb], t8, NEG_INF))
        return jnp.max(m8, axis=0, keepdims=True)

    def weights(hh, m):
        for kb in range(c + 1):
            mk = m if kb == c else jnp.where(picked[hh][kb], m, -NEG_INF)
            x = s_scr[hh, kb * blk:(kb + 1) * blk, :] - mk
            p_scr[hh, kb * blk:(kb + 1) * blk, :] = jnp.exp(x.astype(BF16))

    def weighted_v(hh):
        pv = jnp.dot(vt_ref[0, hh * V_ROWS:(hh + 1) * V_ROWS, :nk], p_scr[hh, :nk, :],
                     preferred_element_type=F32)
        return pv[:HEAD_DIM] / pv[HEAD_DIM:HEAD_DIM + 1]

    ms = [logits(hh) for hh in heads]
    outs = []
    for hh in heads:
        weights(hh, ms[hh])
        outs.append(weighted_v(hh))
    o_ref[0] = jnp.concatenate(outs, axis=0).T.astype(o_ref.dtype)


def _mixer_c_kernel(q_ref, k_ref, vt_ref, bias_ref, o_ref, qm_scr, sel_scr, s_scr, p_scr, *, nblk, ndb):
    qb = pl.program_id(2)

    @pl.when(qb == 0)
    def _():
        _mixer_c_prologue(q_ref, k_ref, qm_scr, sel_scr, nblk=nblk)

    for c in range(nblk):
        pl.when(qb == c)(functools.partial(_mixer_c_block, c, k_ref, vt_ref, bias_ref, o_ref,
                                           qm_scr, sel_scr, s_scr, p_scr, ndb=ndb))


def _mixer_c(zc, vt, bias_c, batch, seq, ndb):
    blk = MOBA_BLOCK
    assert seq % blk == 0
    nblk = seq // blk
    npair = C_HEADS // HEAD_PAIR
    z = zc.reshape(batch, seq, 2 * C_WIDTH)
    o = pl.pallas_call(
        functools.partial(_mixer_c_kernel, nblk=nblk, ndb=ndb),
        grid=(batch, npair, nblk),
        in_specs=[pl.BlockSpec((1, seq, LANE), lambda b, hp, qb: (b, 0, hp)),
                  pl.BlockSpec((1, seq, LANE), lambda b, hp, qb: (b, 0, npair + hp)),
                  pl.BlockSpec((1, HEAD_PAIR * V_ROWS, seq), lambda b, hp, qb: (b, hp, 0)),
                  pl.BlockSpec((HEAD_PAIR, ndb + 1, blk, blk), lambda b, hp, qb: (hp, 0, 0, 0))],
        out_specs=pl.BlockSpec((1, blk, LANE), lambda b, hp, qb: (b, qb, hp)),
        out_shape=jax.ShapeDtypeStruct((batch, seq, C_WIDTH), BF16),
        scratch_shapes=[pltpu.VMEM((HEAD_PAIR, seq, LANE), BF16), pltpu.VMEM((HEAD_PAIR, nblk, seq), F32),
                        pltpu.VMEM((HEAD_PAIR, seq, blk), F32), pltpu.VMEM((HEAD_PAIR, seq, blk), BF16)],
        compiler_params=_cparams(("arbitrary", "arbitrary", "arbitrary")),
        name="mixer_c",
    )(z, z, vt, bias_c)
    return o.reshape(batch * seq, C_WIDTH)


def _merge_kernel(x_ref, o1_ref, o2_ref, o3_ref, l1_ref, l2_ref, l3_ref, ob_ref, oc_ref, zg_ref,
                  pa_ref, pb_ref, pc_ref, wo_ref, out_ref):
    d = D_MODEL
    halves = lambda ref: jnp.concatenate([ref[c] for c in range(ref.shape[0])], axis=-1)
    l1, l2, l3 = halves(l1_ref), halves(l2_ref), halves(l3_ref)
    m = jnp.maximum(jnp.maximum(l1, l2), l3)
    e1, e2, e3 = jnp.exp(l1 - m), jnp.exp(l2 - m), jnp.exp(l3 - m)
    o_a = (e1 * halves(o1_ref) + e2 * halves(o2_ref) + e3 * halves(o3_ref)) / (e1 + e2 + e3)

    def branch(o, p_ref, k):
        gate = jax.nn.sigmoid(zg_ref[:, k * d:(k + 1) * d].astype(F32))
        return gate * jnp.dot(o, p_ref[...], preferred_element_type=F32)

    merged = branch(o_a.astype(BF16), pa_ref, 0) + branch(ob_ref[...], pb_ref, 1) + branch(oc_ref[...], pc_ref, 2)
    out_ref[...] = x_ref[...] + jnp.dot(merged.astype(BF16), wo_ref[...], preferred_element_type=F32)


def _merge(x2d, oa, lse, ob, oc, zg, p_a, p_b, p_c, w_out, tm=512):
    n = x2d.shape[0]
    row = lambda width: pl.BlockSpec((tm, width), lambda i: (i, 0))
    return pl.pallas_call(
        _merge_kernel,
        grid=(n // tm,),
        in_specs=[row(D_MODEL)] + [pl.BlockSpec((A_OUT // LANE, tm, LANE), lambda i: (0, i, 0))] * 6
                 + [row(LRU_WIDTH), row(C_WIDTH), row(3 * D_MODEL),
                  _const_spec((A_OUT, D_MODEL)), _const_spec((LRU_WIDTH, D_MODEL)),
                  _const_spec((C_WIDTH, D_MODEL)), _const_spec((D_MODEL, D_MODEL))],
        out_specs=row(D_MODEL),
        out_shape=jax.ShapeDtypeStruct((n, D_MODEL), F32),
        compiler_params=_cparams(("arbitrary",)),
        name="merge",
    )(x2d, *oa, *lse, ob, oc, zg, p_a.astype(BF16), p_b.astype(BF16), p_c.astype(BF16), w_out.astype(BF16))


def _rms(x, g):
    return x * lax.rsqrt(jnp.mean(x * x, axis=-1, keepdims=True) + EPS) * g


def _ffn_kernel(x_ref, g_ref, wgu_ref, wd_ref, gf_ref, out_ref, h_scr, act_scr, *, final_norm):
    x = x_ref[...]
    h_scr[...] = _rms(x, g_ref[...]).astype(BF16)
    for off, w in _col_chunks(0, FFN_HIDDEN, 256):
        gate = jnp.dot(h_scr[...], wgu_ref[:, off:off + w], preferred_element_type=F32)
        up = jnp.dot(h_scr[...], wgu_ref[:, FFN_HIDDEN + off:FFN_HIDDEN + off + w], preferred_element_type=F32)
        act_scr[:, off:off + w] = (gate * jax.nn.sigmoid(gate) * up).astype(BF16)
    y = x + jnp.dot(act_scr[...], wd_ref[...], preferred_element_type=F32)
    if final_norm:
        y = _rms(y, gf_ref[...])
    out_ref[...] = y


def _ffn(x2d, g_ffn, w_gu, w_down, g_final, final_norm, tm=512):
    n = x2d.shape[0]
    row = pl.BlockSpec((tm, D_MODEL), lambda i: (i, 0))
    return pl.pallas_call(
        functools.partial(_ffn_kernel, final_norm=final_norm),
        grid=(n // tm,),
        in_specs=[row, _const_spec((1, D_MODEL)), _const_spec((D_MODEL, 2 * FFN_HIDDEN)),
                  _const_spec((FFN_HIDDEN, D_MODEL)), _const_spec((1, D_MODEL))],
        out_specs=row,
        out_shape=jax.ShapeDtypeStruct((n, D_MODEL), F32),
        scratch_shapes=[pltpu.VMEM((tm, D_MODEL), BF16), pltpu.VMEM((tm, FFN_HIDDEN), BF16)],
        compiler_params=_cparams(("arbitrary",)),
        name="ffn",
    )(x2d, g_ffn.reshape(1, D_MODEL), w_gu.astype(BF16), w_down.astype(BF16), g_final.reshape(1, D_MODEL))


def kernel(x, rel_bias, g_mix, w_in, conv_w, conv_b, lru_wa, lru_ba, lru_wx, lru_bx, lru_lam,
           p_a, p_b, p_c, w_out, g_ffn, w_gu, w_down, g_final):
    batch, seq, d = x.shape
    assert d == D_MODEL and w_in.shape[-1] == IN_COLS
    depth = w_in.shape[0]
    nblk = seq // MOBA_BLOCK
    ndb = _moba_far_blocks(nblk)

    bias_a = _build_bias(rel_bias, _bucket_index_a(), A_HEADS, 0, A_HPG)
    bias_c = _build_bias(rel_bias, _bucket_index_c(ndb), C_HEADS, A_HEADS, None)

    x2d = x.reshape(batch * seq, d)
    for l in range(depth):
        *za, zb, zc, vt, zg = _in_proj(x2d, g_mix[l], w_in[l], batch, seq)
        oa, lse = zip(*[_mixer_a_group(za[g], bias_a, g, dil, batch, seq)
                        for g, (_, dil) in enumerate(A_GROUPS)])
        ob = _mixer_b(zb, conv_w[l], conv_b[l], lru_wa[l], lru_ba[l], lru_wx[l], lru_bx[l], lru_lam[l],
                      batch, seq)
        oc = _mixer_c(zc, vt, bias_c, batch, seq, ndb)
        x2d = _merge(x2d, oa, lse, ob, oc, zg, p_a[l], p_b[l], p_c[l], w_out[l])
        x2d = _ffn(x2d, g_ffn[l], w_gu[l], w_down[l], g_final, final_norm=(l == depth - 1))
    return x2d.reshape(batch, seq, d)
```

```python
import functools
import math

import numpy as np
import jax
import jax.numpy as jnp
from jax import lax
from jax.experimental import pallas as pl
from jax.experimental.pallas import tpu as pltpu

F32 = jnp.float32
BF16 = jnp.bfloat16

D_MODEL = 1024
HEAD_DIM = 64
SCALE = HEAD_DIM ** -0.5
LOG2E = math.log2(math.e)
NEG_INF = -1e30
EPS = 1e-6
A_GROUPS = ((128, 1), (512, 4), (2048, 16))
A_HPG = 4
A_HEADS = A_HPG * len(A_GROUPS)
A_WIDTH = A_HEADS * HEAD_DIM
A_OUT = A_HPG * HEAD_DIM
A_BLOCK = 128
LRU_WIDTH = D_MODEL // 2
LRU_BLOCKS = 8
CONV_WIDTH = 4
LRU_C = 8.0
C_HEADS = 8
C_WIDTH = C_HEADS * HEAD_DIM
MOBA_BLOCK = 256
MOBA_TOPK = 3
REL_BUCKETS = 32
REL_MAX_DIST = 2048
FFN_HIDDEN = 2816
IN_COLS = 3 * A_WIDTH + 2 * LRU_WIDTH + 3 * C_WIDTH + 3 * D_MODEL
V_C_OFF = 3 * A_WIDTH + 2 * LRU_WIDTH + 2 * C_WIDTH

LANE = 128
V_ROWS = HEAD_DIM + 16
VMEM_LIMIT = 56 * 1024 * 1024

NT_DIMS = (((1,), (1,)), ((), ()))


def _cparams(sem, vmem=VMEM_LIMIT):
    return pltpu.CompilerParams(dimension_semantics=sem, vmem_limit_bytes=vmem)


def _const_spec(shape):
    nd = len(shape)
    return pl.BlockSpec(shape, lambda *_: (0,) * nd, pipeline_mode=pl.Buffered(1))


def _rel_bucket_np(dist):
    max_exact = REL_BUCKETS // 2
    d = np.maximum(dist, 0)
    df = np.maximum(d, 1).astype(np.float32)
    large = max_exact + (np.log(df / np.float32(max_exact)) / np.float32(math.log(REL_MAX_DIST / max_exact))
                         * np.float32(REL_BUCKETS - max_exact)).astype(np.int32)
    large = np.minimum(large, REL_BUCKETS - 1)
    return np.where(d < max_exact, d, large).astype(np.int32)


def _bucket_index_a():
    qi = np.arange(A_BLOCK)[None, :] + A_BLOCK
    kj = np.arange(2 * A_BLOCK)[:, None]
    delta = qi - kj
    mats = []
    for window, dil in A_GROUPS:
        band = (delta >= 0) & (delta <= window // dil)
        mats.append(np.where(band, _rel_bucket_np(delta * dil), -1))
    return np.stack(mats).astype(np.int32)


def _moba_far_blocks(nblk):
    for db in range(1, nblk + 1):
        lo = db * MOBA_BLOCK - (MOBA_BLOCK - 1)
        if np.all(_rel_bucket_np(np.arange(lo, nblk * MOBA_BLOCK)) == REL_BUCKETS - 1):
            return db
    return nblk


def _bucket_index_c(ndb):
    k = np.arange(MOBA_BLOCK)[:, None]
    q = np.arange(MOBA_BLOCK)[None, :]
    mats = []
    for db in range(ndb + 1):
        dist = db * MOBA_BLOCK + q - k
        b = _rel_bucket_np(dist)
        if db == 0:
            b = np.where(dist >= 0, b, -1)
        mats.append(b)
    return np.stack(mats).astype(np.int32)


def _bias_kernel(tab_ref, idx_ref, out_ref, *, head_off, unit):
    h = pl.program_id(0) + head_off
    idx = idx_ref[0]
    out = jnp.full(idx.shape, NEG_INF, F32)
    for b in range(REL_BUCKETS):
        out = jnp.where(idx == b, tab_ref[b, h] * unit, out)
    out_ref[0, 0] = out


def _build_bias(rel_bias, idx, n_heads, head_off, heads_per_idx_group, unit=1.0):
    n_mats = idx.shape[0] if heads_per_idx_group is None else 1
    r, c = idx.shape[-2:]
    if heads_per_idx_group is None:
        idx_map = lambda h, m: (m, 0, 0)
    else:
        idx_map = lambda h, m: (h // heads_per_idx_group, 0, 0)
    return pl.pallas_call(
        functools.partial(_bias_kernel, head_off=head_off, unit=unit),
        grid=(n_heads, n_mats),
        in_specs=[pl.BlockSpec(memory_space=pltpu.SMEM),
                  pl.BlockSpec((1, r, c), idx_map)],
        out_specs=pl.BlockSpec((1, 1, r, c), lambda h, m: (h, m, 0, 0)),
        out_shape=jax.ShapeDtypeStruct((n_heads, n_mats, r, c), F32),
        compiler_params=_cparams(("arbitrary", "arbitrary")),
        name="rel_bias_build",
    )(rel_bias, jnp.asarray(idx))


def _col_chunks(start, stop, width=512):
    out = []
    while start < stop:
        w = min(width, stop - start)
        out.append((start, w))
        start += w
    return out


def _in_proj_kernel(x_ref, g_ref, w_ref, wvt_ref, za0_ref, za1_ref, za2_ref, zb_ref, zc_ref, vt_ref, zg_ref,
                    h_scr, dil_scr, *, tm):
    x = x_ref[...]
    ms = jnp.mean(x * x, axis=-1, keepdims=True)
    h_scr[...] = (x * lax.rsqrt(ms + EPS) * g_ref[...]).astype(BF16)

    def proj(base, width):
        return jnp.dot(h_scr[...], w_ref[:, base:base + width], preferred_element_type=F32)

    def seg(out_ref, base, width, scaled_cols=0, scale=1.0, chunk=512):
        assert scaled_cols % chunk == 0
        for off, w in _col_chunks(0, width, chunk):
            r = proj(base + off, w)
            if off < scaled_cols:
                r = r * scale
            out_ref[:, off:off + w] = r.astype(out_ref.dtype)

    gw = 3 * A_OUT
    for g, (za_ref, (_, dil)) in enumerate(zip((za0_ref, za1_ref, za2_ref), A_GROUPS)):
        q = proj(g * gw, A_OUT) * SCALE
        kv = proj(g * gw + A_OUT, 2 * A_OUT)
        if dil == 1:
            za_ref[0, 0, :, :A_OUT] = q.astype(BF16)
            za_ref[0, 0, :, A_OUT:] = kv.astype(BF16)
        else:
            qkv = (q[:, :LANE], q[:, LANE:]) + tuple(kv[:, c * LANE:(c + 1) * LANE] for c in range(4))
            for c, part in enumerate(qkv):
                dil_scr[c] = part
            for r in range(dil):
                for c in range(len(qkv)):
                    za_ref[0, r, :, c * LANE:(c + 1) * LANE] = (
                        dil_scr[c, pl.ds(r, tm // dil, stride=dil), :].astype(BF16))

    wa, wb, wc = 3 * A_WIDTH, 2 * LRU_WIDTH, 2 * C_WIDTH
    seg(zb_ref, wa, wb)
    seg(zc_ref, wa + wb, wc, C_WIDTH, SCALE * LOG2E)
    seg(zg_ref, wa + wb + wc, 3 * D_MODEL)
    vt = lax.dot_general(wvt_ref[...], h_scr[...], NT_DIMS, preferred_element_type=F32).astype(BF16)
    for h in range(C_HEADS):
        vt_ref[0, h * V_ROWS:h * V_ROWS + HEAD_DIM, :] = vt[h * HEAD_DIM:(h + 1) * HEAD_DIM]
        vt_ref[0, h * V_ROWS + HEAD_DIM:(h + 1) * V_ROWS, :] = jnp.ones((V_ROWS - HEAD_DIM, tm), BF16)


def _in_proj(x2d, g, w_in, batch, seq, tm=512):
    n = x2d.shape[0]
    grp = lambda k: jnp.concatenate([w_in[:, s * A_WIDTH + k * A_OUT:s * A_WIDTH + (k + 1) * A_OUT]
                                     for s in range(3)], axis=1)
    w_main = jnp.concatenate([grp(k) for k in range(len(A_GROUPS))]
                             + [w_in[:, 3 * A_WIDTH:V_C_OFF], w_in[:, V_C_OFF + C_WIDTH:]], axis=1).astype(BF16)
    wvt = w_in[:, V_C_OFF:V_C_OFF + C_WIDTH].T.astype(BF16)
    wb, wc, wg = 2 * LRU_WIDTH, 2 * C_WIDTH, 3 * D_MODEL
    gw = 3 * A_OUT
    tiles_per_seq = seq // tm
    row = lambda i: (i, 0)
    seq_tile = lambda i: (i // tiles_per_seq, 0, i % tiles_per_seq, 0)
    za_specs = [pl.BlockSpec((1, dil, tm // dil, gw), seq_tile) for _, dil in A_GROUPS]
    za_shapes = [jax.ShapeDtypeStruct((batch, dil, seq // dil, gw), BF16) for _, dil in A_GROUPS]
    return pl.pallas_call(
        functools.partial(_in_proj_kernel, tm=tm),
        grid=(n // tm,),
        in_specs=[pl.BlockSpec((tm, D_MODEL), row),
                  _const_spec((1, D_MODEL)),
                  _const_spec(w_main.shape),
                  _const_spec(wvt.shape)],
        out_specs=za_specs + [pl.BlockSpec((tm, wb), row),
                              pl.BlockSpec((tm, wc), row),
                              pl.BlockSpec((1, C_HEADS * V_ROWS, tm),
                                           lambda i: (i // tiles_per_seq, 0, i % tiles_per_seq)),
                              pl.BlockSpec((tm, wg), row)],
        out_shape=za_shapes + [jax.ShapeDtypeStruct((n, wb), BF16),
                               jax.ShapeDtypeStruct((n, wc), BF16),
                               jax.ShapeDtypeStruct((batch, C_HEADS * V_ROWS, seq), BF16),
                               jax.ShapeDtypeStruct((n, wg), BF16)],
        scratch_shapes=[pltpu.VMEM((tm, D_MODEL), BF16), pltpu.VMEM((gw // LANE, tm, LANE), F32)],
        compiler_params=_cparams(("arbitrary",)),
        name="in_proj",
    )(x2d, g.reshape(1, D_MODEL), w_main, wvt)


A_TOKENS_PER_STEP = 2048


def _mixer_a_kernel(z_ref, zp_ref, bias_ref, o_ref, lse_ref, *, dil, nsub):
    t = pl.program_id(1)
    blk, width = A_BLOCK, A_OUT
    kcols, vcols = slice(width, 2 * width), slice(2 * width, 3 * width)
    lane_head = lax.broadcasted_iota(jnp.int32, (blk, width), 1) // HEAD_DIM
    in_prev = (lax.broadcasted_iota(jnp.int32, (2 * blk, 2 * blk), 0) < blk).astype(F32)
    no_prev = in_prev * jnp.where(t == 0, NEG_INF, 0.0)
    transpose_v = lambda v: v.astype(F32).T.astype(BF16)

    ones_rows = jnp.ones((V_ROWS - HEAD_DIM, 2 * blk), BF16)
    windows = {}

    def window(r, j):
        if (r, j) not in windows:
            rows = slice(j * blk, (j + 1) * blk)
            if j == 0:
                k_prev, vt_prev = zp_ref[0, r, :, kcols], transpose_v(zp_ref[0, r, :, vcols])
            else:
                _, k_before, vt_before = window(r, j - 1)
                k_prev, vt_prev = k_before[blk:], vt_before[:, blk:]
            k_win = jnp.concatenate([k_prev, z_ref[0, r, rows, kcols]], axis=0)
            vt_win = jnp.concatenate([vt_prev, transpose_v(z_ref[0, r, rows, vcols])], axis=1)
            windows[(r, j)] = (z_ref[0, r, rows, 0:width], k_win, vt_win)
        return windows[(r, j)]

    def logits(r, j, hp):
        q, k_win, _ = window(r, j)
        heads = (2 * hp, 2 * hp + 1)
        q2 = jnp.concatenate([jnp.where(lane_head == h, q, jnp.zeros_like(q)) for h in heads], axis=0)
        bias2 = jnp.concatenate([bias_ref[h, 0] for h in heads], axis=1)
        s = lax.dot_general(k_win, q2, NT_DIMS, preferred_element_type=F32) + bias2
        return s + no_prev if j == 0 else s

    def attend(r, j, hp, s):
        vt_win = window(r, j)[2]
        m = jnp.max(s, axis=0, keepdims=True)
        p = jnp.exp((s - m).astype(BF16))
        vt_ext = jnp.concatenate([part for h in (2 * hp, 2 * hp + 1)
                                  for part in (vt_win[h * HEAD_DIM:(h + 1) * HEAD_DIM, :], ones_rows)], axis=0)
        pv = jnp.dot(vt_ext, p, preferred_element_type=F32)
        res = []
        for u in range(2):
            cols = slice(u * blk, (u + 1) * blk)
            den = pv[u * V_ROWS + HEAD_DIM:u * V_ROWS + HEAD_DIM + 1, cols]
            res.append((pv[u * V_ROWS:u * V_ROWS + HEAD_DIM, cols] / den,
                        jnp.broadcast_to(m[:, cols] + jnp.log(den), (HEAD_DIM, blk))))
        return res

    n_hp = A_HPG // 2
    units = [(r, j, hp) for r in range(dil) for j in range(nsub) for hp in range(n_hp)]
    s_next = logits(*units[0])
    outs, lses = [], []
    for i, (r, j, hp) in enumerate(units):
        s_cur = s_next
        if i + 1 < len(units):
            s_next = logits(*units[i + 1])
        for o, lse in attend(r, j, hp, s_cur):
            outs.append(o)
            lses.append(lse)
        if hp == n_hp - 1:
            o_t = jnp.concatenate(outs, axis=0).T
            lse_t = jnp.concatenate(lses, axis=0).T
            outs, lses = [], []
            dst = slice(j * blk, (j + 1) * blk) if dil == 1 else pl.ds(j * blk * dil + r, blk, stride=dil)
            for c in range(width // LANE):
                o_ref[c, dst, :] = o_t[:, c * LANE:(c + 1) * LANE]
                lse_ref[c, dst, :] = lse_t[:, c * LANE:(c + 1) * LANE]


def _mixer_a_group(za, bias_a, g, dil, batch, seq):
    sub_len = seq // dil
    assert sub_len % A_BLOCK == 0
    tq = min(sub_len, A_TOKENS_PER_STEP // dil)
    assert tq % A_BLOCK == 0
    nsub = tq // A_BLOCK
    gw = 3 * A_OUT
    out_spec = pl.BlockSpec((A_OUT // LANE, tq * dil, LANE), lambda b, t: (0, b * (sub_len // tq) + t, 0))
    return pl.pallas_call(
        functools.partial(_mixer_a_kernel, dil=dil, nsub=nsub),
        grid=(batch, sub_len // tq),
        in_specs=[pl.BlockSpec((1, dil, tq, gw), lambda b, t: (b, 0, t, 0)),
                  pl.BlockSpec((1, dil, A_BLOCK, gw), lambda b, t: (b, 0, jnp.maximum(t * nsub - 1, 0), 0)),
                  pl.BlockSpec((A_HPG, 1, 2 * A_BLOCK, A_BLOCK), lambda b, t: (g, 0, 0, 0))],
        out_specs=[out_spec, out_spec],
        out_shape=[jax.ShapeDtypeStruct((A_OUT // LANE, batch * seq, LANE), F32)] * 2,
        compiler_params=_cparams(("arbitrary", "arbitrary")),
        name=f"mixer_a_d{dil}",
    )(za, za, bias_a)


SUBLANES = 8


def _mixer_b_kernel(z_ref, cw_ref, cb_ref, wax_ref, bax_ref, lam_ref, y_ref,
                    xs_scr, a_scr, b_scr, h_scr, *, ts):
    w = LRU_WIDTH

    @pl.when(pl.program_id(1) == 0)
    def _():
        xs_scr[0:SUBLANES, :] = jnp.zeros((SUBLANES, w), F32)
        h_scr[...] = jnp.zeros((1, w), F32)

    x = z_ref[0, :, 0:w].astype(F32)
    xs_scr[SUBLANES:SUBLANES + ts, :] = x
    xc = cb_ref[...]
    for i in range(CONV_WIDTH):
        off = SUBLANES - (CONV_WIDTH - 1) + i
        xc = xc + xs_scr[off:off + ts, :] * cw_ref[i:i + 1, :]
    xs_scr[0:SUBLANES, :] = xs_scr[ts:ts + SUBLANES, :]

    ra = jnp.dot(xc.astype(BF16), wax_ref[...], preferred_element_type=F32) + bax_ref[...]
    r = jax.nn.sigmoid(ra[:, :w])
    ig = jax.nn.sigmoid(ra[:, w:])
    nl = -lam_ref[...]
    softplus = jnp.maximum(nl, 0.0) + jnp.log1p(jnp.exp(-jnp.abs(nl)))
    log_a = -LRU_C * r * softplus
    a = jnp.exp(log_a)
    a_scr[...] = a
    b_scr[...] = jnp.sqrt(-jnp.tanh(log_a) * (a * a + 1.0)) * (ig * xc)

    row = lax.broadcasted_iota(jnp.int32, (SUBLANES, w), 0)

    def body(c, h):
        i = pl.multiple_of(c * SUBLANES, SUBLANES)
        a = a_scr[pl.ds(i, SUBLANES), :]
        b = b_scr[pl.ds(i, SUBLANES), :]
        for s in (1, 2, 4):
            keep = row >= s
            a_sh = jnp.where(keep, pltpu.roll(a, s, 0), 1.0)
            b_sh = jnp.where(keep, pltpu.roll(b, s, 0), 0.0)
            b = a * b_sh + b
            a = a * a_sh
        hs = a * h + b
        b_scr[pl.ds(i, SUBLANES), :] = hs
        return hs[SUBLANES - 1:SUBLANES, :]

    h_scr[...] = lax.fori_loop(0, ts // SUBLANES, body, h_scr[...])

    gb = z_ref[0, :, w:2 * w].astype(F32)
    cdf = 0.5 * (1.0 + jnp.tanh(math.sqrt(2.0 / math.pi) * (gb + 0.044715 * (gb * gb * gb))))
    y_ref[0] = (b_scr[...] * (gb * cdf)).astype(y_ref.dtype)


def _block_diag(wblocks):
    nb, di, do = wblocks.shape
    eye = jnp.eye(nb, dtype=wblocks.dtype)
    return (eye[:, None, :, None] * wblocks[:, :, None, :]).reshape(nb * di, nb * do)


def _mixer_b(zb, conv_w, conv_b, lru_wa, lru_ba, lru_wx, lru_bx, lru_lam, batch, seq, ts=512):
    w = LRU_WIDTH
    wax = jnp.concatenate([_block_diag(lru_wa), _block_diag(lru_wx)], axis=1).astype(BF16)
    bax = jnp.concatenate([lru_ba, lru_bx]).reshape(1, 2 * w)
    y = pl.pallas_call(
        functools.partial(_mixer_b_kernel, ts=ts),
        grid=(batch, seq // ts),
        in_specs=[pl.BlockSpec((1, ts, 2 * w), lambda b, t: (b, t, 0)),
                  _const_spec((CONV_WIDTH, w)), _const_spec((1, w)),
                  _const_spec((w, 2 * w)), _const_spec((1, 2 * w)), _const_spec((1, w))],
        out_specs=pl.BlockSpec((1, ts, w), lambda b, t: (b, t, 0)),
        out_shape=jax.ShapeDtypeStruct((batch, seq, w), BF16),
        scratch_shapes=[pltpu.VMEM((ts + SUBLANES, w), F32), pltpu.VMEM((ts, w), F32),
                        pltpu.VMEM((ts, w), F32), pltpu.VMEM((1, w), F32)],
        compiler_params=_cparams(("arbitrary", "arbitrary")),
        name="mixer_b",
    )(zb.reshape(batch, seq, 2 * w), conv_w, conv_b.reshape(1, w), wax, bax, lru_lam.reshape(1, w))
    return y.reshape(batch * seq, w)


HEAD_PAIR = LANE // HEAD_DIM


def _mixer_c_prologue(q_ref, k_ref, qm_scr, sel_scr, *, nblk):
    blk = MOBA_BLOCK
    seq = nblk * blk
    kmean = jnp.concatenate(
        [jnp.sum(k_ref[0, n * blk:(n + 1) * blk, :].astype(F32), axis=0, keepdims=True) for n in range(nblk)],
        axis=0) * (1.0 / blk)
    km_hi = kmean.astype(BF16)
    km_lo = (kmean - km_hi.astype(F32)).astype(BF16)
    q = q_ref[0]
    lane = lax.broadcasted_iota(jnp.int32, q.shape, 1)
    blk_id = lax.broadcasted_iota(jnp.int32, (nblk, seq), 0)
    q_blk = lax.broadcasted_iota(jnp.int32, (nblk, seq), 1) // blk
    past = blk_id < q_blk
    for hh in range(HEAD_PAIR):
        qm = jnp.where((lane >= hh * HEAD_DIM) & (lane < (hh + 1) * HEAD_DIM), q, jnp.zeros_like(q))
        qm_scr[hh] = qm
        gate = (lax.dot_general(km_hi, qm, NT_DIMS, preferred_element_type=F32)
                + lax.dot_general(km_lo, qm, NT_DIMS, preferred_element_type=F32))
        gate = jnp.where(past, gate, NEG_INF)
        sel = jnp.zeros((nblk, seq), F32)
        for _ in range(min(MOBA_TOPK, nblk - 1)):
            top = jnp.max(gate, axis=0, keepdims=True)
            first = jnp.min(jnp.where(gate == top, blk_id, nblk), axis=0, keepdims=True)
            pick = blk_id == first
            sel = jnp.where(pick, 1.0, sel)
            gate = jnp.where(pick, -jnp.inf, gate)
        sel_scr[hh] = jnp.where(past, sel, 0.0)


def _mixer_c_block(c, k_ref, vt_ref, bias_ref, o_ref, qm_scr, sel_scr, s_scr, p_scr, *, ndb):
    blk = MOBA_BLOCK
    nk = (c + 1) * blk
    qcols = slice(c * blk, (c + 1) * blk)
    fold = lambda t: t.reshape(blk // SUBLANES, SUBLANES, blk)
    heads = range(HEAD_PAIR)
    picked = [[sel_scr[hh, kb:kb + 1, qcols] > 0.0 for kb in range(c)] for hh in heads]
    far_bias = [bias_ref[hh, ndb, 0:1, :] for hh in heads]

    pairs = [(k0, min(k0 + 2, c + 1)) for k0 in range(0, c + 1, 2)]

    def logits(hh, res):
        qm = qm_scr[hh, qcols, :]
        m8 = jnp.full((SUBLANES, blk), NEG_INF, F32)
        for k0, k1 in pairs:
            s = lax.dot_general(k_ref[0, k0 * blk:k1 * blk, :], qm, NT_DIMS, preferred_element_type=F32)
            for kb in range(k0, k1):
                su = s[(kb - k0) * blk:(kb - k0 + 1) * blk]
                if c - kb < ndb:
                    su = su + bias_ref[hh, c - kb]
                    t8 = jnp.max(fold(su), axis=0)
                else:
                    t8 = jnp.max(fold(su), axis=0) + far_bias[hh]
                s_scr[hh, kb * blk:(kb + 1) * blk, :] = su
                m8 = jnp.maximum(m8, t8 if kb == c else jnp.where(picked[hh][kb], t8, NEG_INF))
            yield
        res[hh] = jnp.max(m8, axis=0, keepdims=True)

    def attend(hh, m, res):
        pv = jnp.zeros((V_ROWS, blk), F32)
        for k0, k1 in pairs:
            for kb in range(k0, k1):
                mk = m if kb == c else jnp.where(picked[hh][kb], m, -NEG_INF)
                if c - kb >= ndb:
                    mk = mk - far_bias[hh]
                x = s_scr[hh, kb * blk:(kb + 1) * blk, :] - mk
                p_scr[hh, kb * blk:(kb + 1) * blk, :] = jnp.exp2(x.astype(BF16))
            pv = pv + jnp.dot(vt_ref[0, hh * V_ROWS:(hh + 1) * V_ROWS, k0 * blk:k1 * blk],
                              p_scr[hh, k0 * blk:k1 * blk, :], preferred_element_type=F32)
            yield
        res[hh] = pv[:HEAD_DIM] / pv[HEAD_DIM:HEAD_DIM + 1]

    def run(*gens):
        gens = list(gens)
        while gens:
            gens = [g for g in gens if next(g, StopIteration) is not StopIteration]

    ms, outs = {}, {}
    for hh in heads:
        run(logits(hh, ms))
    for hh in heads:
        run(attend(hh, ms[hh], outs))
    o_ref[0] = jnp.concatenate([outs[hh] for hh in heads], axis=0).T.astype(o_ref.dtype)


def _mixer_c_kernel(q_ref, k_ref, vt_ref, bias_ref, o_ref, qm_scr, sel_scr, s_scr, p_scr, *, nblk, ndb):
    qb = pl.program_id(2)

    @pl.when(qb == 0)
    def _():
        _mixer_c_prologue(q_ref, k_ref, qm_scr, sel_scr, nblk=nblk)

    for c in range(nblk):
        pl.when(qb == c)(functools.partial(_mixer_c_block, c, k_ref, vt_ref, bias_ref, o_ref,
                                           qm_scr, sel_scr, s_scr, p_scr, ndb=ndb))


def _mixer_c(zc, vt, bias_c, batch, seq, ndb):
    blk = MOBA_BLOCK
    assert seq % blk == 0
    nblk = seq // blk
    npair = C_HEADS // HEAD_PAIR
    z = zc.reshape(batch, seq, 2 * C_WIDTH)
    o = pl.pallas_call(
        functools.partial(_mixer_c_kernel, nblk=nblk, ndb=ndb),
        grid=(batch, npair, nblk),
        in_specs=[pl.BlockSpec((1, seq, LANE), lambda b, hp, qb: (b, 0, hp)),
                  pl.BlockSpec((1, seq, LANE), lambda b, hp, qb: (b, 0, npair + hp)),
                  pl.BlockSpec((1, HEAD_PAIR * V_ROWS, seq), lambda b, hp, qb: (b, hp, 0)),
                  pl.BlockSpec((HEAD_PAIR, ndb + 1, blk, blk), lambda b, hp, qb: (hp, 0, 0, 0))],
        out_specs=pl.BlockSpec((1, blk, LANE), lambda b, hp, qb: (b, qb, hp)),
        out_shape=jax.ShapeDtypeStruct((batch, seq, C_WIDTH), BF16),
        scratch_shapes=[pltpu.VMEM((HEAD_PAIR, seq, LANE), BF16), pltpu.VMEM((HEAD_PAIR, nblk, seq), F32),
                        pltpu.VMEM((HEAD_PAIR, seq, blk), F32), pltpu.VMEM((HEAD_PAIR, seq, blk), BF16)],
        compiler_params=_cparams(("arbitrary", "arbitrary", "arbitrary")),
        name="mixer_c",
    )(z, z, vt, bias_c)
    return o.reshape(batch * seq, C_WIDTH)


def _merge_kernel(x_ref, o1_ref, o2_ref, o3_ref, l1_ref, l2_ref, l3_ref, ob_ref, oc_ref, zg_ref,
                  pa_ref, pb_ref, pc_ref, wo_ref, out_ref):
    d = D_MODEL
    halves = lambda ref: jnp.concatenate([ref[c] for c in range(ref.shape[0])], axis=-1)
    l1, l2, l3 = halves(l1_ref), halves(l2_ref), halves(l3_ref)
    m = jnp.maximum(jnp.maximum(l1, l2), l3)
    e1, e2, e3 = jnp.exp(l1 - m), jnp.exp(l2 - m), jnp.exp(l3 - m)
    o_a = (e1 * halves(o1_ref) + e2 * halves(o2_ref) + e3 * halves(o3_ref)) / (e1 + e2 + e3)

    def branch(o, p_ref, k):
        gate = jax.nn.sigmoid(zg_ref[:, k * d:(k + 1) * d].astype(F32))
        return gate * jnp.dot(o, p_ref[...], preferred_element_type=F32)

    merged = branch(o_a.astype(BF16), pa_ref, 0) + branch(ob_ref[...], pb_ref, 1) + branch(oc_ref[...], pc_ref, 2)
    out_ref[...] = x_ref[...] + jnp.dot(merged.astype(BF16), wo_ref[...], preferred_element_type=F32)


def _merge(x2d, oa, lse, ob, oc, zg, p_a, p_b, p_c, w_out, tm=512):
    n = x2d.shape[0]
    row = lambda width: pl.BlockSpec((tm, width), lambda i: (i, 0))
    return pl.pallas_call(
        _merge_kernel,
        grid=(n // tm,),
        in_specs=[row(D_MODEL)] + [pl.BlockSpec((A_OUT // LANE, tm, LANE), lambda i: (0, i, 0))] * 6
                 + [row(LRU_WIDTH), row(C_WIDTH), row(3 * D_MODEL),
                  _const_spec((A_OUT, D_MODEL)), _const_spec((LRU_WIDTH, D_MODEL)),
                  _const_spec((C_WIDTH, D_MODEL)), _const_spec((D_MODEL, D_MODEL))],
        out_specs=row(D_MODEL),
        out_shape=jax.ShapeDtypeStruct((n, D_MODEL), F32),
        compiler_params=_cparams(("arbitrary",)),
        name="merge",
    )(x2d, *oa, *lse, ob, oc, zg, p_a.astype(BF16), p_b.astype(BF16), p_c.astype(BF16), w_out.astype(BF16))


def _rms(x, g):
    return x * lax.rsqrt(jnp.mean(x * x, axis=-1, keepdims=True) + EPS) * g


def _ffn_kernel(x_ref, g_ref, wgu_ref, wd_ref, gf_ref, out_ref, h_scr, act_scr, *, final_norm):
    x = x_ref[...]
    h_scr[...] = _rms(x, g_ref[...]).astype(BF16)
    for off, w in _col_chunks(0, FFN_HIDDEN, 256):
        gate = jnp.dot(h_scr[...], wgu_ref[:, off:off + w], preferred_element_type=F32)
        up = jnp.dot(h_scr[...], wgu_ref[:, FFN_HIDDEN + off:FFN_HIDDEN + off + w], preferred_element_type=F32)
        act_scr[:, off:off + w] = (gate * jax.nn.sigmoid(gate) * up).astype(BF16)
    y = x + jnp.dot(act_scr[...], wd_ref[...], preferred_element_type=F32)
    if final_norm:
        y = _rms(y, gf_ref[...])
    out_ref[...] = y


def _ffn(x2d, g_ffn, w_gu, w_down, g_final, final_norm, tm=512):
    n = x2d.shape[0]
    row = pl.BlockSpec((tm, D_MODEL), lambda i: (i, 0))
    return pl.pallas_call(
        functools.partial(_ffn_kernel, final_norm=final_norm),
        grid=(n // tm,),
        in_specs=[row, _const_spec((1, D_MODEL)), _const_spec((D_MODEL, 2 * FFN_HIDDEN)),
                  _const_spec((FFN_HIDDEN, D_MODEL)), _const_spec((1, D_MODEL))],
        out_specs=row,
        out_shape=jax.ShapeDtypeStruct((n, D_MODEL), F32),
        scratch_shapes=[pltpu.VMEM((tm, D_MODEL), BF16), pltpu.VMEM((tm, FFN_HIDDEN), BF16)],
        compiler_params=_cparams(("arbitrary",)),
        name="ffn",
    )(x2d, g_ffn.reshape(1, D_MODEL), w_gu.astype(BF16), w_down.astype(BF16), g_final.reshape(1, D_MODEL))


def kernel(x, rel_bias, g_mix, w_in, conv_w, conv_b, lru_wa, lru_ba, lru_wx, lru_bx, lru_lam,
           p_a, p_b, p_c, w_out, g_ffn, w_gu, w_down, g_final):
    batch, seq, d = x.shape
    assert d == D_MODEL and w_in.shape[-1] == IN_COLS
    depth = w_in.shape[0]
    nblk = seq // MOBA_BLOCK
    ndb = _moba_far_blocks(nblk)

    bias_a = _build_bias(rel_bias, _bucket_index_a(), A_HEADS, 0, A_HPG)
    bias_c = _build_bias(rel_bias, _bucket_index_c(ndb), C_HEADS, A_HEADS, None, LOG2E)

    x2d = x.reshape(batch * seq, d)
    for l in range(depth):
        *za, zb, zc, vt, zg = _in_proj(x2d, g_mix[l], w_in[l], batch, seq)
        oa, lse = zip(*[_mixer_a_group(za[g], bias_a, g, dil, batch, seq)
                        for g, (_, dil) in enumerate(A_GROUPS)])
        ob = _mixer_b(zb, conv_w[l], conv_b[l], lru_wa[l], lru_ba[l], lru_wx[l], lru_bx[l], lru_lam[l],
                      batch, seq)
        oc = _mixer_c(zc, vt, bias_c, batch, seq, ndb)
        x2d = _merge(x2d, oa, lse, ob, oc, zg, p_a[l], p_b[l], p_c[l], w_out[l])
        x2d = _ffn(x2d, g_ffn[l], w_gu[l], w_down[l], g_final, final_norm=(l == depth - 1))
    return x2d.reshape(batch, seq, d)
```

```python
import functools
import math

import numpy as np
import jax
import jax.numpy as jnp
from jax import lax
from jax.experimental import pallas as pl
from jax.experimental.pallas import tpu as pltpu

F32 = jnp.float32
BF16 = jnp.bfloat16

D_MODEL = 1024
HEAD_DIM = 64
SCALE = HEAD_DIM ** -0.5
LOG2E = math.log2(math.e)
NEG_INF = -1e30
EPS = 1e-6
A_GROUPS = ((128, 1), (512, 4), (2048, 16))
A_HPG = 4
A_HEADS = A_HPG * len(A_GROUPS)
A_WIDTH = A_HEADS * HEAD_DIM
A_OUT = A_HPG * HEAD_DIM
A_BLOCK = 128
LRU_WIDTH = D_MODEL // 2
LRU_BLOCKS = 8
CONV_WIDTH = 4
LRU_C = 8.0
C_HEADS = 8
C_WIDTH = C_HEADS * HEAD_DIM
MOBA_BLOCK = 256
MOBA_TOPK = 3
REL_BUCKETS = 32
REL_MAX_DIST = 2048
FFN_HIDDEN = 2816
IN_COLS = 3 * A_WIDTH + 2 * LRU_WIDTH + 3 * C_WIDTH + 3 * D_MODEL
V_C_OFF = 3 * A_WIDTH + 2 * LRU_WIDTH + 2 * C_WIDTH

LANE = 128
V_ROWS = HEAD_DIM + 16
VMEM_LIMIT = 56 * 1024 * 1024

NT_DIMS = (((1,), (1,)), ((), ()))


def _cparams(sem, vmem=VMEM_LIMIT):
    return pltpu.CompilerParams(dimension_semantics=sem, vmem_limit_bytes=vmem)


def _const_spec(shape):
    nd = len(shape)
    return pl.BlockSpec(shape, lambda *_: (0,) * nd, pipeline_mode=pl.Buffered(1))


def _rel_bucket_np(dist):
    max_exact = REL_BUCKETS // 2
    d = np.maximum(dist, 0)
    df = np.maximum(d, 1).astype(np.float32)
    large = max_exact + (np.log(df / np.float32(max_exact)) / np.float32(math.log(REL_MAX_DIST / max_exact))
                         * np.float32(REL_BUCKETS - max_exact)).astype(np.int32)
    large = np.minimum(large, REL_BUCKETS - 1)
    return np.where(d < max_exact, d, large).astype(np.int32)


def _bucket_index_a():
    qi = np.arange(A_BLOCK)[None, :] + A_BLOCK
    kj = np.arange(2 * A_BLOCK)[:, None]
    delta = qi - kj
    mats = []
    for window, dil in A_GROUPS:
        band = (delta >= 0) & (delta <= window // dil)
        mats.append(np.where(band, _rel_bucket_np(delta * dil), -1))
    return np.stack(mats).astype(np.int32)


def _moba_far_blocks(nblk):
    for db in range(1, nblk + 1):
        lo = db * MOBA_BLOCK - (MOBA_BLOCK - 1)
        if np.all(_rel_bucket_np(np.arange(lo, nblk * MOBA_BLOCK)) == REL_BUCKETS - 1):
            return db
    return nblk


def _bucket_index_c(ndb):
    k = np.arange(MOBA_BLOCK)[:, None]
    q = np.arange(MOBA_BLOCK)[None, :]
    mats = []
    for db in range(ndb + 1):
        dist = db * MOBA_BLOCK + q - k
        b = _rel_bucket_np(dist)
        if db == 0:
            b = np.where(dist >= 0, b, -1)
        mats.append(b)
    return np.stack(mats).astype(np.int32)


def _bias_kernel(tab_ref, idx_ref, out_ref, *, head_off, unit):
    h = pl.program_id(0) + head_off
    idx = idx_ref[0]
    out = jnp.full(idx.shape, NEG_INF, F32)
    for b in range(REL_BUCKETS):
        out = jnp.where(idx == b, tab_ref[b, h] * unit, out)
    out_ref[0, 0] = out


def _build_bias(rel_bias, idx, n_heads, head_off, heads_per_idx_group, unit=1.0):
    n_mats = idx.shape[0] if heads_per_idx_group is None else 1
    r, c = idx.shape[-2:]
    if heads_per_idx_group is None:
        idx_map = lambda h, m: (m, 0, 0)
    else:
        idx_map = lambda h, m: (h // heads_per_idx_group, 0, 0)
    return pl.pallas_call(
        functools.partial(_bias_kernel, head_off=head_off, unit=unit),
        grid=(n_heads, n_mats),
        in_specs=[pl.BlockSpec(memory_space=pltpu.SMEM),
                  pl.BlockSpec((1, r, c), idx_map)],
        out_specs=pl.BlockSpec((1, 1, r, c), lambda h, m: (h, m, 0, 0)),
        out_shape=jax.ShapeDtypeStruct((n_heads, n_mats, r, c), F32),
        compiler_params=_cparams(("arbitrary", "arbitrary")),
        name="rel_bias_build",
    )(rel_bias, jnp.asarray(idx))


def _col_chunks(start, stop, width=512):
    out = []
    while start < stop:
        w = min(width, stop - start)
        out.append((start, w))
        start += w
    return out


def _in_proj_kernel(x_ref, g_ref, w_ref, wvt_ref, za0_ref, za1_ref, za2_ref, zb_ref, zc_ref, vt_ref, zg_ref,
                    h_scr, dil_scr, *, tm):
    x = x_ref[...]
    ms = jnp.mean(x * x, axis=-1, keepdims=True)
    h_scr[...] = (x * lax.rsqrt(ms + EPS) * g_ref[...]).astype(BF16)

    def proj(base, width):
        return jnp.dot(h_scr[...], w_ref[:, base:base + width], preferred_element_type=F32)

    def seg(out_ref, base, width, scaled_cols=0, scale=1.0, chunk=512):
        assert scaled_cols % chunk == 0
        for off, w in _col_chunks(0, width, chunk):
            r = proj(base + off, w)
            if off < scaled_cols:
                r = r * scale
            out_ref[:, off:off + w] = r.astype(out_ref.dtype)

    gw = 3 * A_OUT
    for g, (za_ref, (_, dil)) in enumerate(zip((za0_ref, za1_ref, za2_ref), A_GROUPS)):
        q = proj(g * gw, A_OUT) * SCALE
        kv = proj(g * gw + A_OUT, 2 * A_OUT)
        if dil == 1:
            za_ref[0, 0, :, :A_OUT] = q.astype(BF16)
            za_ref[0, 0, :, A_OUT:] = kv.astype(BF16)
        else:
            qkv = (q[:, :LANE], q[:, LANE:]) + tuple(kv[:, c * LANE:(c + 1) * LANE] for c in range(4))
            for c, part in enumerate(qkv):
                dil_scr[c] = part
            for r in range(dil):
                for c in range(len(qkv)):
                    za_ref[0, r, :, c * LANE:(c + 1) * LANE] = (
                        dil_scr[c, pl.ds(r, tm // dil, stride=dil), :].astype(BF16))

    wa, wb, wc = 3 * A_WIDTH, 2 * LRU_WIDTH, 2 * C_WIDTH
    seg(zb_ref, wa, wb)
    seg(zc_ref, wa + wb, wc, C_WIDTH, SCALE * LOG2E)
    seg(zg_ref, wa + wb + wc, 3 * D_MODEL)
    vt = lax.dot_general(wvt_ref[...], h_scr[...], NT_DIMS, preferred_element_type=F32).astype(BF16)
    for h in range(C_HEADS):
        vt_ref[0, h * V_ROWS:h * V_ROWS + HEAD_DIM, :] = vt[h * HEAD_DIM:(h + 1) * HEAD_DIM]
        vt_ref[0, h * V_ROWS + HEAD_DIM:(h + 1) * V_ROWS, :] = jnp.ones((V_ROWS - HEAD_DIM, tm), BF16)


def _in_proj(x2d, g, w_in, batch, seq, tm=512):
    n = x2d.shape[0]
    grp = lambda k: jnp.concatenate([w_in[:, s * A_WIDTH + k * A_OUT:s * A_WIDTH + (k + 1) * A_OUT]
                                     for s in range(3)], axis=1)
    w_main = jnp.concatenate([grp(k) for k in range(len(A_GROUPS))]
                             + [w_in[:, 3 * A_WIDTH:V_C_OFF], w_in[:, V_C_OFF + C_WIDTH:]], axis=1).astype(BF16)
    wvt = w_in[:, V_C_OFF:V_C_OFF + C_WIDTH].T.astype(BF16)
    wb, wc, wg = 2 * LRU_WIDTH, 2 * C_WIDTH, 3 * D_MODEL
    gw = 3 * A_OUT
    tiles_per_seq = seq // tm
    row = lambda i: (i, 0)
    seq_tile = lambda i: (i // tiles_per_seq, 0, i % tiles_per_seq, 0)
    za_specs = [pl.BlockSpec((1, dil, tm // dil, gw), seq_tile) for _, dil in A_GROUPS]
    za_shapes = [jax.ShapeDtypeStruct((batch, dil, seq // dil, gw), BF16) for _, dil in A_GROUPS]
    return pl.pallas_call(
        functools.partial(_in_proj_kernel, tm=tm),
        grid=(n // tm,),
        in_specs=[pl.BlockSpec((tm, D_MODEL), row),
                  _const_spec((1, D_MODEL)),
                  _const_spec(w_main.shape),
                  _const_spec(wvt.shape)],
        out_specs=za_specs + [pl.BlockSpec((tm, wb), row),
                              pl.BlockSpec((tm, wc), row),
                              pl.BlockSpec((1, C_HEADS * V_ROWS, tm),
                                           lambda i: (i // tiles_per_seq, 0, i % tiles_per_seq)),
                              pl.BlockSpec((tm, wg), row)],
        out_shape=za_shapes + [jax.ShapeDtypeStruct((n, wb), BF16),
                               jax.ShapeDtypeStruct((n, wc), BF16),
                               jax.ShapeDtypeStruct((batch, C_HEADS * V_ROWS, seq), BF16),
                               jax.ShapeDtypeStruct((n, wg), BF16)],
        scratch_shapes=[pltpu.VMEM((tm, D_MODEL), BF16), pltpu.VMEM((gw // LANE, tm, LANE), F32)],
        compiler_params=_cparams(("arbitrary",)),
        name="in_proj",
    )(x2d, g.reshape(1, D_MODEL), w_main, wvt)


A_TOKENS_PER_STEP = 2048


def _mixer_a_kernel(z_ref, zp_ref, bias_ref, o_ref, lse_ref, *, dil, nsub):
    t = pl.program_id(1)
    blk, width = A_BLOCK, A_OUT
    kcols, vcols = slice(width, 2 * width), slice(2 * width, 3 * width)
    lane_head = lax.broadcasted_iota(jnp.int32, (blk, width), 1) // HEAD_DIM
    in_prev = (lax.broadcasted_iota(jnp.int32, (2 * blk, 2 * blk), 0) < blk).astype(F32)
    no_prev = in_prev * jnp.where(t == 0, NEG_INF, 0.0)
    transpose_v = lambda v: v.astype(F32).T.astype(BF16)

    ones_rows = jnp.ones((V_ROWS - HEAD_DIM, 2 * blk), BF16)
    windows = {}

    def window(r, j):
        if (r, j) not in windows:
            rows = slice(j * blk, (j + 1) * blk)
            if j == 0:
                k_prev, vt_prev = zp_ref[0, r, :, kcols], transpose_v(zp_ref[0, r, :, vcols])
            else:
                _, k_before, vt_before = window(r, j - 1)
                k_prev, vt_prev = k_before[blk:], vt_before[:, blk:]
            k_win = jnp.concatenate([k_prev, z_ref[0, r, rows, kcols]], axis=0)
            vt_win = jnp.concatenate([vt_prev, transpose_v(z_ref[0, r, rows, vcols])], axis=1)
            windows[(r, j)] = (z_ref[0, r, rows, 0:width], k_win, vt_win)
        return windows[(r, j)]

    def logits(r, j, hp):
        q, k_win, _ = window(r, j)
        heads = (2 * hp, 2 * hp + 1)
        q2 = jnp.concatenate([jnp.where(lane_head == h, q, jnp.zeros_like(q)) for h in heads], axis=0)
        bias2 = jnp.concatenate([bias_ref[h, 0] for h in heads], axis=1)
        s = lax.dot_general(k_win, q2, NT_DIMS, preferred_element_type=F32) + bias2
        return s + no_prev if j == 0 else s

    def attend(r, j, hp, s):
        vt_win = window(r, j)[2]
        m = jnp.max(s, axis=0, keepdims=True)
        p = jnp.exp((s - m).astype(BF16))
        vt_ext = jnp.concatenate([part for h in (2 * hp, 2 * hp + 1)
                                  for part in (vt_win[h * HEAD_DIM:(h + 1) * HEAD_DIM, :], ones_rows)], axis=0)
        pv = jnp.dot(vt_ext, p, preferred_element_type=F32)
        res = []
        for u in range(2):
            cols = slice(u * blk, (u + 1) * blk)
            den = pv[u * V_ROWS + HEAD_DIM:u * V_ROWS + HEAD_DIM + 1, cols]
            res.append((pv[u * V_ROWS:u * V_ROWS + HEAD_DIM, cols] / den,
                        jnp.broadcast_to(m[:, cols] + jnp.log(den), (HEAD_DIM, blk))))
        return res

    n_hp = A_HPG // 2
    units = [(r, j, hp) for r in range(dil) for j in range(nsub) for hp in range(n_hp)]
    s_next = logits(*units[0])
    outs, lses = [], []
    for i, (r, j, hp) in enumerate(units):
        s_cur = s_next
        if i + 1 < len(units):
            s_next = logits(*units[i + 1])
        for o, lse in attend(r, j, hp, s_cur):
            outs.append(o)
            lses.append(lse)
        if hp == n_hp - 1:
            o_t = jnp.concatenate(outs, axis=0).T
            lse_t = jnp.concatenate(lses, axis=0).T
            outs, lses = [], []
            dst = slice(j * blk, (j + 1) * blk) if dil == 1 else pl.ds(j * blk * dil + r, blk, stride=dil)
            for c in range(width // LANE):
                o_ref[c, dst, :] = o_t[:, c * LANE:(c + 1) * LANE]
                lse_ref[c, dst, :] = lse_t[:, c * LANE:(c + 1) * LANE]


def _mixer_a_group(za, bias_a, g, dil, batch, seq):
    sub_len = seq // dil
    assert sub_len % A_BLOCK == 0
    tq = min(sub_len, A_TOKENS_PER_STEP // dil)
    assert tq % A_BLOCK == 0
    nsub = tq // A_BLOCK
    gw = 3 * A_OUT
    out_spec = pl.BlockSpec((A_OUT // LANE, tq * dil, LANE), lambda b, t: (0, b * (sub_len // tq) + t, 0))
    return pl.pallas_call(
        functools.partial(_mixer_a_kernel, dil=dil, nsub=nsub),
        grid=(batch, sub_len // tq),
        in_specs=[pl.BlockSpec((1, dil, tq, gw), lambda b, t: (b, 0, t, 0)),
                  pl.BlockSpec((1, dil, A_BLOCK, gw), lambda b, t: (b, 0, jnp.maximum(t * nsub - 1, 0), 0)),
                  pl.BlockSpec((A_HPG, 1, 2 * A_BLOCK, A_BLOCK), lambda b, t: (g, 0, 0, 0))],
        out_specs=[out_spec, out_spec],
        out_shape=[jax.ShapeDtypeStruct((A_OUT // LANE, batch * seq, LANE), F32)] * 2,
        compiler_params=_cparams(("arbitrary", "arbitrary")),
        name=f"mixer_a_d{dil}",
    )(za, za, bias_a)


SUBLANES = 8


def _mixer_b_kernel(z_ref, cw_ref, cb_ref, wax_ref, bax_ref, lam_ref, y_ref,
                    xs_scr, a_scr, b_scr, h_scr, *, ts):
    w = LRU_WIDTH

    @pl.when(pl.program_id(1) == 0)
    def _():
        xs_scr[0:SUBLANES, :] = jnp.zeros((SUBLANES, w), F32)
        h_scr[...] = jnp.zeros((1, w), F32)

    x = z_ref[0, :, 0:w].astype(F32)
    xs_scr[SUBLANES:SUBLANES + ts, :] = x
    xc = cb_ref[...]
    for i in range(CONV_WIDTH):
        off = SUBLANES - (CONV_WIDTH - 1) + i
        xc = xc + xs_scr[off:off + ts, :] * cw_ref[i:i + 1, :]
    xs_scr[0:SUBLANES, :] = xs_scr[ts:ts + SUBLANES, :]

    ra = jnp.dot(xc.astype(BF16), wax_ref[...], preferred_element_type=F32) + bax_ref[...]
    r = jax.nn.sigmoid(ra[:, :w])
    ig = jax.nn.sigmoid(ra[:, w:])
    nl = -lam_ref[...]
    softplus = jnp.maximum(nl, 0.0) + jnp.log1p(jnp.exp(-jnp.abs(nl)))
    log_a = -LRU_C * r * softplus
    a = jnp.exp(log_a)
    a_scr[...] = a
    b_scr[...] = jnp.sqrt(-jnp.tanh(log_a) * (a * a + 1.0)) * (ig * xc)

    row = lax.broadcasted_iota(jnp.int32, (SUBLANES, w), 0)

    def body(c, h):
        i = pl.multiple_of(c * SUBLANES, SUBLANES)
        a = a_scr[pl.ds(i, SUBLANES), :]
        b = b_scr[pl.ds(i, SUBLANES), :]
        for s in (1, 2, 4):
            keep = row >= s
            a_sh = jnp.where(keep, pltpu.roll(a, s, 0), 1.0)
            b_sh = jnp.where(keep, pltpu.roll(b, s, 0), 0.0)
            b = a * b_sh + b
            a = a * a_sh
        hs = a * h + b
        b_scr[pl.ds(i, SUBLANES), :] = hs
        return hs[SUBLANES - 1:SUBLANES, :]

    h_scr[...] = lax.fori_loop(0, ts // SUBLANES, body, h_scr[...])

    gb = z_ref[0, :, w:2 * w].astype(F32)
    cdf = 0.5 * (1.0 + jnp.tanh(math.sqrt(2.0 / math.pi) * (gb + 0.044715 * (gb * gb * gb))))
    y_ref[0] = (b_scr[...] * (gb * cdf)).astype(y_ref.dtype)


def _block_diag(wblocks):
    nb, di, do = wblocks.shape
    eye = jnp.eye(nb, dtype=wblocks.dtype)
    return (eye[:, None, :, None] * wblocks[:, :, None, :]).reshape(nb * di, nb * do)


def _mixer_b(zb, conv_w, conv_b, lru_wa, lru_ba, lru_wx, lru_bx, lru_lam, batch, seq, ts=512):
    w = LRU_WIDTH
    wax = jnp.concatenate([_block_diag(lru_wa), _block_diag(lru_wx)], axis=1).astype(BF16)
    bax = jnp.concatenate([lru_ba, lru_bx]).reshape(1, 2 * w)
    y = pl.pallas_call(
        functools.partial(_mixer_b_kernel, ts=ts),
        grid=(batch, seq // ts),
        in_specs=[pl.BlockSpec((1, ts, 2 * w), lambda b, t: (b, t, 0)),
                  _const_spec((CONV_WIDTH, w)), _const_spec((1, w)),
                  _const_spec((w, 2 * w)), _const_spec((1, 2 * w)), _const_spec((1, w))],
        out_specs=pl.BlockSpec((1, ts, w), lambda b, t: (b, t, 0)),
        out_shape=jax.ShapeDtypeStruct((batch, seq, w), BF16),
        scratch_shapes=[pltpu.VMEM((ts + SUBLANES, w), F32), pltpu.VMEM((ts, w), F32),
                        pltpu.VMEM((ts, w), F32), pltpu.VMEM((1, w), F32)],
        compiler_params=_cparams(("arbitrary", "arbitrary")),
        name="mixer_b",
    )(zb.reshape(batch, seq, 2 * w), conv_w, conv_b.reshape(1, w), wax, bax, lru_lam.reshape(1, w))
    return y.reshape(batch * seq, w)


HEAD_PAIR = LANE // HEAD_DIM


def _mixer_c_prologue(q_ref, k_ref, qm_scr, sel_scr, *, nblk):
    blk = MOBA_BLOCK
    seq = nblk * blk
    kmean = jnp.concatenate(
        [jnp.sum(k_ref[0, n * blk:(n + 1) * blk, :].astype(F32), axis=0, keepdims=True) for n in range(nblk)],
        axis=0) * (1.0 / blk)
    km_hi = kmean.astype(BF16)
    km_lo = (kmean - km_hi.astype(F32)).astype(BF16)
    q = q_ref[0]
    lane = lax.broadcasted_iota(jnp.int32, q.shape, 1)
    blk_id = lax.broadcasted_iota(jnp.int32, (nblk, seq), 0)
    q_blk = lax.broadcasted_iota(jnp.int32, (nblk, seq), 1) // blk
    past = blk_id < q_blk
    for hh in range(HEAD_PAIR):
        qm = jnp.where((lane >= hh * HEAD_DIM) & (lane < (hh + 1) * HEAD_DIM), q, jnp.zeros_like(q))
        qm_scr[hh] = qm
        gate = (lax.dot_general(km_hi, qm, NT_DIMS, preferred_element_type=F32)
                + lax.dot_general(km_lo, qm, NT_DIMS, preferred_element_type=F32))
        gate = jnp.where(past, gate, NEG_INF)
        sel = jnp.zeros((nblk, seq), F32)
        for _ in range(min(MOBA_TOPK, nblk - 1)):
            top = jnp.max(gate, axis=0, keepdims=True)
            first = jnp.min(jnp.where(gate == top, blk_id, nblk), axis=0, keepdims=True)
            pick = blk_id == first
            sel = jnp.where(pick, 1.0, sel)
            gate = jnp.where(pick, -jnp.inf, gate)
        sel_scr[hh] = jnp.where(past, sel, 0.0)


def _mixer_c_block(c, k_ref, vt_ref, bias_ref, o_ref, qm_scr, sel_scr, s_scr, p_scr, *, ndb):
    blk = MOBA_BLOCK
    nk = (c + 1) * blk
    qcols = slice(c * blk, (c + 1) * blk)
    fold = lambda t: t.reshape(blk // SUBLANES, SUBLANES, blk)
    heads = range(HEAD_PAIR)
    picked = [[sel_scr[hh, kb:kb + 1, qcols] > 0.0 for kb in range(c)] for hh in heads]
    far_bias = [bias_ref[hh, ndb, 0:1, :] for hh in heads]

    pairs = [(k0, min(k0 + 2, c + 1)) for k0 in range(0, c + 1, 2)]

    def logits(hh, res):
        qm = qm_scr[hh, qcols, :]
        m8 = jnp.full((SUBLANES, blk), NEG_INF, F32)
        for k0, k1 in pairs:
            s = lax.dot_general(k_ref[0, k0 * blk:k1 * blk, :], qm, NT_DIMS, preferred_element_type=F32)
            for kb in range(k0, k1):
                su = s[(kb - k0) * blk:(kb - k0 + 1) * blk]
                if c - kb < ndb:
                    su = su + bias_ref[hh, c - kb]
                    t8 = jnp.max(fold(su), axis=0)
                else:
                    t8 = jnp.max(fold(su), axis=0) + far_bias[hh]
                s_scr[hh, kb * blk:(kb + 1) * blk, :] = su
                m8 = jnp.maximum(m8, t8 if kb == c else jnp.where(picked[hh][kb], t8, NEG_INF))
            yield
        res[hh] = jnp.max(m8, axis=0, keepdims=True)

    def attend(hh, m, res):
        pv = jnp.zeros((V_ROWS, blk), F32)
        for k0, k1 in pairs:
            for kb in range(k0, k1):
                mk = m if kb == c else jnp.where(picked[hh][kb], m, -NEG_INF)
                if c - kb >= ndb:
                    mk = mk - far_bias[hh]
                x = s_scr[hh, kb * blk:(kb + 1) * blk, :] - mk
                p_scr[hh, kb * blk:(kb + 1) * blk, :] = jnp.exp2(x.astype(BF16))
            pv = pv + jnp.dot(vt_ref[0, hh * V_ROWS:(hh + 1) * V_ROWS, k0 * blk:k1 * blk],
                              p_scr[hh, k0 * blk:k1 * blk, :], preferred_element_type=F32)
            yield
        res[hh] = pv[:HEAD_DIM] / pv[HEAD_DIM:HEAD_DIM + 1]

    def run(*gens):
        gens = list(gens)
        while gens:
            gens = [g for g in gens if next(g, StopIteration) is not StopIteration]

    ms, outs = {}, {}
    for hh in heads:
        run(logits(hh, ms))
    for hh in heads:
        run(attend(hh, ms[hh], outs))
    o_ref[0, qcols, :] = jnp.concatenate([outs[hh] for hh in heads], axis=0).T.astype(o_ref.dtype)


def _mixer_c_kernel(q_ref, k_ref, vt_ref, bias_ref, o_ref, qm_scr, sel_scr, s_scr, p_scr, *, nblk, ndb):
    _mixer_c_prologue(q_ref, k_ref, qm_scr, sel_scr, nblk=nblk)

    def query_block(qb, carry):
        for c in range(nblk):
            pl.when(qb == c)(functools.partial(_mixer_c_block, c, k_ref, vt_ref, bias_ref, o_ref,
                                               qm_scr, sel_scr, s_scr, p_scr, ndb=ndb))
        return carry

    lax.fori_loop(0, nblk, query_block, 0)


def _mixer_c(zc, vt, bias_c, batch, seq, ndb):
    blk = MOBA_BLOCK
    assert seq % blk == 0
    nblk = seq // blk
    npair = C_HEADS // HEAD_PAIR
    z = zc.reshape(batch, seq, 2 * C_WIDTH)
    o = pl.pallas_call(
        functools.partial(_mixer_c_kernel, nblk=nblk, ndb=ndb),
        grid=(batch, npair),
        in_specs=[pl.BlockSpec((1, seq, LANE), lambda b, hp: (b, 0, hp)),
                  pl.BlockSpec((1, seq, LANE), lambda b, hp: (b, 0, npair + hp)),
                  pl.BlockSpec((1, HEAD_PAIR * V_ROWS, seq), lambda b, hp: (b, hp, 0)),
                  pl.BlockSpec((HEAD_PAIR, ndb + 1, blk, blk), lambda b, hp: (hp, 0, 0, 0))],
        out_specs=pl.BlockSpec((1, seq, LANE), lambda b, hp: (b, 0, hp)),
        out_shape=jax.ShapeDtypeStruct((batch, seq, C_WIDTH), BF16),
        scratch_shapes=[pltpu.VMEM((HEAD_PAIR, seq, LANE), BF16), pltpu.VMEM((HEAD_PAIR, nblk, seq), F32),
                        pltpu.VMEM((HEAD_PAIR, seq, blk), F32), pltpu.VMEM((HEAD_PAIR, seq, blk), BF16)],
        compiler_params=_cparams(("arbitrary", "arbitrary")),
        name="mixer_c",
    )(z, z, vt, bias_c)
    return o.reshape(batch * seq, C_WIDTH)


def _merge_kernel(x_ref, o1_ref, o2_ref, o3_ref, l1_ref, l2_ref, l3_ref, ob_ref, oc_ref, zg_ref,
                  pa_ref, pb_ref, pc_ref, wo_ref, out_ref):
    d = D_MODEL
    halves = lambda ref: jnp.concatenate([ref[c] for c in range(ref.shape[0])], axis=-1)
    l1, l2, l3 = halves(l1_ref), halves(l2_ref), halves(l3_ref)
    m = jnp.maximum(jnp.maximum(l1, l2), l3)
    e1, e2, e3 = jnp.exp(l1 - m), jnp.exp(l2 - m), jnp.exp(l3 - m)
    o_a = (e1 * halves(o1_ref) + e2 * halves(o2_ref) + e3 * halves(o3_ref)) / (e1 + e2 + e3)

    def branch(o, p_ref, k):
        gate = jax.nn.sigmoid(zg_ref[:, k * d:(k + 1) * d].astype(F32))
        return gate * jnp.dot(o, p_ref[...], preferred_element_type=F32)

    merged = branch(o_a.astype(BF16), pa_ref, 0) + branch(ob_ref[...], pb_ref, 1) + branch(oc_ref[...], pc_ref, 2)
    out_ref[...] = x_ref[...] + jnp.dot(merged.astype(BF16), wo_ref[...], preferred_element_type=F32)


def _merge(x2d, oa, lse, ob, oc, zg, p_a, p_b, p_c, w_out, tm=512):
    n = x2d.shape[0]
    row = lambda width: pl.BlockSpec((tm, width), lambda i: (i, 0))
    return pl.pallas_call(
        _merge_kernel,
        grid=(n // tm,),
        in_specs=[row(D_MODEL)] + [pl.BlockSpec((A_OUT // LANE, tm, LANE), lambda i: (0, i, 0))] * 6
                 + [row(LRU_WIDTH), row(C_WIDTH), row(3 * D_MODEL),
                  _const_spec((A_OUT, D_MODEL)), _const_spec((LRU_WIDTH, D_MODEL)),
                  _const_spec((C_WIDTH, D_MODEL)), _const_spec((D_MODEL, D_MODEL))],
        out_specs=row(D_MODEL),
        out_shape=jax.ShapeDtypeStruct((n, D_MODEL), F32),
        compiler_params=_cparams(("arbitrary",)),
        name="merge",
    )(x2d, *oa, *lse, ob, oc, zg, p_a.astype(BF16), p_b.astype(BF16), p_c.astype(BF16), w_out.astype(BF16))


def _rms(x, g):
    return x * lax.rsqrt(jnp.mean(x * x, axis=-1, keepdims=True) + EPS) * g


def _ffn_kernel(x_ref, g_ref, wgu_ref, wd_ref, gf_ref, out_ref, h_scr, act_scr, *, final_norm):
    x = x_ref[...]
    h_scr[...] = _rms(x, g_ref[...]).astype(BF16)
    for off, w in _col_chunks(0, FFN_HIDDEN, 256):
        gate = jnp.dot(h_scr[...], wgu_ref[:, off:off + w], preferred_element_type=F32)
        up = jnp.dot(h_scr[...], wgu_ref[:, FFN_HIDDEN + off:FFN_HIDDEN + off + w], preferred_element_type=F32)
        act_scr[:, off:off + w] = (gate * jax.nn.sigmoid(gate) * up).astype(BF16)
    y = x + jnp.dot(act_scr[...], wd_ref[...], preferred_element_type=F32)
    if final_norm:
        y = _rms(y, gf_ref[...])
    out_ref[...] = y


def _ffn(x2d, g_ffn, w_gu, w_down, g_final, final_norm, tm=512):
    n = x2d.shape[0]
    row = pl.BlockSpec((tm, D_MODEL), lambda i: (i, 0))
    return pl.pallas_call(
        functools.partial(_ffn_kernel, final_norm=final_norm),
        grid=(n // tm,),
        in_specs=[row, _const_spec((1, D_MODEL)), _const_spec((D_MODEL, 2 * FFN_HIDDEN)),
                  _const_spec((FFN_HIDDEN, D_MODEL)), _const_spec((1, D_MODEL))],
        out_specs=row,
        out_shape=jax.ShapeDtypeStruct((n, D_MODEL), F32),
        scratch_shapes=[pltpu.VMEM((tm, D_MODEL), BF16), pltpu.VMEM((tm, FFN_HIDDEN), BF16)],
        compiler_params=_cparams(("arbitrary",)),
        name="ffn",
    )(x2d, g_ffn.reshape(1, D_MODEL), w_gu.astype(BF16), w_down.astype(BF16), g_final.reshape(1, D_MODEL))


def kernel(x, rel_bias, g_mix, w_in, conv_w, conv_b, lru_wa, lru_ba, lru_wx, lru_bx, lru_lam,
           p_a, p_b, p_c, w_out, g_ffn, w_gu, w_down, g_final):
    batch, seq, d = x.shape
    assert d == D_MODEL and w_in.shape[-1] == IN_COLS
    depth = w_in.shape[0]
    nblk = seq // MOBA_BLOCK
    ndb = _moba_far_blocks(nblk)

    bias_a = _build_bias(rel_bias, _bucket_index_a(), A_HEADS, 0, A_HPG)
    bias_c = _build_bias(rel_bias, _bucket_index_c(ndb), C_HEADS, A_HEADS, None, LOG2E)

    x2d = x.reshape(batch * seq, d)
    for l in range(depth):
        *za, zb, zc, vt, zg = _in_proj(x2d, g_mix[l], w_in[l], batch, seq)
        oa, lse = zip(*[_mixer_a_group(za[g], bias_a, g, dil, batch, seq)
                        for g, (_, dil) in enumerate(A_GROUPS)])
        ob = _mixer_b(zb, conv_w[l], conv_b[l], lru_wa[l], lru_ba[l], lru_wx[l], lru_bx[l], lru_lam[l],
                      batch, seq)
        oc = _mixer_c(zc, vt, bias_c, batch, seq, ndb)
        x2d = _merge(x2d, oa, lse, ob, oc, zg, p_a[l], p_b[l], p_c[l], w_out[l])
        x2d = _ffn(x2d, g_ffn[l], w_gu[l], w_down[l], g_final, final_norm=(l == depth - 1))
    return x2d.reshape(batch, seq, d)
```

```python
import functools
import math

import numpy as np
import jax
import jax.numpy as jnp
from jax import lax
from jax.experimental import pallas as pl
from jax.experimental.pallas import tpu as pltpu

F32 = jnp.float32
BF16 = jnp.bfloat16

D_MODEL = 1024
HEAD_DIM = 64
SCALE = HEAD_DIM ** -0.5
LOG2E = math.log2(math.e)
NEG_INF = -1e30
EPS = 1e-6
A_GROUPS = ((128, 1), (512, 4), (2048, 16))
A_HPG = 4
A_HEADS = A_HPG * len(A_GROUPS)
A_WIDTH = A_HEADS * HEAD_DIM
A_OUT = A_HPG * HEAD_DIM
A_BLOCK = 128
LRU_WIDTH = D_MODEL // 2
LRU_BLOCKS = 8
CONV_WIDTH = 4
LRU_C = 8.0
C_HEADS = 8
C_WIDTH = C_HEADS * HEAD_DIM
MOBA_BLOCK = 256
MOBA_TOPK = 3
REL_BUCKETS = 32
REL_MAX_DIST = 2048
FFN_HIDDEN = 2816
IN_COLS = 3 * A_WIDTH + 2 * LRU_WIDTH + 3 * C_WIDTH + 3 * D_MODEL
V_C_OFF = 3 * A_WIDTH + 2 * LRU_WIDTH + 2 * C_WIDTH

LANE = 128
V_ROWS = HEAD_DIM + 16
VMEM_LIMIT = 56 * 1024 * 1024

NT_DIMS = (((1,), (1,)), ((), ()))


def _cparams(sem, vmem=VMEM_LIMIT):
    return pltpu.CompilerParams(dimension_semantics=sem, vmem_limit_bytes=vmem)


def _const_spec(shape):
    nd = len(shape)
    return pl.BlockSpec(shape, lambda *_: (0,) * nd, pipeline_mode=pl.Buffered(1))


def _rel_bucket_np(dist):
    max_exact = REL_BUCKETS // 2
    d = np.maximum(dist, 0)
    df = np.maximum(d, 1).astype(np.float32)
    large = max_exact + (np.log(df / np.float32(max_exact)) / np.float32(math.log(REL_MAX_DIST / max_exact))
                         * np.float32(REL_BUCKETS - max_exact)).astype(np.int32)
    large = np.minimum(large, REL_BUCKETS - 1)
    return np.where(d < max_exact, d, large).astype(np.int32)


def _bucket_index_a():
    qi = np.arange(A_BLOCK)[None, :] + A_BLOCK
    kj = np.arange(2 * A_BLOCK)[:, None]
    delta = qi - kj
    mats = []
    for window, dil in A_GROUPS:
        band = (delta >= 0) & (delta <= window // dil)
        mats.append(np.where(band, _rel_bucket_np(delta * dil), -1))
    return np.stack(mats).astype(np.int32)


def _moba_far_blocks(nblk):
    for db in range(1, nblk + 1):
        lo = db * MOBA_BLOCK - (MOBA_BLOCK - 1)
        if np.all(_rel_bucket_np(np.arange(lo, nblk * MOBA_BLOCK)) == REL_BUCKETS - 1):
            return db
    return nblk


def _bucket_index_c(ndb):
    k = np.arange(MOBA_BLOCK)[:, None]
    q = np.arange(MOBA_BLOCK)[None, :]
    mats = []
    for db in range(ndb + 1):
        dist = db * MOBA_BLOCK + q - k
        b = _rel_bucket_np(dist)
        if db == 0:
            b = np.where(dist >= 0, b, -1)
        mats.append(b)
    return np.stack(mats).astype(np.int32)


def _bias_kernel(tab_ref, idx_ref, out_ref, *, head_off, unit):
    h = pl.program_id(0) + head_off
    idx = idx_ref[0]
    out = jnp.full(idx.shape, NEG_INF, F32)
    for b in range(REL_BUCKETS):
        out = jnp.where(idx == b, tab_ref[b, h] * unit, out)
    out_ref[0, 0] = out


def _build_bias(rel_bias, idx, n_heads, head_off, heads_per_idx_group, unit=1.0):
    n_mats = idx.shape[0] if heads_per_idx_group is None else 1
    r, c = idx.shape[-2:]
    if heads_per_idx_group is None:
        idx_map = lambda h, m: (m, 0, 0)
    else:
        idx_map = lambda h, m: (h // heads_per_idx_group, 0, 0)
    return pl.pallas_call(
        functools.partial(_bias_kernel, head_off=head_off, unit=unit),
        grid=(n_heads, n_mats),
        in_specs=[pl.BlockSpec(memory_space=pltpu.SMEM),
                  pl.BlockSpec((1, r, c), idx_map)],
        out_specs=pl.BlockSpec((1, 1, r, c), lambda h, m: (h, m, 0, 0)),
        out_shape=jax.ShapeDtypeStruct((n_heads, n_mats, r, c), F32),
        compiler_params=_cparams(("arbitrary", "arbitrary")),
        name="rel_bias_build",
    )(rel_bias, jnp.asarray(idx))


def _col_chunks(start, stop, width=512):
    out = []
    while start < stop:
        w = min(width, stop - start)
        out.append((start, w))
        start += w
    return out


def _in_proj_kernel(x_ref, g_ref, w_ref, wvt_ref, za0_ref, za1_ref, za2_ref, zb_ref, zc_ref, vt_ref, zg_ref,
                    h_scr, dil_scr, *, tm):
    x = x_ref[...]
    ms = jnp.mean(x * x, axis=-1, keepdims=True)
    h_scr[...] = (x * lax.rsqrt(ms + EPS) * g_ref[...]).astype(BF16)

    def proj(base, width):
        return jnp.dot(h_scr[...], w_ref[:, base:base + width], preferred_element_type=F32)

    def seg(out_ref, base, width, scaled_cols=0, scale=1.0, chunk=512):
        assert scaled_cols % chunk == 0
        for off, w in _col_chunks(0, width, chunk):
            r = proj(base + off, w)
            if off < scaled_cols:
                r = r * scale
            out_ref[:, off:off + w] = r.astype(out_ref.dtype)

    gw = 3 * A_OUT
    for g, (za_ref, (_, dil)) in enumerate(zip((za0_ref, za1_ref, za2_ref), A_GROUPS)):
        q = proj(g * gw, A_OUT) * SCALE
        kv = proj(g * gw + A_OUT, 2 * A_OUT)
        if dil == 1:
            za_ref[0, 0, :, :A_OUT] = q.astype(BF16)
            za_ref[0, 0, :, A_OUT:] = kv.astype(BF16)
        else:
            qkv = (q[:, :LANE], q[:, LANE:]) + tuple(kv[:, c * LANE:(c + 1) * LANE] for c in range(4))
            for c, part in enumerate(qkv):
                dil_scr[c] = part
            for r in range(dil):
                for c in range(len(qkv)):
                    za_ref[0, r, :, c * LANE:(c + 1) * LANE] = (
                        dil_scr[c, pl.ds(r, tm // dil, stride=dil), :].astype(BF16))

    wa, wb, wc = 3 * A_WIDTH, 2 * LRU_WIDTH, 2 * C_WIDTH
    seg(zb_ref, wa, wb)
    seg(zc_ref, wa + wb, wc, C_WIDTH, SCALE * LOG2E)
    seg(zg_ref, wa + wb + wc, 3 * D_MODEL)
    vt = lax.dot_general(wvt_ref[...], h_scr[...], NT_DIMS, preferred_element_type=F32).astype(BF16)
    for h in range(C_HEADS):
        vt_ref[0, h * V_ROWS:h * V_ROWS + HEAD_DIM, :] = vt[h * HEAD_DIM:(h + 1) * HEAD_DIM]
        vt_ref[0, h * V_ROWS + HEAD_DIM:(h + 1) * V_ROWS, :] = jnp.ones((V_ROWS - HEAD_DIM, tm), BF16)


def _in_proj(x2d, g, w_in, batch, seq, tm=512):
    n = x2d.shape[0]
    grp = lambda k: jnp.concatenate([w_in[:, s * A_WIDTH + k * A_OUT:s * A_WIDTH + (k + 1) * A_OUT]
                                     for s in range(3)], axis=1)
    w_main = jnp.concatenate([grp(k) for k in range(len(A_GROUPS))]
                             + [w_in[:, 3 * A_WIDTH:V_C_OFF], w_in[:, V_C_OFF + C_WIDTH:]], axis=1).astype(BF16)
    wvt = w_in[:, V_C_OFF:V_C_OFF + C_WIDTH].T.astype(BF16)
    wb, wc, wg = 2 * LRU_WIDTH, 2 * C_WIDTH, 3 * D_MODEL
    gw = 3 * A_OUT
    tiles_per_seq = seq // tm
    row = lambda i: (i, 0)
    seq_tile = lambda i: (i // tiles_per_seq, 0, i % tiles_per_seq, 0)
    za_specs = [pl.BlockSpec((1, dil, tm // dil, gw), seq_tile) for _, dil in A_GROUPS]
    za_shapes = [jax.ShapeDtypeStruct((batch, dil, seq // dil, gw), BF16) for _, dil in A_GROUPS]
    return pl.pallas_call(
        functools.partial(_in_proj_kernel, tm=tm),
        grid=(n // tm,),
        in_specs=[pl.BlockSpec((tm, D_MODEL), row),
                  _const_spec((1, D_MODEL)),
                  _const_spec(w_main.shape),
                  _const_spec(wvt.shape)],
        out_specs=za_specs + [pl.BlockSpec((tm, wb), row),
                              pl.BlockSpec((tm, wc), row),
                              pl.BlockSpec((1, C_HEADS * V_ROWS, tm),
                                           lambda i: (i // tiles_per_seq, 0, i % tiles_per_seq)),
                              pl.BlockSpec((tm, wg), row)],
        out_shape=za_shapes + [jax.ShapeDtypeStruct((n, wb), BF16),
                               jax.ShapeDtypeStruct((n, wc), BF16),
                               jax.ShapeDtypeStruct((batch, C_HEADS * V_ROWS, seq), BF16),
                               jax.ShapeDtypeStruct((n, wg), BF16)],
        scratch_shapes=[pltpu.VMEM((tm, D_MODEL), BF16), pltpu.VMEM((gw // LANE, tm, LANE), F32)],
        compiler_params=_cparams(("arbitrary",)),
        name="in_proj",
    )(x2d, g.reshape(1, D_MODEL), w_main, wvt)


A_TOKENS_PER_STEP = 2048


def _mixer_a_kernel(z_ref, zp_ref, bias_ref, o_ref, lse_ref, *, dil, nsub):
    t = pl.program_id(1)
    blk, width = A_BLOCK, A_OUT
    kcols, vcols = slice(width, 2 * width), slice(2 * width, 3 * width)
    lane_head = lax.broadcasted_iota(jnp.int32, (blk, width), 1) // HEAD_DIM
    in_prev = (lax.broadcasted_iota(jnp.int32, (2 * blk, 2 * blk), 0) < blk).astype(F32)
    no_prev = in_prev * jnp.where(t == 0, NEG_INF, 0.0)
    transpose_v = lambda v: v.astype(F32).T.astype(BF16)

    ones_rows = jnp.ones((V_ROWS - HEAD_DIM, 2 * blk), BF16)
    windows = {}

    def window(r, j):
        if (r, j) not in windows:
            rows = slice(j * blk, (j + 1) * blk)
            if j == 0:
                k_prev, vt_prev = zp_ref[0, r, :, kcols], transpose_v(zp_ref[0, r, :, vcols])
            else:
                _, k_before, vt_before = window(r, j - 1)
                k_prev, vt_prev = k_before[blk:], vt_before[:, blk:]
            k_win = jnp.concatenate([k_prev, z_ref[0, r, rows, kcols]], axis=0)
            vt_win = jnp.concatenate([vt_prev, transpose_v(z_ref[0, r, rows, vcols])], axis=1)
            windows[(r, j)] = (z_ref[0, r, rows, 0:width], k_win, vt_win)
        return windows[(r, j)]

    def logits(r, j, hp):
        q, k_win, _ = window(r, j)
        heads = (2 * hp, 2 * hp + 1)
        q2 = jnp.concatenate([jnp.where(lane_head == h, q, jnp.zeros_like(q)) for h in heads], axis=0)
        bias2 = jnp.concatenate([bias_ref[h, 0] for h in heads], axis=1)
        s = lax.dot_general(k_win, q2, NT_DIMS, preferred_element_type=F32) + bias2
        return s + no_prev if j == 0 else s

    def attend(r, j, hp, s):
        vt_win = window(r, j)[2]
        m = jnp.max(s, axis=0, keepdims=True)
        p = jnp.exp((s - m).astype(BF16))
        vt_ext = jnp.concatenate([part for h in (2 * hp, 2 * hp + 1)
                                  for part in (vt_win[h * HEAD_DIM:(h + 1) * HEAD_DIM, :], ones_rows)], axis=0)
        pv = jnp.dot(vt_ext, p, preferred_element_type=F32)
        res = []
        for u in range(2):
            cols = slice(u * blk, (u + 1) * blk)
            den = pv[u * V_ROWS + HEAD_DIM:u * V_ROWS + HEAD_DIM + 1, cols]
            res.append((pv[u * V_ROWS:u * V_ROWS + HEAD_DIM, cols] / den,
                        jnp.broadcast_to(m[:, cols] + jnp.log(den), (HEAD_DIM, blk))))
        return res

    n_hp = A_HPG // 2
    units = [(r, j, hp) for r in range(dil) for j in range(nsub) for hp in range(n_hp)]
    s_next = logits(*units[0])
    outs, lses = [], []
    for i, (r, j, hp) in enumerate(units):
        s_cur = s_next
        if i + 1 < len(units):
            s_next = logits(*units[i + 1])
        for o, lse in attend(r, j, hp, s_cur):
            outs.append(o)
            lses.append(lse)
        if hp == n_hp - 1:
            o_t = jnp.concatenate(outs, axis=0).T
            lse_t = jnp.concatenate(lses, axis=0).T
            outs, lses = [], []
            dst = slice(j * blk, (j + 1) * blk) if dil == 1 else pl.ds(j * blk * dil + r, blk, stride=dil)
            for c in range(width // LANE):
                o_ref[c, dst, :] = o_t[:, c * LANE:(c + 1) * LANE]
                lse_ref[c, dst, :] = lse_t[:, c * LANE:(c + 1) * LANE]


def _mixer_a_group(za, bias_a, g, dil, batch, seq):
    sub_len = seq // dil
    assert sub_len % A_BLOCK == 0
    tq = min(sub_len, A_TOKENS_PER_STEP // dil)
    assert tq % A_BLOCK == 0
    nsub = tq // A_BLOCK
    gw = 3 * A_OUT
    out_spec = pl.BlockSpec((A_OUT // LANE, tq * dil, LANE), lambda b, t: (0, b * (sub_len // tq) + t, 0))
    return pl.pallas_call(
        functools.partial(_mixer_a_kernel, dil=dil, nsub=nsub),
        grid=(batch, sub_len // tq),
        in_specs=[pl.BlockSpec((1, dil, tq, gw), lambda b, t: (b, 0, t, 0)),
                  pl.BlockSpec((1, dil, A_BLOCK, gw), lambda b, t: (b, 0, jnp.maximum(t * nsub - 1, 0), 0)),
                  pl.BlockSpec((A_HPG, 1, 2 * A_BLOCK, A_BLOCK), lambda b, t: (g, 0, 0, 0))],
        out_specs=[out_spec, out_spec],
        out_shape=[jax.ShapeDtypeStruct((A_OUT // LANE, batch * seq, LANE), F32)] * 2,
        compiler_params=_cparams(("arbitrary", "arbitrary")),
        name=f"mixer_a_d{dil}",
    )(za, za, bias_a)


SUBLANES = 8


def _mixer_b_kernel(z_ref, cw_ref, cb_ref, wax_ref, bax_ref, lam_ref, y_ref,
                    xs_scr, a_scr, b_scr, h_scr, *, ts):
    w = LRU_WIDTH

    @pl.when(pl.program_id(1) == 0)
    def _():
        xs_scr[0:SUBLANES, :] = jnp.zeros((SUBLANES, w), F32)
        h_scr[...] = jnp.zeros((1, w), F32)

    x = z_ref[0, :, 0:w].astype(F32)
    xs_scr[SUBLANES:SUBLANES + ts, :] = x
    xc = cb_ref[...]
    for i in range(CONV_WIDTH):
        off = SUBLANES - (CONV_WIDTH - 1) + i
        xc = xc + xs_scr[off:off + ts, :] * cw_ref[i:i + 1, :]
    xs_scr[0:SUBLANES, :] = xs_scr[ts:ts + SUBLANES, :]

    ra = jnp.dot(xc.astype(BF16), wax_ref[...], preferred_element_type=F32) + bax_ref[...]
    r = jax.nn.sigmoid(ra[:, :w])
    ig = jax.nn.sigmoid(ra[:, w:])
    nl = -lam_ref[...]
    softplus = jnp.maximum(nl, 0.0) + jnp.log1p(jnp.exp(-jnp.abs(nl)))
    log_a = -LRU_C * r * softplus
    a = jnp.exp(log_a)
    a_scr[...] = a
    b_scr[...] = jnp.sqrt(-jnp.tanh(log_a) * (a * a + 1.0)) * (ig * xc)

    row = lax.broadcasted_iota(jnp.int32, (SUBLANES, w), 0)

    def body(c, h):
        i = pl.multiple_of(c * SUBLANES, SUBLANES)
        a = a_scr[pl.ds(i, SUBLANES), :]
        b = b_scr[pl.ds(i, SUBLANES), :]
        for s in (1, 2, 4):
            keep = row >= s
            a_sh = jnp.where(keep, pltpu.roll(a, s, 0), 1.0)
            b_sh = jnp.where(keep, pltpu.roll(b, s, 0), 0.0)
            b = a * b_sh + b
            a = a * a_sh
        hs = a * h + b
        b_scr[pl.ds(i, SUBLANES), :] = hs
        return hs[SUBLANES - 1:SUBLANES, :]

    h_scr[...] = lax.fori_loop(0, ts // SUBLANES, body, h_scr[...])

    gb = z_ref[0, :, w:2 * w].astype(F32)
    cdf = 0.5 * (1.0 + jnp.tanh(math.sqrt(2.0 / math.pi) * (gb + 0.044715 * (gb * gb * gb))))
    y_ref[0] = (b_scr[...] * (gb * cdf)).astype(y_ref.dtype)


def _block_diag(wblocks):
    nb, di, do = wblocks.shape
    eye = jnp.eye(nb, dtype=wblocks.dtype)
    return (eye[:, None, :, None] * wblocks[:, :, None, :]).reshape(nb * di, nb * do)


def _mixer_b(zb, conv_w, conv_b, lru_wa, lru_ba, lru_wx, lru_bx, lru_lam, batch, seq, ts=512):
    w = LRU_WIDTH
    wax = jnp.concatenate([_block_diag(lru_wa), _block_diag(lru_wx)], axis=1).astype(BF16)
    bax = jnp.concatenate([lru_ba, lru_bx]).reshape(1, 2 * w)
    y = pl.pallas_call(
        functools.partial(_mixer_b_kernel, ts=ts),
        grid=(batch, seq // ts),
        in_specs=[pl.BlockSpec((1, ts, 2 * w), lambda b, t: (b, t, 0)),
                  _const_spec((CONV_WIDTH, w)), _const_spec((1, w)),
                  _const_spec((w, 2 * w)), _const_spec((1, 2 * w)), _const_spec((1, w))],
        out_specs=pl.BlockSpec((1, ts, w), lambda b, t: (b, t, 0)),
        out_shape=jax.ShapeDtypeStruct((batch, seq, w), BF16),
        scratch_shapes=[pltpu.VMEM((ts + SUBLANES, w), F32), pltpu.VMEM((ts, w), F32),
                        pltpu.VMEM((ts, w), F32), pltpu.VMEM((1, w), F32)],
        compiler_params=_cparams(("arbitrary", "arbitrary")),
        name="mixer_b",
    )(zb.reshape(batch, seq, 2 * w), conv_w, conv_b.reshape(1, w), wax, bax, lru_lam.reshape(1, w))
    return y.reshape(batch * seq, w)


HEAD_PAIR = LANE // HEAD_DIM


def _mixer_c_prologue(q_ref, k_ref, qm_scr, sel_scr, *, nblk):
    blk = MOBA_BLOCK
    seq = nblk * blk
    kmean = jnp.concatenate(
        [jnp.sum(k_ref[0, n * blk:(n + 1) * blk, :].astype(F32), axis=0, keepdims=True) for n in range(nblk)],
        axis=0) * (1.0 / blk)
    km_hi = kmean.astype(BF16)
    km_lo = (kmean - km_hi.astype(F32)).astype(BF16)
    q = q_ref[0]
    lane = lax.broadcasted_iota(jnp.int32, q.shape, 1)
    blk_id = lax.broadcasted_iota(jnp.int32, (nblk, seq), 0)
    q_blk = lax.broadcasted_iota(jnp.int32, (nblk, seq), 1) // blk
    past = blk_id < q_blk
    for hh in range(HEAD_PAIR):
        qm = jnp.where((lane >= hh * HEAD_DIM) & (lane < (hh + 1) * HEAD_DIM), q, jnp.zeros_like(q))
        qm_scr[hh] = qm
        gate = (lax.dot_general(km_hi, qm, NT_DIMS, preferred_element_type=F32)
                + lax.dot_general(km_lo, qm, NT_DIMS, preferred_element_type=F32))
        gate = jnp.where(past, gate, NEG_INF)
        sel = jnp.zeros((nblk, seq), F32)
        for _ in range(min(MOBA_TOPK, nblk - 1)):
            top = jnp.max(gate, axis=0, keepdims=True)
            first = jnp.min(jnp.where(gate == top, blk_id, nblk), axis=0, keepdims=True)
            pick = blk_id == first
            sel = jnp.where(pick, 1.0, sel)
            gate = jnp.where(pick, -jnp.inf, gate)
        sel_scr[hh] = jnp.where(past, sel, 0.0)


def _run_round_robin(*gens):
    gens = list(gens)
    while gens:
        gens = [g for g in gens if next(g, StopIteration) is not StopIteration]


def _mixer_c_block(c, k_ref, vt_ref, bias_ref, o_ref, qm_scr, sel_scr, s_scr, m_scr, p_scr, *, ndb):
    blk = MOBA_BLOCK
    par = c % 2
    qcols = slice(c * blk, (c + 1) * blk)
    fold = lambda t: t.reshape(blk // SUBLANES, SUBLANES, blk)
    heads = range(HEAD_PAIR)
    pairs = [(k0, min(k0 + 2, c + 1)) for k0 in range(0, c + 1, 2)]
    picked = lambda hh, kb: sel_scr[hh, kb:kb + 1, qcols] > 0.0
    far_bias = lambda hh: bias_ref[hh, ndb, 0:1, :]

    def logits(hh):
        qm = qm_scr[hh, qcols, :]
        m8 = jnp.full((SUBLANES, blk), NEG_INF, F32)
        for k0, k1 in pairs:
            s = lax.dot_general(k_ref[0, k0 * blk:k1 * blk, :], qm, NT_DIMS, preferred_element_type=F32)
            for kb in range(k0, k1):
                su = s[(kb - k0) * blk:(kb - k0 + 1) * blk]
                if c - kb < ndb:
                    su = su + bias_ref[hh, c - kb]
                    t8 = jnp.max(fold(su), axis=0)
                else:
                    t8 = jnp.max(fold(su), axis=0) + far_bias(hh)
                s_scr[par, hh, kb * blk:(kb + 1) * blk, :] = su
                m8 = jnp.maximum(m8, t8 if kb == c else jnp.where(picked(hh, kb), t8, NEG_INF))
            yield
        m_scr[par, hh] = jnp.broadcast_to(jnp.max(m8, axis=0, keepdims=True), (SUBLANES, blk))

    outs = {}

    def attend(hh):
        m = m_scr[par, hh, 0:1, :]
        pv = jnp.zeros((V_ROWS, blk), F32)
        for k0, k1 in pairs:
            for kb in range(k0, k1):
                mk = m if kb == c else jnp.where(picked(hh, kb), m, -NEG_INF)
                if c - kb >= ndb:
                    mk = mk - far_bias(hh)
                x = s_scr[par, hh, kb * blk:(kb + 1) * blk, :] - mk
                p_scr[hh, kb * blk:(kb + 1) * blk, :] = jnp.exp2(x.astype(BF16))
            pv = pv + jnp.dot(vt_ref[0, hh * V_ROWS:(hh + 1) * V_ROWS, k0 * blk:k1 * blk],
                              p_scr[hh, k0 * blk:k1 * blk, :], preferred_element_type=F32)
            yield
        outs[hh] = pv[:HEAD_DIM] / pv[HEAD_DIM:HEAD_DIM + 1]
        if len(outs) == HEAD_PAIR:
            o_ref[0, qcols, :] = jnp.concatenate([outs[h] for h in heads], axis=0).T.astype(o_ref.dtype)

    return [logits(hh) for hh in heads], [attend(hh) for hh in heads]


def _mixer_c_kernel(q_ref, k_ref, vt_ref, bias_ref, o_ref, qm_scr, sel_scr, s_scr, m_scr, p_scr, *, nblk, ndb):
    _mixer_c_prologue(q_ref, k_ref, qm_scr, sel_scr, nblk=nblk)
    stages = functools.partial(_mixer_c_block, k_ref=k_ref, vt_ref=vt_ref, bias_ref=bias_ref, o_ref=o_ref,
                               qm_scr=qm_scr, sel_scr=sel_scr, s_scr=s_scr, m_scr=m_scr, p_scr=p_scr, ndb=ndb)

    def step(t):
        gens = []
        if t < nblk:
            gens += stages(t)[0]
        if t >= 1:
            gens += stages(t - 1)[1]
        _run_round_robin(*gens)

    def body(i, carry):
        for t in range(nblk + 1):
            pl.when(i == t)(functools.partial(step, t))
        return carry

    lax.fori_loop(0, nblk + 1, body, 0)


def _mixer_c(zc, vt, bias_c, batch, seq, ndb):
    blk = MOBA_BLOCK
    assert seq % blk == 0
    nblk = seq // blk
    npair = C_HEADS // HEAD_PAIR
    z = zc.reshape(batch, seq, 2 * C_WIDTH)
    o = pl.pallas_call(
        functools.partial(_mixer_c_kernel, nblk=nblk, ndb=ndb),
        grid=(batch, npair),
        in_specs=[pl.BlockSpec((1, seq, LANE), lambda b, hp: (b, 0, hp)),
                  pl.BlockSpec((1, seq, LANE), lambda b, hp: (b, 0, npair + hp)),
                  pl.BlockSpec((1, HEAD_PAIR * V_ROWS, seq), lambda b, hp: (b, hp, 0)),
                  pl.BlockSpec((HEAD_PAIR, ndb + 1, blk, blk), lambda b, hp: (hp, 0, 0, 0))],
        out_specs=pl.BlockSpec((1, seq, LANE), lambda b, hp: (b, 0, hp)),
        out_shape=jax.ShapeDtypeStruct((batch, seq, C_WIDTH), BF16),
        scratch_shapes=[pltpu.VMEM((HEAD_PAIR, seq, LANE), BF16), pltpu.VMEM((HEAD_PAIR, nblk, seq), F32),
                        pltpu.VMEM((2, HEAD_PAIR, seq, blk), F32), pltpu.VMEM((2, HEAD_PAIR, SUBLANES, blk), F32),
                        pltpu.VMEM((HEAD_PAIR, seq, blk), BF16)],
        compiler_params=_cparams(("arbitrary", "arbitrary")),
        name="mixer_c",
    )(z, z, vt, bias_c)
    return o.reshape(batch * seq, C_WIDTH)


def _merge_kernel(x_ref, o1_ref, o2_ref, o3_ref, l1_ref, l2_ref, l3_ref, ob_ref, oc_ref, zg_ref,
                  pa_ref, pb_ref, pc_ref, wo_ref, out_ref):
    d = D_MODEL
    halves = lambda ref: jnp.concatenate([ref[c] for c in range(ref.shape[0])], axis=-1)
    l1, l2, l3 = halves(l1_ref), halves(l2_ref), halves(l3_ref)
    m = jnp.maximum(jnp.maximum(l1, l2), l3)
    e1, e2, e3 = jnp.exp(l1 - m), jnp.exp(l2 - m), jnp.exp(l3 - m)
    o_a = (e1 * halves(o1_ref) + e2 * halves(o2_ref) + e3 * halves(o3_ref)) / (e1 + e2 + e3)

    def branch(o, p_ref, k):
        gate = jax.nn.sigmoid(zg_ref[:, k * d:(k + 1) * d].astype(F32))
        return gate * jnp.dot(o, p_ref[...], preferred_element_type=F32)

    merged = branch(o_a.astype(BF16), pa_ref, 0) + branch(ob_ref[...], pb_ref, 1) + branch(oc_ref[...], pc_ref, 2)
    out_ref[...] = x_ref[...] + jnp.dot(merged.astype(BF16), wo_ref[...], preferred_element_type=F32)


def _merge(x2d, oa, lse, ob, oc, zg, p_a, p_b, p_c, w_out, tm=512):
    n = x2d.shape[0]
    row = lambda width: pl.BlockSpec((tm, width), lambda i: (i, 0))
    return pl.pallas_call(
        _merge_kernel,
        grid=(n // tm,),
        in_specs=[row(D_MODEL)] + [pl.BlockSpec((A_OUT // LANE, tm, LANE), lambda i: (0, i, 0))] * 6
                 + [row(LRU_WIDTH), row(C_WIDTH), row(3 * D_MODEL),
                  _const_spec((A_OUT, D_MODEL)), _const_spec((LRU_WIDTH, D_MODEL)),
                  _const_spec((C_WIDTH, D_MODEL)), _const_spec((D_MODEL, D_MODEL))],
        out_specs=row(D_MODEL),
        out_shape=jax.ShapeDtypeStruct((n, D_MODEL), F32),
        compiler_params=_cparams(("arbitrary",)),
        name="merge",
    )(x2d, *oa, *lse, ob, oc, zg, p_a.astype(BF16), p_b.astype(BF16), p_c.astype(BF16), w_out.astype(BF16))


def _rms(x, g):
    return x * lax.rsqrt(jnp.mean(x * x, axis=-1, keepdims=True) + EPS) * g


def _ffn_kernel(x_ref, g_ref, wgu_ref, wd_ref, gf_ref, out_ref, h_scr, act_scr, *, final_norm):
    x = x_ref[...]
    h_scr[...] = _rms(x, g_ref[...]).astype(BF16)
    for off, w in _col_chunks(0, FFN_HIDDEN, 256):
        gate = jnp.dot(h_scr[...], wgu_ref[:, off:off + w], preferred_element_type=F32)
        up = jnp.dot(h_scr[...], wgu_ref[:, FFN_HIDDEN + off:FFN_HIDDEN + off + w], preferred_element_type=F32)
        act_scr[:, off:off + w] = (gate * jax.nn.sigmoid(gate) * up).astype(BF16)
    y = x + jnp.dot(act_scr[...], wd_ref[...], preferred_element_type=F32)
    if final_norm:
        y = _rms(y, gf_ref[...])
    out_ref[...] = y


def _ffn(x2d, g_ffn, w_gu, w_down, g_final, final_norm, tm=512):
    n = x2d.shape[0]
    row = pl.BlockSpec((tm, D_MODEL), lambda i: (i, 0))
    return pl.pallas_call(
        functools.partial(_ffn_kernel, final_norm=final_norm),
        grid=(n // tm,),
        in_specs=[row, _const_spec((1, D_MODEL)), _const_spec((D_MODEL, 2 * FFN_HIDDEN)),
                  _const_spec((FFN_HIDDEN, D_MODEL)), _const_spec((1, D_MODEL))],
        out_specs=row,
        out_shape=jax.ShapeDtypeStruct((n, D_MODEL), F32),
        scratch_shapes=[pltpu.VMEM((tm, D_MODEL), BF16), pltpu.VMEM((tm, FFN_HIDDEN), BF16)],
        compiler_params=_cparams(("arbitrary",)),
        name="ffn",
    )(x2d, g_ffn.reshape(1, D_MODEL), w_gu.astype(BF16), w_down.astype(BF16), g_final.reshape(1, D_MODEL))


def kernel(x, rel_bias, g_mix, w_in, conv_w, conv_b, lru_wa, lru_ba, lru_wx, lru_bx, lru_lam,
           p_a, p_b, p_c, w_out, g_ffn, w_gu, w_down, g_final):
    batch, seq, d = x.shape
    assert d == D_MODEL and w_in.shape[-1] == IN_COLS
    depth = w_in.shape[0]
    nblk = seq // MOBA_BLOCK
    ndb = _moba_far_blocks(nblk)

    bias_a = _build_bias(rel_bias, _bucket_index_a(), A_HEADS, 0, A_HPG)
    bias_c = _build_bias(rel_bias, _bucket_index_c(ndb), C_HEADS, A_HEADS, None, LOG2E)

    x2d = x.reshape(batch * seq, d)
    for l in range(depth):
        *za, zb, zc, vt, zg = _in_proj(x2d, g_mix[l], w_in[l], batch, seq)
        oa, lse = zip(*[_mixer_a_group(za[g], bias_a, g, dil, batch, seq)
                        for g, (_, dil) in enumerate(A_GROUPS)])
        ob = _mixer_b(zb, conv_w[l], conv_b[l], lru_wa[l], lru_ba[l], lru_wx[l], lru_bx[l], lru_lam[l],
                      batch, seq)
        oc = _mixer_c(zc, vt, bias_c, batch, seq, ndb)
        x2d = _merge(x2d, oa, lse, ob, oc, zg, p_a[l], p_b[l], p_c[l], w_out[l])
        x2d = _ffn(x2d, g_ffn[l], w_gu[l], w_down[l], g_final, final_norm=(l == depth - 1))
    return x2d.reshape(batch, seq, d)
```

```python
import functools
import math

import numpy as np
import jax
import jax.numpy as jnp
from jax import lax
from jax.experimental import pallas as pl
from jax.experimental.pallas import tpu as pltpu

F32 = jnp.float32
BF16 = jnp.bfloat16

D_MODEL = 1024
HEAD_DIM = 64
SCALE = HEAD_DIM ** -0.5
LOG2E = math.log2(math.e)
NEG_INF = -1e30
EPS = 1e-6
A_GROUPS = ((128, 1), (512, 4), (2048, 16))
A_HPG = 4
A_HEADS = A_HPG * len(A_GROUPS)
A_WIDTH = A_HEADS * HEAD_DIM
A_OUT = A_HPG * HEAD_DIM
A_BLOCK = 128
LRU_WIDTH = D_MODEL // 2
LRU_BLOCKS = 8
CONV_WIDTH = 4
LRU_C = 8.0
C_HEADS = 8
C_WIDTH = C_HEADS * HEAD_DIM
MOBA_BLOCK = 256
MOBA_TOPK = 3
REL_BUCKETS = 32
REL_MAX_DIST = 2048
FFN_HIDDEN = 2816
IN_COLS = 3 * A_WIDTH + 2 * LRU_WIDTH + 3 * C_WIDTH + 3 * D_MODEL
V_C_OFF = 3 * A_WIDTH + 2 * LRU_WIDTH + 2 * C_WIDTH

LANE = 128
V_ROWS = HEAD_DIM + 16
VMEM_LIMIT = 56 * 1024 * 1024

NT_DIMS = (((1,), (1,)), ((), ()))


def _cparams(sem, vmem=VMEM_LIMIT):
    return pltpu.CompilerParams(dimension_semantics=sem, vmem_limit_bytes=vmem)


def _const_spec(shape):
    nd = len(shape)
    return pl.BlockSpec(shape, lambda *_: (0,) * nd, pipeline_mode=pl.Buffered(1))


def _rel_bucket_np(dist):
    max_exact = REL_BUCKETS // 2
    d = np.maximum(dist, 0)
    df = np.maximum(d, 1).astype(np.float32)
    large = max_exact + (np.log(df / np.float32(max_exact)) / np.float32(math.log(REL_MAX_DIST / max_exact))
                         * np.float32(REL_BUCKETS - max_exact)).astype(np.int32)
    large = np.minimum(large, REL_BUCKETS - 1)
    return np.where(d < max_exact, d, large).astype(np.int32)


def _bucket_index_a():
    qi = np.arange(A_BLOCK)[None, :] + A_BLOCK
    kj = np.arange(2 * A_BLOCK)[:, None]
    delta = qi - kj
    mats = []
    for window, dil in A_GROUPS:
        band = (delta >= 0) & (delta <= window // dil)
        mats.append(np.where(band, _rel_bucket_np(delta * dil), -1))
    return np.stack(mats).astype(np.int32)


def _moba_far_blocks(nblk):
    for db in range(1, nblk + 1):
        lo = db * MOBA_BLOCK - (MOBA_BLOCK - 1)
        if np.all(_rel_bucket_np(np.arange(lo, nblk * MOBA_BLOCK)) == REL_BUCKETS - 1):
            return db
    return nblk


def _bucket_index_c(ndb):
    k = np.arange(MOBA_BLOCK)[:, None]
    q = np.arange(MOBA_BLOCK)[None, :]
    mats = []
    for db in range(ndb + 1):
        dist = db * MOBA_BLOCK + q - k
        b = _rel_bucket_np(dist)
        if db == 0:
            b = np.where(dist >= 0, b, -1)
        mats.append(b)
    return np.stack(mats).astype(np.int32)


def _bias_kernel(tab_ref, idx_ref, out_ref, *, head_off, unit):
    h = pl.program_id(0) + head_off
    idx = idx_ref[0]
    out = jnp.full(idx.shape, NEG_INF, F32)
    for b in range(REL_BUCKETS):
        out = jnp.where(idx == b, tab_ref[b, h] * unit, out)
    out_ref[0, 0] = out


def _build_bias(rel_bias, idx, n_heads, head_off, heads_per_idx_group, unit=1.0):
    n_mats = idx.shape[0] if heads_per_idx_group is None else 1
    r, c = idx.shape[-2:]
    if heads_per_idx_group is None:
        idx_map = lambda h, m: (m, 0, 0)
    else:
        idx_map = lambda h, m: (h // heads_per_idx_group, 0, 0)
    return pl.pallas_call(
        functools.partial(_bias_kernel, head_off=head_off, unit=unit),
        grid=(n_heads, n_mats),
        in_specs=[pl.BlockSpec(memory_space=pltpu.SMEM),
                  pl.BlockSpec((1, r, c), idx_map)],
        out_specs=pl.BlockSpec((1, 1, r, c), lambda h, m: (h, m, 0, 0)),
        out_shape=jax.ShapeDtypeStruct((n_heads, n_mats, r, c), F32),
        compiler_params=_cparams(("arbitrary", "arbitrary")),
        name="rel_bias_build",
    )(rel_bias, jnp.asarray(idx))


def _col_chunks(start, stop, width=512):
    out = []
    while start < stop:
        w = min(width, stop - start)
        out.append((start, w))
        start += w
    return out


def _in_proj_kernel(x_ref, g_ref, w_ref, wvt_ref, za0_ref, za1_ref, za2_ref, zb_ref, zc_ref, vt_ref, zg_ref,
                    h_scr, dil_scr, *, tm):
    x = x_ref[...]
    ms = jnp.mean(x * x, axis=-1, keepdims=True)
    h_scr[...] = (x * lax.rsqrt(ms + EPS) * g_ref[...]).astype(BF16)

    def proj(base, width):
        return jnp.dot(h_scr[...], w_ref[:, base:base + width], preferred_element_type=F32)

    def seg(out_ref, base, width, scaled_cols=0, scale=1.0, chunk=512):
        assert scaled_cols % chunk == 0
        for off, w in _col_chunks(0, width, chunk):
            r = proj(base + off, w)
            if off < scaled_cols:
                r = r * scale
            out_ref[:, off:off + w] = r.astype(out_ref.dtype)

    gw = 3 * A_OUT
    for g, (za_ref, (_, dil)) in enumerate(zip((za0_ref, za1_ref, za2_ref), A_GROUPS)):
        q = proj(g * gw, A_OUT) * SCALE
        kv = proj(g * gw + A_OUT, 2 * A_OUT)
        if dil == 1:
            za_ref[0, 0, :, :A_OUT] = q.astype(BF16)
            za_ref[0, 0, :, A_OUT:] = kv.astype(BF16)
        else:
            qkv = (q[:, :LANE], q[:, LANE:]) + tuple(kv[:, c * LANE:(c + 1) * LANE] for c in range(4))
            for c, part in enumerate(qkv):
                dil_scr[c] = part
            for r in range(dil):
                for c in range(len(qkv)):
                    za_ref[0, r, :, c * LANE:(c + 1) * LANE] = (
                        dil_scr[c, pl.ds(r, tm // dil, stride=dil), :].astype(BF16))

    wa, wb, wc = 3 * A_WIDTH, 2 * LRU_WIDTH, 2 * C_WIDTH
    seg(zb_ref, wa, wb)
    seg(zc_ref, wa + wb, wc, C_WIDTH, SCALE * LOG2E)
    seg(zg_ref, wa + wb + wc, 3 * D_MODEL)
    vt = lax.dot_general(wvt_ref[...], h_scr[...], NT_DIMS, preferred_element_type=F32).astype(BF16)
    for h in range(C_HEADS):
        vt_ref[0, h * V_ROWS:h * V_ROWS + HEAD_DIM, :] = vt[h * HEAD_DIM:(h + 1) * HEAD_DIM]
        vt_ref[0, h * V_ROWS + HEAD_DIM:(h + 1) * V_ROWS, :] = jnp.ones((V_ROWS - HEAD_DIM, tm), BF16)


def _in_proj(x2d, g, w_in, batch, seq, tm=512):
    n = x2d.shape[0]
    grp = lambda k: jnp.concatenate([w_in[:, s * A_WIDTH + k * A_OUT:s * A_WIDTH + (k + 1) * A_OUT]
                                     for s in range(3)], axis=1)
    w_main = jnp.concatenate([grp(k) for k in range(len(A_GROUPS))]
                             + [w_in[:, 3 * A_WIDTH:V_C_OFF], w_in[:, V_C_OFF + C_WIDTH:]], axis=1).astype(BF16)
    wvt = w_in[:, V_C_OFF:V_C_OFF + C_WIDTH].T.astype(BF16)
    wb, wc, wg = 2 * LRU_WIDTH, 2 * C_WIDTH, 3 * D_MODEL
    gw = 3 * A_OUT
    tiles_per_seq = seq // tm
    row = lambda i: (i, 0)
    seq_tile = lambda i: (i // tiles_per_seq, 0, i % tiles_per_seq, 0)
    za_specs = [pl.BlockSpec((1, dil, tm // dil, gw), seq_tile) for _, dil in A_GROUPS]
    za_shapes = [jax.ShapeDtypeStruct((batch, dil, seq // dil, gw), BF16) for _, dil in A_GROUPS]
    return pl.pallas_call(
        functools.partial(_in_proj_kernel, tm=tm),
        grid=(n // tm,),
        in_specs=[pl.BlockSpec((tm, D_MODEL), row),
                  _const_spec((1, D_MODEL)),
                  _const_spec(w_main.shape),
                  _const_spec(wvt.shape)],
        out_specs=za_specs + [pl.BlockSpec((tm, wb), row),
                              pl.BlockSpec((tm, wc), row),
                              pl.BlockSpec((1, C_HEADS * V_ROWS, tm),
                                           lambda i: (i // tiles_per_seq, 0, i % tiles_per_seq)),
                              pl.BlockSpec((tm, wg), row)],
        out_shape=za_shapes + [jax.ShapeDtypeStruct((n, wb), BF16),
                               jax.ShapeDtypeStruct((n, wc), BF16),
                               jax.ShapeDtypeStruct((batch, C_HEADS * V_ROWS, seq), BF16),
                               jax.ShapeDtypeStruct((n, wg), BF16)],
        scratch_shapes=[pltpu.VMEM((tm, D_MODEL), BF16), pltpu.VMEM((gw // LANE, tm, LANE), F32)],
        compiler_params=_cparams(("arbitrary",)),
        name="in_proj",
    )(x2d, g.reshape(1, D_MODEL), w_main, wvt)


A_TOKENS_PER_STEP = 2048
A_CHAINS = 4


def _mixer_a_kernel(z_ref, zp_ref, bias_ref, o_ref, lse_ref, *, dil, nsub):
    t = pl.program_id(1)
    blk, width = A_BLOCK, A_OUT
    kcols, vcols = slice(width, 2 * width), slice(2 * width, 3 * width)
    lane_head = lax.broadcasted_iota(jnp.int32, (blk, width), 1) // HEAD_DIM
    in_prev = (lax.broadcasted_iota(jnp.int32, (2 * blk, 2 * blk), 0) < blk).astype(F32)
    no_prev = in_prev * jnp.where(t == 0, NEG_INF, 0.0)
    transpose_v = lambda v: v.astype(F32).T.astype(BF16)

    ones_rows = jnp.ones((V_ROWS - HEAD_DIM, 2 * blk), BF16)
    blocks, windows = {}, {}

    def block_kv(r, j):
        if (r, j) not in blocks:
            src = zp_ref[0, r] if j < 0 else z_ref[0, r, j * blk:(j + 1) * blk, :]
            blocks[(r, j)] = (src[:, kcols], transpose_v(src[:, vcols]))
        return blocks[(r, j)]

    def window(r, j):
        if (r, j) not in windows:
            (k_prev, vt_prev), (k_cur, vt_cur) = block_kv(r, j - 1), block_kv(r, j)
            k_win = jnp.concatenate([k_prev, k_cur], axis=0)
            vt_win = jnp.concatenate([vt_prev, vt_cur], axis=1)
            windows[(r, j)] = (z_ref[0, r, j * blk:(j + 1) * blk, 0:width], k_win, vt_win)
        return windows[(r, j)]

    def logits(r, j, hp):
        q, k_win, _ = window(r, j)
        heads = (2 * hp, 2 * hp + 1)
        q2 = jnp.concatenate([jnp.where(lane_head == h, q, jnp.zeros_like(q)) for h in heads], axis=0)
        bias2 = jnp.concatenate([bias_ref[h, 0] for h in heads], axis=1)
        s = lax.dot_general(k_win, q2, NT_DIMS, preferred_element_type=F32) + bias2
        return s + no_prev if j == 0 else s

    def attend(r, j, hp, s):
        vt_win = window(r, j)[2]
        m = jnp.max(s, axis=0, keepdims=True)
        p = jnp.exp((s - m).astype(BF16))
        vt_ext = jnp.concatenate([part for h in (2 * hp, 2 * hp + 1)
                                  for part in (vt_win[h * HEAD_DIM:(h + 1) * HEAD_DIM, :], ones_rows)], axis=0)
        pv = jnp.dot(vt_ext, p, preferred_element_type=F32)
        res = []
        for u in range(2):
            cols = slice(u * blk, (u + 1) * blk)
            den = pv[u * V_ROWS + HEAD_DIM:u * V_ROWS + HEAD_DIM + 1, cols]
            res.append((pv[u * V_ROWS:u * V_ROWS + HEAD_DIM, cols] / den,
                        jnp.broadcast_to(m[:, cols] + jnp.log(den), (HEAD_DIM, blk))))
        return res

    n_hp = A_HPG // 2

    def chain(units):
        s_next = logits(*units[0])
        outs, lses = [], []
        for i, (r, j, hp) in enumerate(units):
            s_cur = s_next
            if i + 1 < len(units):
                s_next = logits(*units[i + 1])
            for o, lse in attend(r, j, hp, s_cur):
                outs.append(o)
                lses.append(lse)
            if hp == n_hp - 1:
                o_t = jnp.concatenate(outs, axis=0).T
                lse_t = jnp.concatenate(lses, axis=0).T
                outs, lses = [], []
                dst = slice(j * blk, (j + 1) * blk) if dil == 1 else pl.ds(j * blk * dil + r, blk, stride=dil)
                for c in range(width // LANE):
                    o_ref[c, dst, :] = o_t[:, c * LANE:(c + 1) * LANE]
                    lse_ref[c, dst, :] = lse_t[:, c * LANE:(c + 1) * LANE]
            yield

    units = [(r, j, hp) for r in range(dil) for j in range(nsub) for hp in range(n_hp)]
    per_chain = len(units) // A_CHAINS
    _run_round_robin(*[chain(units[i * per_chain:(i + 1) * per_chain]) for i in range(A_CHAINS)])


def _mixer_a_group(za, bias_a, g, dil, batch, seq):
    sub_len = seq // dil
    assert sub_len % A_BLOCK == 0
    tq = min(sub_len, A_TOKENS_PER_STEP // dil)
    assert tq % A_BLOCK == 0
    nsub = tq // A_BLOCK
    gw = 3 * A_OUT
    out_spec = pl.BlockSpec((A_OUT // LANE, tq * dil, LANE), lambda b, t: (0, b * (sub_len // tq) + t, 0))
    return pl.pallas_call(
        functools.partial(_mixer_a_kernel, dil=dil, nsub=nsub),
        grid=(batch, sub_len // tq),
        in_specs=[pl.BlockSpec((1, dil, tq, gw), lambda b, t: (b, 0, t, 0)),
                  pl.BlockSpec((1, dil, A_BLOCK, gw), lambda b, t: (b, 0, jnp.maximum(t * nsub - 1, 0), 0)),
                  pl.BlockSpec((A_HPG, 1, 2 * A_BLOCK, A_BLOCK), lambda b, t: (g, 0, 0, 0))],
        out_specs=[out_spec, out_spec],
        out_shape=[jax.ShapeDtypeStruct((A_OUT // LANE, batch * seq, LANE), F32)] * 2,
        compiler_params=_cparams(("arbitrary", "arbitrary")),
        name=f"mixer_a_d{dil}",
    )(za, za, bias_a)


SUBLANES = 8


def _mixer_b_kernel(z_ref, cw_ref, cb_ref, wax_ref, bax_ref, lam_ref, y_ref,
                    xs_scr, a_scr, b_scr, h_scr, *, ts):
    w = LRU_WIDTH

    @pl.when(pl.program_id(1) == 0)
    def _():
        xs_scr[0:SUBLANES, :] = jnp.zeros((SUBLANES, w), F32)
        h_scr[...] = jnp.zeros((1, w), F32)

    x = z_ref[0, :, 0:w].astype(F32)
    xs_scr[SUBLANES:SUBLANES + ts, :] = x
    xc = cb_ref[...]
    for i in range(CONV_WIDTH):
        off = SUBLANES - (CONV_WIDTH - 1) + i
        xc = xc + xs_scr[off:off + ts, :] * cw_ref[i:i + 1, :]
    xs_scr[0:SUBLANES, :] = xs_scr[ts:ts + SUBLANES, :]

    ra = jnp.dot(xc.astype(BF16), wax_ref[...], preferred_element_type=F32) + bax_ref[...]
    r = jax.nn.sigmoid(ra[:, :w])
    ig = jax.nn.sigmoid(ra[:, w:])
    nl = -lam_ref[...]
    softplus = jnp.maximum(nl, 0.0) + jnp.log1p(jnp.exp(-jnp.abs(nl)))
    log_a = -LRU_C * r * softplus
    a = jnp.exp(log_a)
    a_scr[...] = a
    b_scr[...] = jnp.sqrt(-jnp.tanh(log_a) * (a * a + 1.0)) * (ig * xc)

    row = lax.broadcasted_iota(jnp.int32, (SUBLANES, w), 0)

    def body(c, h):
        i = pl.multiple_of(c * SUBLANES, SUBLANES)
        a = a_scr[pl.ds(i, SUBLANES), :]
        b = b_scr[pl.ds(i, SUBLANES), :]
        for s in (1, 2, 4):
            keep = row >= s
            a_sh = jnp.where(keep, pltpu.roll(a, s, 0), 1.0)
            b_sh = jnp.where(keep, pltpu.roll(b, s, 0), 0.0)
            b = a * b_sh + b
            a = a * a_sh
        hs = a * h + b
        b_scr[pl.ds(i, SUBLANES), :] = hs
        return hs[SUBLANES - 1:SUBLANES, :]

    h_scr[...] = lax.fori_loop(0, ts // SUBLANES, body, h_scr[...])

    gb = z_ref[0, :, w:2 * w].astype(F32)
    cdf = 0.5 * (1.0 + jnp.tanh(math.sqrt(2.0 / math.pi) * (gb + 0.044715 * (gb * gb * gb))))
    y_ref[0] = (b_scr[...] * (gb * cdf)).astype(y_ref.dtype)


def _block_diag(wblocks):
    nb, di, do = wblocks.shape
    eye = jnp.eye(nb, dtype=wblocks.dtype)
    return (eye[:, None, :, None] * wblocks[:, :, None, :]).reshape(nb * di, nb * do)


def _mixer_b(zb, conv_w, conv_b, lru_wa, lru_ba, lru_wx, lru_bx, lru_lam, batch, seq, ts=512):
    w = LRU_WIDTH
    wax = jnp.concatenate([_block_diag(lru_wa), _block_diag(lru_wx)], axis=1).astype(BF16)
    bax = jnp.concatenate([lru_ba, lru_bx]).reshape(1, 2 * w)
    y = pl.pallas_call(
        functools.partial(_mixer_b_kernel, ts=ts),
        grid=(batch, seq // ts),
        in_specs=[pl.BlockSpec((1, ts, 2 * w), lambda b, t: (b, t, 0)),
                  _const_spec((CONV_WIDTH, w)), _const_spec((1, w)),
                  _const_spec((w, 2 * w)), _const_spec((1, 2 * w)), _const_spec((1, w))],
        out_specs=pl.BlockSpec((1, ts, w), lambda b, t: (b, t, 0)),
        out_shape=jax.ShapeDtypeStruct((batch, seq, w), BF16),
        scratch_shapes=[pltpu.VMEM((ts + SUBLANES, w), F32), pltpu.VMEM((ts, w), F32),
                        pltpu.VMEM((ts, w), F32), pltpu.VMEM((1, w), F32)],
        compiler_params=_cparams(("arbitrary", "arbitrary")),
        name="mixer_b",
    )(zb.reshape(batch, seq, 2 * w), conv_w, conv_b.reshape(1, w), wax, bax, lru_lam.reshape(1, w))
    return y.reshape(batch * seq, w)


HEAD_PAIR = LANE // HEAD_DIM


def _mixer_c_prologue(q_ref, k_ref, qm_scr, sel_scr, *, nblk):
    blk = MOBA_BLOCK
    seq = nblk * blk
    kmean = jnp.concatenate(
        [jnp.sum(k_ref[0, n * blk:(n + 1) * blk, :].astype(F32), axis=0, keepdims=True) for n in range(nblk)],
        axis=0) * (1.0 / blk)
    km_hi = kmean.astype(BF16)
    km_lo = (kmean - km_hi.astype(F32)).astype(BF16)
    q = q_ref[0]
    lane = lax.broadcasted_iota(jnp.int32, q.shape, 1)
    blk_id = lax.broadcasted_iota(jnp.int32, (nblk, seq), 0)
    q_blk = lax.broadcasted_iota(jnp.int32, (nblk, seq), 1) // blk
    past = blk_id < q_blk
    for hh in range(HEAD_PAIR):
        qm = jnp.where((lane >= hh * HEAD_DIM) & (lane < (hh + 1) * HEAD_DIM), q, jnp.zeros_like(q))
        qm_scr[hh] = qm
        gate = (lax.dot_general(km_hi, qm, NT_DIMS, preferred_element_type=F32)
                + lax.dot_general(km_lo, qm, NT_DIMS, preferred_element_type=F32))
        gate = jnp.where(past, gate, NEG_INF)
        sel = jnp.zeros((nblk, seq), F32)
        for _ in range(min(MOBA_TOPK, nblk - 1)):
            top = jnp.max(gate, axis=0, keepdims=True)
            first = jnp.min(jnp.where(gate == top, blk_id, nblk), axis=0, keepdims=True)
            pick = blk_id == first
            sel = jnp.where(pick, 1.0, sel)
            gate = jnp.where(pick, -jnp.inf, gate)
        sel_scr[hh] = jnp.where(past, sel, 0.0)


def _run_round_robin(*gens):
    gens = list(gens)
    while gens:
        gens = [g for g in gens if next(g, StopIteration) is not StopIteration]


def _mixer_c_block(c, k_ref, vt_ref, bias_ref, o_ref, qm_scr, sel_scr, s_scr, m_scr, p_scr, *, ndb):
    blk = MOBA_BLOCK
    par = c % 2
    qcols = slice(c * blk, (c + 1) * blk)
    fold = lambda t: t.reshape(blk // SUBLANES, SUBLANES, blk)
    heads = range(HEAD_PAIR)
    pairs = [(k0, min(k0 + 2, c + 1)) for k0 in range(0, c + 1, 2)]
    picked = lambda hh, kb: sel_scr[hh, kb:kb + 1, qcols] > 0.0
    far_bias = lambda hh: bias_ref[hh, ndb, 0:1, :]

    def logits(hh):
        qm = qm_scr[hh, qcols, :]
        m8 = jnp.full((SUBLANES, blk), NEG_INF, F32)
        for k0, k1 in pairs:
            s = lax.dot_general(k_ref[0, k0 * blk:k1 * blk, :], qm, NT_DIMS, preferred_element_type=F32)
            for kb in range(k0, k1):
                su = s[(kb - k0) * blk:(kb - k0 + 1) * blk]
                if c - kb < ndb:
                    su = su + bias_ref[hh, c - kb]
                    t8 = jnp.max(fold(su), axis=0)
                else:
                    t8 = jnp.max(fold(su), axis=0) + far_bias(hh)
                s_scr[par, hh, kb * blk:(kb + 1) * blk, :] = su
                m8 = jnp.maximum(m8, t8 if kb == c else jnp.where(picked(hh, kb), t8, NEG_INF))
            yield
        m_scr[par, hh] = jnp.broadcast_to(jnp.max(m8, axis=0, keepdims=True), (SUBLANES, blk))

    outs = {}

    def attend(hh):
        m = m_scr[par, hh, 0:1, :]
        pv = jnp.zeros((V_ROWS, blk), F32)
        for k0, k1 in pairs:
            for kb in range(k0, k1):
                mk = m if kb == c else jnp.where(picked(hh, kb), m, -NEG_INF)
                if c - kb >= ndb:
                    mk = mk - far_bias(hh)
                x = s_scr[par, hh, kb * blk:(kb + 1) * blk, :] - mk
                p_scr[hh, kb * blk:(kb + 1) * blk, :] = jnp.exp2(x.astype(BF16))
            pv = pv + jnp.dot(vt_ref[0, hh * V_ROWS:(hh + 1) * V_ROWS, k0 * blk:k1 * blk],
                              p_scr[hh, k0 * blk:k1 * blk, :], preferred_element_type=F32)
            yield
        outs[hh] = pv[:HEAD_DIM] / pv[HEAD_DIM:HEAD_DIM + 1]
        if len(outs) == HEAD_PAIR:
            o_ref[0, qcols, :] = jnp.concatenate([outs[h] for h in heads], axis=0).T.astype(o_ref.dtype)

    return [logits(hh) for hh in heads], [attend(hh) for hh in heads]


def _mixer_c_kernel(q_ref, k_ref, vt_ref, bias_ref, o_ref, qm_scr, sel_scr, s_scr, m_scr, p_scr, *, nblk, ndb):
    _mixer_c_prologue(q_ref, k_ref, qm_scr, sel_scr, nblk=nblk)
    stages = functools.partial(_mixer_c_block, k_ref=k_ref, vt_ref=vt_ref, bias_ref=bias_ref, o_ref=o_ref,
                               qm_scr=qm_scr, sel_scr=sel_scr, s_scr=s_scr, m_scr=m_scr, p_scr=p_scr, ndb=ndb)

    def step(t):
        gens = []
        if t < nblk:
            gens += stages(t)[0]
        if t >= 1:
            gens += stages(t - 1)[1]
        _run_round_robin(*gens)

    def body(i, carry):
        for t in range(nblk + 1):
            pl.when(i == t)(functools.partial(step, t))
        return carry

    lax.fori_loop(0, nblk + 1, body, 0)


def _mixer_c(zc, vt, bias_c, batch, seq, ndb):
    blk = MOBA_BLOCK
    assert seq % blk == 0
    nblk = seq // blk
    npair = C_HEADS // HEAD_PAIR
    z = zc.reshape(batch, seq, 2 * C_WIDTH)
    o = pl.pallas_call(
        functools.partial(_mixer_c_kernel, nblk=nblk, ndb=ndb),
        grid=(batch, npair),
        in_specs=[pl.BlockSpec((1, seq, LANE), lambda b, hp: (b, 0, hp)),
                  pl.BlockSpec((1, seq, LANE), lambda b, hp: (b, 0, npair + hp)),
                  pl.BlockSpec((1, HEAD_PAIR * V_ROWS, seq), lambda b, hp: (b, hp, 0)),
                  pl.BlockSpec((HEAD_PAIR, ndb + 1, blk, blk), lambda b, hp: (hp, 0, 0, 0))],
        out_specs=pl.BlockSpec((1, seq, LANE), lambda b, hp: (b, 0, hp)),
        out_shape=jax.ShapeDtypeStruct((batch, seq, C_WIDTH), BF16),
        scratch_shapes=[pltpu.VMEM((HEAD_PAIR, seq, LANE), BF16), pltpu.VMEM((HEAD_PAIR, nblk, seq), F32),
                        pltpu.VMEM((2, HEAD_PAIR, seq, blk), F32), pltpu.VMEM((2, HEAD_PAIR, SUBLANES, blk), F32),
                        pltpu.VMEM((HEAD_PAIR, seq, blk), BF16)],
        compiler_params=_cparams(("arbitrary", "arbitrary")),
        name="mixer_c",
    )(z, z, vt, bias_c)
    return o.reshape(batch * seq, C_WIDTH)


def _merge_kernel(x_ref, o1_ref, o2_ref, o3_ref, l1_ref, l2_ref, l3_ref, ob_ref, oc_ref, zg_ref,
                  pa_ref, pb_ref, pc_ref, wo_ref, out_ref):
    d = D_MODEL
    halves = lambda ref: jnp.concatenate([ref[c] for c in range(ref.shape[0])], axis=-1)
    l1, l2, l3 = halves(l1_ref), halves(l2_ref), halves(l3_ref)
    m = jnp.maximum(jnp.maximum(l1, l2), l3)
    e1, e2, e3 = jnp.exp(l1 - m), jnp.exp(l2 - m), jnp.exp(l3 - m)
    o_a = (e1 * halves(o1_ref) + e2 * halves(o2_ref) + e3 * halves(o3_ref)) / (e1 + e2 + e3)

    def branch(o, p_ref, k):
        gate = jax.nn.sigmoid(zg_ref[:, k * d:(k + 1) * d].astype(F32))
        return gate * jnp.dot(o, p_ref[...], preferred_element_type=F32)

    merged = branch(o_a.astype(BF16), pa_ref, 0) + branch(ob_ref[...], pb_ref, 1) + branch(oc_ref[...], pc_ref, 2)
    out_ref[...] = x_ref[...] + jnp.dot(merged.astype(BF16), wo_ref[...], preferred_element_type=F32)


def _merge(x2d, oa, lse, ob, oc, zg, p_a, p_b, p_c, w_out, tm=512):
    n = x2d.shape[0]
    row = lambda width: pl.BlockSpec((tm, width), lambda i: (i, 0))
    return pl.pallas_call(
        _merge_kernel,
        grid=(n // tm,),
        in_specs=[row(D_MODEL)] + [pl.BlockSpec((A_OUT // LANE, tm, LANE), lambda i: (0, i, 0))] * 6
                 + [row(LRU_WIDTH), row(C_WIDTH), row(3 * D_MODEL),
                  _const_spec((A_OUT, D_MODEL)), _const_spec((LRU_WIDTH, D_MODEL)),
                  _const_spec((C_WIDTH, D_MODEL)), _const_spec((D_MODEL, D_MODEL))],
        out_specs=row(D_MODEL),
        out_shape=jax.ShapeDtypeStruct((n, D_MODEL), F32),
        compiler_params=_cparams(("arbitrary",)),
        name="merge",
    )(x2d, *oa, *lse, ob, oc, zg, p_a.astype(BF16), p_b.astype(BF16), p_c.astype(BF16), w_out.astype(BF16))


def _rms(x, g):
    return x * lax.rsqrt(jnp.mean(x * x, axis=-1, keepdims=True) + EPS) * g


def _ffn_kernel(x_ref, g_ref, wgu_ref, wd_ref, gf_ref, out_ref, h_scr, act_scr, *, final_norm):
    x = x_ref[...]
    h_scr[...] = _rms(x, g_ref[...]).astype(BF16)
    for off, w in _col_chunks(0, FFN_HIDDEN, 256):
        gate = jnp.dot(h_scr[...], wgu_ref[:, off:off + w], preferred_element_type=F32)
        up = jnp.dot(h_scr[...], wgu_ref[:, FFN_HIDDEN + off:FFN_HIDDEN + off + w], preferred_element_type=F32)
        act_scr[:, off:off + w] = (gate * jax.nn.sigmoid(gate) * up).astype(BF16)
    y = x + jnp.dot(act_scr[...], wd_ref[...], preferred_element_type=F32)
    if final_norm:
        y = _rms(y, gf_ref[...])
    out_ref[...] = y


def _ffn(x2d, g_ffn, w_gu, w_down, g_final, final_norm, tm=512):
    n = x2d.shape[0]
    row = pl.BlockSpec((tm, D_MODEL), lambda i: (i, 0))
    return pl.pallas_call(
        functools.partial(_ffn_kernel, final_norm=final_norm),
        grid=(n // tm,),
        in_specs=[row, _const_spec((1, D_MODEL)), _const_spec((D_MODEL, 2 * FFN_HIDDEN)),
                  _const_spec((FFN_HIDDEN, D_MODEL)), _const_spec((1, D_MODEL))],
        out_specs=row,
        out_shape=jax.ShapeDtypeStruct((n, D_MODEL), F32),
        scratch_shapes=[pltpu.VMEM((tm, D_MODEL), BF16), pltpu.VMEM((tm, FFN_HIDDEN), BF16)],
        compiler_params=_cparams(("arbitrary",)),
        name="ffn",
    )(x2d, g_ffn.reshape(1, D_MODEL), w_gu.astype(BF16), w_down.astype(BF16), g_final.reshape(1, D_MODEL))


def kernel(x, rel_bias, g_mix, w_in, conv_w, conv_b, lru_wa, lru_ba, lru_wx, lru_bx, lru_lam,
           p_a, p_b, p_c, w_out, g_ffn, w_gu, w_down, g_final):
    batch, seq, d = x.shape
    assert d == D_MODEL and w_in.shape[-1] == IN_COLS
    depth = w_in.shape[0]
    nblk = seq // MOBA_BLOCK
    ndb = _moba_far_blocks(nblk)

    bias_a = _build_bias(rel_bias, _bucket_index_a(), A_HEADS, 0, A_HPG)
    bias_c = _build_bias(rel_bias, _bucket_index_c(ndb), C_HEADS, A_HEADS, None, LOG2E)

    x2d = x.reshape(batch * seq, d)
    for l in range(depth):
        *za, zb, zc, vt, zg = _in_proj(x2d, g_mix[l], w_in[l], batch, seq)
        oa, lse = zip(*[_mixer_a_group(za[g], bias_a, g, dil, batch, seq)
                        for g, (_, dil) in enumerate(A_GROUPS)])
        ob = _mixer_b(zb, conv_w[l], conv_b[l], lru_wa[l], lru_ba[l], lru_wx[l], lru_bx[l], lru_lam[l],
                      batch, seq)
        oc = _mixer_c(zc, vt, bias_c, batch, seq, ndb)
        x2d = _merge(x2d, oa, lse, ob, oc, zg, p_a[l], p_b[l], p_c[l], w_out[l])
        x2d = _ffn(x2d, g_ffn[l], w_gu[l], w_down[l], g_final, final_norm=(l == depth - 1))
    return x2d.reshape(batch, seq, d)
```

```python
import functools
import math

import numpy as np
import jax
import jax.numpy as jnp
from jax import lax
from jax.experimental import pallas as pl
from jax.experimental.pallas import tpu as pltpu

F32 = jnp.float32
BF16 = jnp.bfloat16

D_MODEL = 1024
HEAD_DIM = 64
SCALE = HEAD_DIM ** -0.5
LOG2E = math.log2(math.e)
NEG_INF = -1e30
EPS = 1e-6
A_GROUPS = ((128, 1), (512, 4), (2048, 16))
A_HPG = 4
A_HEADS = A_HPG * len(A_GROUPS)
A_WIDTH = A_HEADS * HEAD_DIM
A_OUT = A_HPG * HEAD_DIM
A_BLOCK = 128
LRU_WIDTH = D_MODEL // 2
LRU_BLOCKS = 8
CONV_WIDTH = 4
LRU_C = 8.0
C_HEADS = 8
C_WIDTH = C_HEADS * HEAD_DIM
MOBA_BLOCK = 256
MOBA_TOPK = 3
REL_BUCKETS = 32
REL_MAX_DIST = 2048
FFN_HIDDEN = 2816
IN_COLS = 3 * A_WIDTH + 2 * LRU_WIDTH + 3 * C_WIDTH + 3 * D_MODEL
V_C_OFF = 3 * A_WIDTH + 2 * LRU_WIDTH + 2 * C_WIDTH

LANE = 128
V_ROWS = HEAD_DIM + 16
VMEM_LIMIT = 56 * 1024 * 1024

NT_DIMS = (((1,), (1,)), ((), ()))


def _cparams(sem, vmem=VMEM_LIMIT):
    return pltpu.CompilerParams(dimension_semantics=sem, vmem_limit_bytes=vmem)


def _const_spec(shape):
    nd = len(shape)
    return pl.BlockSpec(shape, lambda *_: (0,) * nd, pipeline_mode=pl.Buffered(1))


def _rel_bucket_np(dist):
    max_exact = REL_BUCKETS // 2
    d = np.maximum(dist, 0)
    df = np.maximum(d, 1).astype(np.float32)
    large = max_exact + (np.log(df / np.float32(max_exact)) / np.float32(math.log(REL_MAX_DIST / max_exact))
                         * np.float32(REL_BUCKETS - max_exact)).astype(np.int32)
    large = np.minimum(large, REL_BUCKETS - 1)
    return np.where(d < max_exact, d, large).astype(np.int32)


def _bucket_index_a():
    qi = np.arange(A_BLOCK)[None, :] + A_BLOCK
    kj = np.arange(2 * A_BLOCK)[:, None]
    delta = qi - kj
    mats = []
    for window, dil in A_GROUPS:
        band = (delta >= 0) & (delta <= window // dil)
        mats.append(np.where(band, _rel_bucket_np(delta * dil), -1))
    return np.stack(mats).astype(np.int32)


def _moba_far_blocks(nblk):
    for db in range(1, nblk + 1):
        lo = db * MOBA_BLOCK - (MOBA_BLOCK - 1)
        if np.all(_rel_bucket_np(np.arange(lo, nblk * MOBA_BLOCK)) == REL_BUCKETS - 1):
            return db
    return nblk


def _bucket_index_c(ndb):
    k = np.arange(MOBA_BLOCK)[:, None]
    q = np.arange(MOBA_BLOCK)[None, :]
    mats = []
    for db in range(ndb + 1):
        dist = db * MOBA_BLOCK + q - k
        b = _rel_bucket_np(dist)
        if db == 0:
            b = np.where(dist >= 0, b, -1)
        mats.append(b)
    return np.stack(mats).astype(np.int32)


def _bias_kernel(tab_ref, idx_ref, out_ref, *, head_off, unit, present):
    h = pl.program_id(0) + head_off
    for m, buckets in enumerate(present):
        idx = idx_ref[m]
        out = jnp.full(idx.shape, NEG_INF, F32)
        for b in buckets:
            out = jnp.where(idx == b, tab_ref[b, h] * unit, out)
        out_ref[0, m] = out


def _build_bias(rel_bias, idx, n_heads, head_off, heads_per_idx_group, unit=1.0):
    r, c = idx.shape[-2:]
    if heads_per_idx_group is None:
        n_mats = idx.shape[0]
        idx_map = lambda h: (0, 0, 0)
        present = tuple(tuple(int(b) for b in np.unique(m[m >= 0])) for m in idx)
    else:
        n_mats = 1
        idx_map = lambda h: (h // heads_per_idx_group, 0, 0)
        present = (tuple(int(b) for b in np.unique(idx[idx >= 0])),)
    return pl.pallas_call(
        functools.partial(_bias_kernel, head_off=head_off, unit=unit, present=present),
        grid=(n_heads,),
        in_specs=[pl.BlockSpec(memory_space=pltpu.SMEM),
                  pl.BlockSpec((n_mats, r, c), idx_map)],
        out_specs=pl.BlockSpec((1, n_mats, r, c), lambda h: (h, 0, 0, 0)),
        out_shape=jax.ShapeDtypeStruct((n_heads, n_mats, r, c), F32),
        compiler_params=_cparams(("arbitrary",)),
        name="rel_bias_build",
    )(rel_bias, jnp.asarray(idx))


def _col_chunks(start, stop, width=512):
    out = []
    while start < stop:
        w = min(width, stop - start)
        out.append((start, w))
        start += w
    return out


def _in_proj_kernel(x_ref, g_ref, w_ref, wvt_ref, za0_ref, za1_ref, za2_ref, zb_ref, zc_ref, vt_ref, zg_ref,
                    h_scr, dil_scr, *, tm):
    x = x_ref[...]
    ms = jnp.mean(x * x, axis=-1, keepdims=True)
    h_scr[...] = (x * lax.rsqrt(ms + EPS) * g_ref[...]).astype(BF16)

    def proj(base, width):
        return jnp.dot(h_scr[...], w_ref[:, base:base + width], preferred_element_type=F32)

    def seg(out_ref, base, width, scaled_cols=0, scale=1.0, chunk=512):
        assert scaled_cols % chunk == 0
        for off, w in _col_chunks(0, width, chunk):
            r = proj(base + off, w)
            if off < scaled_cols:
                r = r * scale
            out_ref[:, off:off + w] = r.astype(out_ref.dtype)

    for g, (za_ref, (_, dil)) in enumerate(zip((za0_ref, za1_ref, za2_ref), A_GROUPS)):
        qkv = [proj(s * A_WIDTH + g * A_OUT, A_OUT) for s in range(3)]
        qkv[0] = qkv[0] * SCALE
        if dil == 1:
            for s, part in enumerate(qkv):
                za_ref[0, 0, :, s * A_OUT:(s + 1) * A_OUT] = part.astype(BF16)
        else:
            planes = [part[:, c * LANE:(c + 1) * LANE] for part in qkv for c in range(A_OUT // LANE)]
            for c, plane in enumerate(planes):
                dil_scr[c] = plane
            for r in range(dil):
                for c in range(len(planes)):
                    za_ref[0, r, :, c * LANE:(c + 1) * LANE] = (
                        dil_scr[c, pl.ds(r, tm // dil, stride=dil), :].astype(BF16))

    wa, wb, wc = 3 * A_WIDTH, 2 * LRU_WIDTH, 2 * C_WIDTH
    seg(zb_ref, wa, wb)
    seg(zc_ref, wa + wb, wc, C_WIDTH, SCALE * LOG2E)
    seg(zg_ref, V_C_OFF + C_WIDTH, 3 * D_MODEL)
    vt = lax.dot_general(wvt_ref[...], h_scr[...], NT_DIMS, preferred_element_type=F32).astype(BF16)
    for h in range(C_HEADS):
        vt_ref[0, h * V_ROWS:h * V_ROWS + HEAD_DIM, :] = vt[h * HEAD_DIM:(h + 1) * HEAD_DIM]
        vt_ref[0, h * V_ROWS + HEAD_DIM:(h + 1) * V_ROWS, :] = jnp.ones((V_ROWS - HEAD_DIM, tm), BF16)


def _in_proj(x2d, g, w_in, batch, seq, tm=512):
    n = x2d.shape[0]
    w_main = w_in.astype(BF16)
    wvt = w_in[:, V_C_OFF:V_C_OFF + C_WIDTH].T.astype(BF16)
    wb, wc, wg = 2 * LRU_WIDTH, 2 * C_WIDTH, 3 * D_MODEL
    gw = 3 * A_OUT
    tiles_per_seq = seq // tm
    row = lambda i: (i, 0)
    seq_tile = lambda i: (i // tiles_per_seq, 0, i % tiles_per_seq, 0)
    za_specs = [pl.BlockSpec((1, dil, tm // dil, gw), seq_tile) for _, dil in A_GROUPS]
    za_shapes = [jax.ShapeDtypeStruct((batch, dil, seq // dil, gw), BF16) for _, dil in A_GROUPS]
    return pl.pallas_call(
        functools.partial(_in_proj_kernel, tm=tm),
        grid=(n // tm,),
        in_specs=[pl.BlockSpec((tm, D_MODEL), row),
                  _const_spec((1, D_MODEL)),
                  _const_spec(w_main.shape),
                  _const_spec(wvt.shape)],
        out_specs=za_specs + [pl.BlockSpec((tm, wb), row),
                              pl.BlockSpec((tm, wc), row),
                              pl.BlockSpec((1, C_HEADS * V_ROWS, tm),
                                           lambda i: (i // tiles_per_seq, 0, i % tiles_per_seq)),
                              pl.BlockSpec((tm, wg), row)],
        out_shape=za_shapes + [jax.ShapeDtypeStruct((n, wb), BF16),
                               jax.ShapeDtypeStruct((n, wc), BF16),
                               jax.ShapeDtypeStruct((batch, C_HEADS * V_ROWS, seq), BF16),
                               jax.ShapeDtypeStruct((n, wg), BF16)],
        scratch_shapes=[pltpu.VMEM((tm, D_MODEL), BF16), pltpu.VMEM((gw // LANE, tm, LANE), F32)],
        compiler_params=_cparams(("arbitrary",)),
        name="in_proj",
    )(x2d, g.reshape(1, D_MODEL), w_main, wvt)


A_TOKENS_PER_STEP = 2048
A_CHAINS = 4


def _mixer_a_kernel(z_ref, zp_ref, bias_ref, o_ref, lse_ref, *, dil, nsub):
    t = pl.program_id(1)
    blk, width = A_BLOCK, A_OUT
    kcols, vcols = slice(width, 2 * width), slice(2 * width, 3 * width)
    lane_head = lax.broadcasted_iota(jnp.int32, (blk, width), 1) // HEAD_DIM
    in_prev = (lax.broadcasted_iota(jnp.int32, (2 * blk, 2 * blk), 0) < blk).astype(F32)
    no_prev = in_prev * jnp.where(t == 0, NEG_INF, 0.0)
    transpose_v = lambda v: v.astype(F32).T.astype(BF16)

    ones_rows = jnp.ones((V_ROWS - HEAD_DIM, 2 * blk), BF16)
    blocks, windows = {}, {}

    def block_kv(r, j):
        if (r, j) not in blocks:
            src = zp_ref[0, r] if j < 0 else z_ref[0, r, j * blk:(j + 1) * blk, :]
            blocks[(r, j)] = (src[:, kcols], transpose_v(src[:, vcols]))
        return blocks[(r, j)]

    def window(r, j):
        if (r, j) not in windows:
            (k_prev, vt_prev), (k_cur, vt_cur) = block_kv(r, j - 1), block_kv(r, j)
            k_win = jnp.concatenate([k_prev, k_cur], axis=0)
            vt_win = jnp.concatenate([vt_prev, vt_cur], axis=1)
            windows[(r, j)] = (z_ref[0, r, j * blk:(j + 1) * blk, 0:width], k_win, vt_win)
        return windows[(r, j)]

    def logits(r, j, hp):
        q, k_win, _ = window(r, j)
        heads = (2 * hp, 2 * hp + 1)
        q2 = jnp.concatenate([jnp.where(lane_head == h, q, jnp.zeros_like(q)) for h in heads], axis=0)
        bias2 = jnp.concatenate([bias_ref[h, 0] for h in heads], axis=1)
        s = lax.dot_general(k_win, q2, NT_DIMS, preferred_element_type=F32) + bias2
        return s + no_prev if j == 0 else s

    def attend(r, j, hp, s):
        vt_win = window(r, j)[2]
        m = jnp.max(s, axis=0, keepdims=True)
        p = jnp.exp((s - m).astype(BF16))
        vt_ext = jnp.concatenate([part for h in (2 * hp, 2 * hp + 1)
                                  for part in (vt_win[h * HEAD_DIM:(h + 1) * HEAD_DIM, :], ones_rows)], axis=0)
        pv = jnp.dot(vt_ext, p, preferred_element_type=F32)
        res = []
        for u in range(2):
            cols = slice(u * blk, (u + 1) * blk)
            den = pv[u * V_ROWS + HEAD_DIM:u * V_ROWS + HEAD_DIM + 1, cols]
            res.append((pv[u * V_ROWS:u * V_ROWS + HEAD_DIM, cols] / den,
                        jnp.broadcast_to(m[:, cols] + jnp.log(den), (HEAD_DIM, blk))))
        return res

    n_hp = A_HPG // 2

    def chain(units):
        s_next = logits(*units[0])
        outs, lses = [], []
        for i, (r, j, hp) in enumerate(units):
            s_cur = s_next
            if i + 1 < len(units):
                s_next = logits(*units[i + 1])
            for o, lse in attend(r, j, hp, s_cur):
                outs.append(o)
                lses.append(lse)
            if hp == n_hp - 1:
                o_t = jnp.concatenate(outs, axis=0).T
                lse_t = jnp.concatenate(lses, axis=0).T
                outs, lses = [], []
                dst = slice(j * blk, (j + 1) * blk) if dil == 1 else pl.ds(j * blk * dil + r, blk, stride=dil)
                for c in range(width // LANE):
                    o_ref[c, dst, :] = o_t[:, c * LANE:(c + 1) * LANE]
                    lse_ref[c, dst, :] = lse_t[:, c * LANE:(c + 1) * LANE]
            yield

    units = [(r, j, hp) for r in range(dil) for j in range(nsub) for hp in range(n_hp)]
    per_chain = len(units) // A_CHAINS
    _run_round_robin(*[chain(units[i * per_chain:(i + 1) * per_chain]) for i in range(A_CHAINS)])


def _mixer_a_group(za, bias_a, g, dil, batch, seq):
    sub_len = seq // dil
    assert sub_len % A_BLOCK == 0
    tq = min(sub_len, A_TOKENS_PER_STEP // dil)
    assert tq % A_BLOCK == 0
    nsub = tq // A_BLOCK
    gw = 3 * A_OUT
    out_spec = pl.BlockSpec((A_OUT // LANE, tq * dil, LANE), lambda b, t: (0, b * (sub_len // tq) + t, 0))
    return pl.pallas_call(
        functools.partial(_mixer_a_kernel, dil=dil, nsub=nsub),
        grid=(batch, sub_len // tq),
        in_specs=[pl.BlockSpec((1, dil, tq, gw), lambda b, t: (b, 0, t, 0)),
                  pl.BlockSpec((1, dil, A_BLOCK, gw), lambda b, t: (b, 0, jnp.maximum(t * nsub - 1, 0), 0)),
                  pl.BlockSpec((A_HPG, 1, 2 * A_BLOCK, A_BLOCK), lambda b, t: (g, 0, 0, 0))],
        out_specs=[out_spec, out_spec],
        out_shape=[jax.ShapeDtypeStruct((A_OUT // LANE, batch * seq, LANE), F32)] * 2,
        compiler_params=_cparams(("arbitrary", "arbitrary")),
        name=f"mixer_a_d{dil}",
    )(za, za, bias_a)


SUBLANES = 8


def _mixer_b_kernel(z_ref, cw_ref, cb_ref, wax_ref, bax_ref, lam_ref, y_ref,
                    xs_scr, a_scr, b_scr, h_scr, *, ts):
    w = LRU_WIDTH

    @pl.when(pl.program_id(1) == 0)
    def _():
        xs_scr[0:SUBLANES, :] = jnp.zeros((SUBLANES, w), F32)
        h_scr[...] = jnp.zeros((1, w), F32)

    x = z_ref[0, :, 0:w].astype(F32)
    xs_scr[SUBLANES:SUBLANES + ts, :] = x
    xc = cb_ref[...]
    for i in range(CONV_WIDTH):
        off = SUBLANES - (CONV_WIDTH - 1) + i
        xc = xc + xs_scr[off:off + ts, :] * cw_ref[i:i + 1, :]
    xs_scr[0:SUBLANES, :] = xs_scr[ts:ts + SUBLANES, :]

    ra = jnp.dot(xc.astype(BF16), wax_ref[...], preferred_element_type=F32) + bax_ref[...]
    r = jax.nn.sigmoid(ra[:, :w])
    ig = jax.nn.sigmoid(ra[:, w:])
    nl = -lam_ref[...]
    softplus = jnp.maximum(nl, 0.0) + jnp.log1p(jnp.exp(-jnp.abs(nl)))
    log_a = -LRU_C * r * softplus
    a = jnp.exp(log_a)
    a_scr[...] = a
    b_scr[...] = jnp.sqrt(-jnp.tanh(log_a) * (a * a + 1.0)) * (ig * xc)

    row = lax.broadcasted_iota(jnp.int32, (SUBLANES, w), 0)

    def body(c, h):
        i = pl.multiple_of(c * SUBLANES, SUBLANES)
        a = a_scr[pl.ds(i, SUBLANES), :]
        b = b_scr[pl.ds(i, SUBLANES), :]
        for s in (1, 2, 4):
            keep = row >= s
            a_sh = jnp.where(keep, pltpu.roll(a, s, 0), 1.0)
            b_sh = jnp.where(keep, pltpu.roll(b, s, 0), 0.0)
            b = a * b_sh + b
            a = a * a_sh
        hs = a * h + b
        b_scr[pl.ds(i, SUBLANES), :] = hs
        return hs[SUBLANES - 1:SUBLANES, :]

    h_scr[...] = lax.fori_loop(0, ts // SUBLANES, body, h_scr[...])

    gb = z_ref[0, :, w:2 * w].astype(F32)
    cdf = 0.5 * (1.0 + jnp.tanh(math.sqrt(2.0 / math.pi) * (gb + 0.044715 * (gb * gb * gb))))
    y_ref[0] = (b_scr[...] * (gb * cdf)).astype(y_ref.dtype)


def _block_diag(wblocks):
    nb, di, do = wblocks.shape
    eye = jnp.eye(nb, dtype=wblocks.dtype)
    return (eye[:, None, :, None] * wblocks[:, :, None, :]).reshape(nb * di, nb * do)


def _mixer_b(zb, conv_w, conv_b, lru_wa, lru_ba, lru_wx, lru_bx, lru_lam, batch, seq, ts=512):
    w = LRU_WIDTH
    wax = jnp.concatenate([_block_diag(lru_wa), _block_diag(lru_wx)], axis=1).astype(BF16)
    bax = jnp.concatenate([lru_ba, lru_bx]).reshape(1, 2 * w)
    y = pl.pallas_call(
        functools.partial(_mixer_b_kernel, ts=ts),
        grid=(batch, seq // ts),
        in_specs=[pl.BlockSpec((1, ts, 2 * w), lambda b, t: (b, t, 0)),
                  _const_spec((CONV_WIDTH, w)), _const_spec((1, w)),
                  _const_spec((w, 2 * w)), _const_spec((1, 2 * w)), _const_spec((1, w))],
        out_specs=pl.BlockSpec((1, ts, w), lambda b, t: (b, t, 0)),
        out_shape=jax.ShapeDtypeStruct((batch, seq, w), BF16),
        scratch_shapes=[pltpu.VMEM((ts + SUBLANES, w), F32), pltpu.VMEM((ts, w), F32),
                        pltpu.VMEM((ts, w), F32), pltpu.VMEM((1, w), F32)],
        compiler_params=_cparams(("arbitrary", "arbitrary")),
        name="mixer_b",
    )(zb.reshape(batch, seq, 2 * w), conv_w, conv_b.reshape(1, w), wax, bax, lru_lam.reshape(1, w))
    return y.reshape(batch * seq, w)


HEAD_PAIR = LANE // HEAD_DIM


def _mixer_c_prologue(q_ref, k_ref, qm_scr, sel_scr, *, nblk):
    blk = MOBA_BLOCK
    seq = nblk * blk
    kmean = jnp.concatenate(
        [jnp.sum(k_ref[0, n * blk:(n + 1) * blk, :].astype(F32), axis=0, keepdims=True) for n in range(nblk)],
        axis=0) * (1.0 / blk)
    km_hi = kmean.astype(BF16)
    km_lo = (kmean - km_hi.astype(F32)).astype(BF16)
    q = q_ref[0]
    lane = lax.broadcasted_iota(jnp.int32, q.shape, 1)
    blk_id = lax.broadcasted_iota(jnp.int32, (nblk, seq), 0)
    q_blk = lax.broadcasted_iota(jnp.int32, (nblk, seq), 1) // blk
    past = blk_id < q_blk
    for hh in range(HEAD_PAIR):
        qm = jnp.where((lane >= hh * HEAD_DIM) & (lane < (hh + 1) * HEAD_DIM), q, jnp.zeros_like(q))
        qm_scr[hh] = qm
        gate = (lax.dot_general(km_hi, qm, NT_DIMS, preferred_element_type=F32)
                + lax.dot_general(km_lo, qm, NT_DIMS, preferred_element_type=F32))
        gate = jnp.where(past, gate, NEG_INF)
        sel = jnp.zeros((nblk, seq), F32)
        for _ in range(min(MOBA_TOPK, nblk - 1)):
            top = jnp.max(gate, axis=0, keepdims=True)
            first = jnp.min(jnp.where(gate == top, blk_id, nblk), axis=0, keepdims=True)
            pick = blk_id == first
            sel = jnp.where(pick, 1.0, sel)
            gate = jnp.where(pick, -jnp.inf, gate)
        sel_scr[hh] = jnp.where(past, sel, 0.0)


def _run_round_robin(*gens):
    gens = list(gens)
    while gens:
        gens = [g for g in gens if next(g, StopIteration) is not StopIteration]


def _mixer_c_block(c, k_ref, vt_ref, bias_ref, o_ref, qm_scr, sel_scr, s_scr, m_scr, p_scr, *, ndb):
    blk = MOBA_BLOCK
    par = c % 2
    qcols = slice(c * blk, (c + 1) * blk)
    fold = lambda t: t.reshape(blk // SUBLANES, SUBLANES, blk)
    heads = range(HEAD_PAIR)
    pairs = [(k0, min(k0 + 2, c + 1)) for k0 in range(0, c + 1, 2)]
    picked = lambda hh, kb: sel_scr[hh, kb:kb + 1, qcols] > 0.0
    far_bias = lambda hh: bias_ref[hh, ndb, 0:1, :]

    def logits(hh):
        qm = qm_scr[hh, qcols, :]
        m8 = jnp.full((SUBLANES, blk), NEG_INF, F32)
        for k0, k1 in pairs:
            s = lax.dot_general(k_ref[0, k0 * blk:k1 * blk, :], qm, NT_DIMS, preferred_element_type=F32)
            for kb in range(k0, k1):
                su = s[(kb - k0) * blk:(kb - k0 + 1) * blk]
                if c - kb < ndb:
                    su = su + bias_ref[hh, c - kb]
                    t8 = jnp.max(fold(su), axis=0)
                else:
                    t8 = jnp.max(fold(su), axis=0) + far_bias(hh)
                s_scr[par, hh, kb * blk:(kb + 1) * blk, :] = su
                m8 = jnp.maximum(m8, t8 if kb == c else jnp.where(picked(hh, kb), t8, NEG_INF))
            yield
        m_scr[par, hh] = jnp.broadcast_to(jnp.max(m8, axis=0, keepdims=True), (SUBLANES, blk))

    outs = {}

    def attend(hh):
        m = m_scr[par, hh, 0:1, :]
        pv = jnp.zeros((V_ROWS, blk), F32)
        for k0, k1 in pairs:
            for kb in range(k0, k1):
                mk = m if kb == c else jnp.where(picked(hh, kb), m, -NEG_INF)
                if c - kb >= ndb:
                    mk = mk - far_bias(hh)
                x = s_scr[par, hh, kb * blk:(kb + 1) * blk, :] - mk
                p_scr[hh, kb * blk:(kb + 1) * blk, :] = jnp.exp2(x.astype(BF16))
            pv = pv + jnp.dot(vt_ref[0, hh * V_ROWS:(hh + 1) * V_ROWS, k0 * blk:k1 * blk],
                              p_scr[hh, k0 * blk:k1 * blk, :], preferred_element_type=F32)
            yield
        outs[hh] = pv[:HEAD_DIM] / pv[HEAD_DIM:HEAD_DIM + 1]
        if len(outs) == HEAD_PAIR:
            o_ref[0, qcols, :] = jnp.concatenate([outs[h] for h in heads], axis=0).T.astype(o_ref.dtype)

    return [logits(hh) for hh in heads], [attend(hh) for hh in heads]


def _mixer_c_kernel(q_ref, k_ref, vt_ref, bias_ref, o_ref, qm_scr, sel_scr, s_scr, m_scr, p_scr, *, nblk, ndb):
    _mixer_c_prologue(q_ref, k_ref, qm_scr, sel_scr, nblk=nblk)
    stages = functools.partial(_mixer_c_block, k_ref=k_ref, vt_ref=vt_ref, bias_ref=bias_ref, o_ref=o_ref,
                               qm_scr=qm_scr, sel_scr=sel_scr, s_scr=s_scr, m_scr=m_scr, p_scr=p_scr, ndb=ndb)

    def step(t):
        gens = []
        if t < nblk:
            gens += stages(t)[0]
        if t >= 1:
            gens += stages(t - 1)[1]
        _run_round_robin(*gens)

    def body(i, carry):
        for t in range(nblk + 1):
            pl.when(i == t)(functools.partial(step, t))
        return carry

    lax.fori_loop(0, nblk + 1, body, 0)


def _mixer_c(zc, vt, bias_c, batch, seq, ndb):
    blk = MOBA_BLOCK
    assert seq % blk == 0
    nblk = seq // blk
    npair = C_HEADS // HEAD_PAIR
    z = zc.reshape(batch, seq, 2 * C_WIDTH)
    o = pl.pallas_call(
        functools.partial(_mixer_c_kernel, nblk=nblk, ndb=ndb),
        grid=(batch, npair),
        in_specs=[pl.BlockSpec((1, seq, LANE), lambda b, hp: (b, 0, hp)),
                  pl.BlockSpec((1, seq, LANE), lambda b, hp: (b, 0, npair + hp)),
                  pl.BlockSpec((1, HEAD_PAIR * V_ROWS, seq), lambda b, hp: (b, hp, 0)),
                  pl.BlockSpec((HEAD_PAIR, ndb + 1, blk, blk), lambda b, hp: (hp, 0, 0, 0))],
        out_specs=pl.BlockSpec((1, seq, LANE), lambda b, hp: (b, 0, hp)),
        out_shape=jax.ShapeDtypeStruct((batch, seq, C_WIDTH), BF16),
        scratch_shapes=[pltpu.VMEM((HEAD_PAIR, seq, LANE), BF16), pltpu.VMEM((HEAD_PAIR, nblk, seq), F32),
                        pltpu.VMEM((2, HEAD_PAIR, seq, blk), F32), pltpu.VMEM((2, HEAD_PAIR, SUBLANES, blk), F32),
                        pltpu.VMEM((HEAD_PAIR, seq, blk), BF16)],
        compiler_params=_cparams(("arbitrary", "arbitrary")),
        name="mixer_c",
    )(z, z, vt, bias_c)
    return o.reshape(batch * seq, C_WIDTH)


def _merge_kernel(x_ref, o1_ref, o2_ref, o3_ref, l1_ref, l2_ref, l3_ref, ob_ref, oc_ref, zg_ref,
                  pa_ref, pb_ref, pc_ref, wo_ref, out_ref):
    d = D_MODEL
    halves = lambda ref: jnp.concatenate([ref[c] for c in range(ref.shape[0])], axis=-1)
    l1, l2, l3 = halves(l1_ref), halves(l2_ref), halves(l3_ref)
    m = jnp.maximum(jnp.maximum(l1, l2), l3)
    e1, e2, e3 = jnp.exp(l1 - m), jnp.exp(l2 - m), jnp.exp(l3 - m)
    o_a = (e1 * halves(o1_ref) + e2 * halves(o2_ref) + e3 * halves(o3_ref)) / (e1 + e2 + e3)

    def branch(o, p_ref, k):
        gate = jax.nn.sigmoid(zg_ref[:, k * d:(k + 1) * d].astype(F32))
        return gate * jnp.dot(o, p_ref[...], preferred_element_type=F32)

    merged = branch(o_a.astype(BF16), pa_ref, 0) + branch(ob_ref[...], pb_ref, 1) + branch(oc_ref[...], pc_ref, 2)
    out_ref[...] = x_ref[...] + jnp.dot(merged.astype(BF16), wo_ref[...], preferred_element_type=F32)


def _merge(x2d, oa, lse, ob, oc, zg, p_a, p_b, p_c, w_out, tm=512):
    n = x2d.shape[0]
    row = lambda width: pl.BlockSpec((tm, width), lambda i: (i, 0))
    return pl.pallas_call(
        _merge_kernel,
        grid=(n // tm,),
        in_specs=[row(D_MODEL)] + [pl.BlockSpec((A_OUT // LANE, tm, LANE), lambda i: (0, i, 0))] * 6
                 + [row(LRU_WIDTH), row(C_WIDTH), row(3 * D_MODEL),
                  _const_spec((A_OUT, D_MODEL)), _const_spec((LRU_WIDTH, D_MODEL)),
                  _const_spec((C_WIDTH, D_MODEL)), _const_spec((D_MODEL, D_MODEL))],
        out_specs=row(D_MODEL),
        out_shape=jax.ShapeDtypeStruct((n, D_MODEL), F32),
        compiler_params=_cparams(("arbitrary",)),
        name="merge",
    )(x2d, *oa, *lse, ob, oc, zg, p_a.astype(BF16), p_b.astype(BF16), p_c.astype(BF16), w_out.astype(BF16))


def _rms(x, g):
    return x * lax.rsqrt(jnp.mean(x * x, axis=-1, keepdims=True) + EPS) * g


def _ffn_kernel(x_ref, g_ref, wgu_ref, wd_ref, gf_ref, out_ref, h_scr, act_scr, *, final_norm):
    x = x_ref[...]
    h_scr[...] = _rms(x, g_ref[...]).astype(BF16)
    for off, w in _col_chunks(0, FFN_HIDDEN, 256):
        gate = jnp.dot(h_scr[...], wgu_ref[:, off:off + w], preferred_element_type=F32)
        up = jnp.dot(h_scr[...], wgu_ref[:, FFN_HIDDEN + off:FFN_HIDDEN + off + w], preferred_element_type=F32)
        act_scr[:, off:off + w] = (gate * jax.nn.sigmoid(gate) * up).astype(BF16)
    y = x + jnp.dot(act_scr[...], wd_ref[...], preferred_element_type=F32)
    if final_norm:
        y = _rms(y, gf_ref[...])
    out_ref[...] = y


def _ffn(x2d, g_ffn, w_gu, w_down, g_final, final_norm, tm=512):
    n = x2d.shape[0]
    row = pl.BlockSpec((tm, D_MODEL), lambda i: (i, 0))
    return pl.pallas_call(
        functools.partial(_ffn_kernel, final_norm=final_norm),
        grid=(n // tm,),
        in_specs=[row, _const_spec((1, D_MODEL)), _const_spec((D_MODEL, 2 * FFN_HIDDEN)),
                  _const_spec((FFN_HIDDEN, D_MODEL)), _const_spec((1, D_MODEL))],
        out_specs=row,
        out_shape=jax.ShapeDtypeStruct((n, D_MODEL), F32),
        scratch_shapes=[pltpu.VMEM((tm, D_MODEL), BF16), pltpu.VMEM((tm, FFN_HIDDEN), BF16)],
        compiler_params=_cparams(("arbitrary",)),
        name="ffn",
    )(x2d, g_ffn.reshape(1, D_MODEL), w_gu.astype(BF16), w_down.astype(BF16), g_final.reshape(1, D_MODEL))


def kernel(x, rel_bias, g_mix, w_in, conv_w, conv_b, lru_wa, lru_ba, lru_wx, lru_bx, lru_lam,
           p_a, p_b, p_c, w_out, g_ffn, w_gu, w_down, g_final):
    batch, seq, d = x.shape
    assert d == D_MODEL and w_in.shape[-1] == IN_COLS
    depth = w_in.shape[0]
    nblk = seq // MOBA_BLOCK
    ndb = _moba_far_blocks(nblk)

    bias_a = _build_bias(rel_bias, _bucket_index_a(), A_HEADS, 0, A_HPG)
    bias_c = _build_bias(rel_bias, _bucket_index_c(ndb), C_HEADS, A_HEADS, None, LOG2E)

    w_in, p_a, p_b, p_c, w_out, w_gu, w_down = (
        t.astype(BF16) for t in (w_in, p_a, p_b, p_c, w_out, w_gu, w_down))

    x2d = x.reshape(batch * seq, d)
    for l in range(depth):
        *za, zb, zc, vt, zg = _in_proj(x2d, g_mix[l], w_in[l], batch, seq)
        oa, lse = zip(*[_mixer_a_group(za[g], bias_a, g, dil, batch, seq)
                        for g, (_, dil) in enumerate(A_GROUPS)])
        ob = _mixer_b(zb, conv_w[l], conv_b[l], lru_wa[l], lru_ba[l], lru_wx[l], lru_bx[l], lru_lam[l],
                      batch, seq)
        oc = _mixer_c(zc, vt, bias_c, batch, seq, ndb)
        x2d = _merge(x2d, oa, lse, ob, oc, zg, p_a[l], p_b[l], p_c[l], w_out[l])
        x2d = _ffn(x2d, g_ffn[l], w_gu[l], w_down[l], g_final, final_norm=(l == depth - 1))
    return x2d.reshape(batch, seq, d)
```

```python
import functools
import math

import numpy as np
import jax
import jax.numpy as jnp
from jax import lax
from jax.experimental import pallas as pl
from jax.experimental.pallas import tpu as pltpu

F32 = jnp.float32
BF16 = jnp.bfloat16

D_MODEL = 1024
HEAD_DIM = 64
SCALE = HEAD_DIM ** -0.5
LOG2E = math.log2(math.e)
NEG_INF = -1e30
EPS = 1e-6
A_GROUPS = ((128, 1), (512, 4), (2048, 16))
A_HPG = 4
A_HEADS = A_HPG * len(A_GROUPS)
A_WIDTH = A_HEADS * HEAD_DIM
A_OUT = A_HPG * HEAD_DIM
A_BLOCK = 128
LRU_WIDTH = D_MODEL // 2
LRU_BLOCKS = 8
CONV_WIDTH = 4
LRU_C = 8.0
C_HEADS = 8
C_WIDTH = C_HEADS * HEAD_DIM
MOBA_BLOCK = 256
MOBA_TOPK = 3
REL_BUCKETS = 32
REL_MAX_DIST = 2048
FFN_HIDDEN = 2816
IN_COLS = 3 * A_WIDTH + 2 * LRU_WIDTH + 3 * C_WIDTH + 3 * D_MODEL
V_C_OFF = 3 * A_WIDTH + 2 * LRU_WIDTH + 2 * C_WIDTH

LANE = 128
V_ROWS = HEAD_DIM + 16
VMEM_LIMIT = 56 * 1024 * 1024

NT_DIMS = (((1,), (1,)), ((), ()))


def _cparams(sem, vmem=VMEM_LIMIT):
    return pltpu.CompilerParams(dimension_semantics=sem, vmem_limit_bytes=vmem)


def _const_spec(shape):
    nd = len(shape)
    return pl.BlockSpec(shape, lambda *_: (0,) * nd, pipeline_mode=pl.Buffered(1))


def _layer_spec(stacked, layer):
    return pl.BlockSpec((None,) + stacked.shape[1:], lambda *_: (layer, 0, 0), pipeline_mode=pl.Buffered(1))


def _rel_bucket_np(dist):
    max_exact = REL_BUCKETS // 2
    d = np.maximum(dist, 0)
    df = np.maximum(d, 1).astype(np.float32)
    large = max_exact + (np.log(df / np.float32(max_exact)) / np.float32(math.log(REL_MAX_DIST / max_exact))
                         * np.float32(REL_BUCKETS - max_exact)).astype(np.int32)
    large = np.minimum(large, REL_BUCKETS - 1)
    return np.where(d < max_exact, d, large).astype(np.int32)


def _bucket_index_a():
    qi = np.arange(A_BLOCK)[None, :] + A_BLOCK
    kj = np.arange(2 * A_BLOCK)[:, None]
    delta = qi - kj
    mats = []
    for window, dil in A_GROUPS:
        band = (delta >= 0) & (delta <= window // dil)
        mats.append(np.where(band, _rel_bucket_np(delta * dil), -1))
    return np.stack(mats).astype(np.int32)


def _moba_far_blocks(nblk):
    for db in range(1, nblk + 1):
        lo = db * MOBA_BLOCK - (MOBA_BLOCK - 1)
        if np.all(_rel_bucket_np(np.arange(lo, nblk * MOBA_BLOCK)) == REL_BUCKETS - 1):
            return db
    return nblk


def _bucket_index_c(ndb):
    k = np.arange(MOBA_BLOCK)[:, None]
    q = np.arange(MOBA_BLOCK)[None, :]
    mats = []
    for db in range(ndb + 1):
        dist = db * MOBA_BLOCK + q - k
        b = _rel_bucket_np(dist)
        if db == 0:
            b = np.where(dist >= 0, b, -1)
        mats.append(b)
    return np.stack(mats).astype(np.int32)


def _bias_kernel(tab_ref, idx_ref, out_ref, *, head_off, unit, present):
    h = pl.program_id(0) + head_off
    for m, buckets in enumerate(present):
        idx = idx_ref[m]
        out = jnp.full(idx.shape, NEG_INF, F32)
        for b in buckets:
            out = jnp.where(idx == b, tab_ref[b, h] * unit, out)
        out_ref[0, m] = out


def _build_bias(rel_bias, idx, n_heads, head_off, heads_per_idx_group, unit=1.0):
    r, c = idx.shape[-2:]
    if heads_per_idx_group is None:
        n_mats = idx.shape[0]
        idx_map = lambda h: (0, 0, 0)
        present = tuple(tuple(int(b) for b in np.unique(m[m >= 0])) for m in idx)
    else:
        n_mats = 1
        idx_map = lambda h: (h // heads_per_idx_group, 0, 0)
        present = (tuple(int(b) for b in np.unique(idx[idx >= 0])),)
    return pl.pallas_call(
        functools.partial(_bias_kernel, head_off=head_off, unit=unit, present=present),
        grid=(n_heads,),
        in_specs=[pl.BlockSpec(memory_space=pltpu.SMEM),
                  pl.BlockSpec((n_mats, r, c), idx_map)],
        out_specs=pl.BlockSpec((1, n_mats, r, c), lambda h: (h, 0, 0, 0)),
        out_shape=jax.ShapeDtypeStruct((n_heads, n_mats, r, c), F32),
        compiler_params=_cparams(("arbitrary",)),
        name="rel_bias_build",
    )(rel_bias, jnp.asarray(idx))


def _col_chunks(start, stop, width=512):
    out = []
    while start < stop:
        w = min(width, stop - start)
        out.append((start, w))
        start += w
    return out


def _in_proj_kernel(x_ref, g_ref, w_ref, wvt_ref, za0_ref, za1_ref, za2_ref, zb_ref, zc_ref, vt_ref, zg_ref,
                    h_scr, dil_scr, *, tm):
    x = x_ref[...]
    ms = jnp.mean(x * x, axis=-1, keepdims=True)
    h_scr[...] = (x * lax.rsqrt(ms + EPS) * g_ref[...]).astype(BF16)

    def proj(base, width):
        return jnp.dot(h_scr[...], w_ref[:, base:base + width], preferred_element_type=F32)

    def seg(out_ref, base, width, scaled_cols=0, scale=1.0, chunk=512):
        assert scaled_cols % chunk == 0
        for off, w in _col_chunks(0, width, chunk):
            r = proj(base + off, w)
            if off < scaled_cols:
                r = r * scale
            out_ref[:, off:off + w] = r.astype(out_ref.dtype)

    for g, (za_ref, (_, dil)) in enumerate(zip((za0_ref, za1_ref, za2_ref), A_GROUPS)):
        qkv = [proj(s * A_WIDTH + g * A_OUT, A_OUT) for s in range(3)]
        qkv[0] = qkv[0] * SCALE
        if dil == 1:
            for s, part in enumerate(qkv):
                za_ref[0, 0, :, s * A_OUT:(s + 1) * A_OUT] = part.astype(BF16)
        else:
            planes = [part[:, c * LANE:(c + 1) * LANE] for part in qkv for c in range(A_OUT // LANE)]
            for c, plane in enumerate(planes):
                dil_scr[c] = plane
            for r in range(dil):
                for c in range(len(planes)):
                    za_ref[0, r, :, c * LANE:(c + 1) * LANE] = (
                        dil_scr[c, pl.ds(r, tm // dil, stride=dil), :].astype(BF16))

    wa, wb, wc = 3 * A_WIDTH, 2 * LRU_WIDTH, 2 * C_WIDTH
    seg(zb_ref, wa, wb)
    seg(zc_ref, wa + wb, wc, C_WIDTH, SCALE * LOG2E)
    seg(zg_ref, V_C_OFF + C_WIDTH, 3 * D_MODEL)
    vt = lax.dot_general(wvt_ref[...], h_scr[...], NT_DIMS, preferred_element_type=F32).astype(BF16)
    for h in range(C_HEADS):
        vt_ref[0, h * V_ROWS:h * V_ROWS + HEAD_DIM, :] = vt[h * HEAD_DIM:(h + 1) * HEAD_DIM]
        vt_ref[0, h * V_ROWS + HEAD_DIM:(h + 1) * V_ROWS, :] = jnp.ones((V_ROWS - HEAD_DIM, tm), BF16)


def _in_proj(x2d, g, w_in, layer, batch, seq, tm=512):
    n = x2d.shape[0]
    wvt = w_in[layer, :, V_C_OFF:V_C_OFF + C_WIDTH].T
    wb, wc, wg = 2 * LRU_WIDTH, 2 * C_WIDTH, 3 * D_MODEL
    gw = 3 * A_OUT
    tiles_per_seq = seq // tm
    row = lambda i: (i, 0)
    seq_tile = lambda i: (i // tiles_per_seq, 0, i % tiles_per_seq, 0)
    za_specs = [pl.BlockSpec((1, dil, tm // dil, gw), seq_tile) for _, dil in A_GROUPS]
    za_shapes = [jax.ShapeDtypeStruct((batch, dil, seq // dil, gw), BF16) for _, dil in A_GROUPS]
    return pl.pallas_call(
        functools.partial(_in_proj_kernel, tm=tm),
        grid=(n // tm,),
        in_specs=[pl.BlockSpec((tm, D_MODEL), row),
                  _const_spec((1, D_MODEL)),
                  _layer_spec(w_in, layer),
                  _const_spec(wvt.shape)],
        out_specs=za_specs + [pl.BlockSpec((tm, wb), row),
                              pl.BlockSpec((tm, wc), row),
                              pl.BlockSpec((1, C_HEADS * V_ROWS, tm),
                                           lambda i: (i // tiles_per_seq, 0, i % tiles_per_seq)),
                              pl.BlockSpec((tm, wg), row)],
        out_shape=za_shapes + [jax.ShapeDtypeStruct((n, wb), BF16),
                               jax.ShapeDtypeStruct((n, wc), BF16),
                               jax.ShapeDtypeStruct((batch, C_HEADS * V_ROWS, seq), BF16),
                               jax.ShapeDtypeStruct((n, wg), BF16)],
        scratch_shapes=[pltpu.VMEM((tm, D_MODEL), BF16), pltpu.VMEM((gw // LANE, tm, LANE), F32)],
        compiler_params=_cparams(("arbitrary",)),
        name="in_proj",
    )(x2d, g.reshape(1, D_MODEL), w_in, wvt)


A_TOKENS_PER_STEP = 2048
A_CHAINS = 4


def _mixer_a_kernel(z_ref, zp_ref, bias_ref, o_ref, lse_ref, *, dil, nsub):
    t = pl.program_id(1)
    blk, width = A_BLOCK, A_OUT
    kcols, vcols = slice(width, 2 * width), slice(2 * width, 3 * width)
    lane_head = lax.broadcasted_iota(jnp.int32, (blk, width), 1) // HEAD_DIM
    in_prev = (lax.broadcasted_iota(jnp.int32, (2 * blk, 2 * blk), 0) < blk).astype(F32)
    no_prev = in_prev * jnp.where(t == 0, NEG_INF, 0.0)
    transpose_v = lambda v: v.astype(F32).T.astype(BF16)

    ones_rows = jnp.ones((V_ROWS - HEAD_DIM, 2 * blk), BF16)
    blocks, windows = {}, {}

    def block_kv(r, j):
        if (r, j) not in blocks:
            src = zp_ref[0, r] if j < 0 else z_ref[0, r, j * blk:(j + 1) * blk, :]
            blocks[(r, j)] = (src[:, kcols], transpose_v(src[:, vcols]))
        return blocks[(r, j)]

    def window(r, j):
        if (r, j) not in windows:
            (k_prev, vt_prev), (k_cur, vt_cur) = block_kv(r, j - 1), block_kv(r, j)
            k_win = jnp.concatenate([k_prev, k_cur], axis=0)
            vt_win = jnp.concatenate([vt_prev, vt_cur], axis=1)
            windows[(r, j)] = (z_ref[0, r, j * blk:(j + 1) * blk, 0:width], k_win, vt_win)
        return windows[(r, j)]

    def logits(r, j, hp):
        q, k_win, _ = window(r, j)
        heads = (2 * hp, 2 * hp + 1)
        q2 = jnp.concatenate([jnp.where(lane_head == h, q, jnp.zeros_like(q)) for h in heads], axis=0)
        bias2 = jnp.concatenate([bias_ref[h, 0] for h in heads], axis=1)
        s = lax.dot_general(k_win, q2, NT_DIMS, preferred_element_type=F32) + bias2
        return s + no_prev if j == 0 else s

    def attend(r, j, hp, s):
        vt_win = window(r, j)[2]
        m = jnp.max(s, axis=0, keepdims=True)
        p = jnp.exp((s - m).astype(BF16))
        vt_ext = jnp.concatenate([part for h in (2 * hp, 2 * hp + 1)
                                  for part in (vt_win[h * HEAD_DIM:(h + 1) * HEAD_DIM, :], ones_rows)], axis=0)
        pv = jnp.dot(vt_ext, p, preferred_element_type=F32)
        res = []
        for u in range(2):
            cols = slice(u * blk, (u + 1) * blk)
            den = pv[u * V_ROWS + HEAD_DIM:u * V_ROWS + HEAD_DIM + 1, cols]
            res.append((pv[u * V_ROWS:u * V_ROWS + HEAD_DIM, cols] / den,
                        jnp.broadcast_to(m[:, cols] + jnp.log(den), (HEAD_DIM, blk))))
        return res

    n_hp = A_HPG // 2

    def chain(units):
        s_next = logits(*units[0])
        outs, lses = [], []
        for i, (r, j, hp) in enumerate(units):
            s_cur = s_next
            if i + 1 < len(units):
                s_next = logits(*units[i + 1])
            for o, lse in attend(r, j, hp, s_cur):
                outs.append(o)
                lses.append(lse)
            if hp == n_hp - 1:
                o_t = jnp.concatenate(outs, axis=0).T
                lse_t = jnp.concatenate(lses, axis=0).T
                outs, lses = [], []
                dst = slice(j * blk, (j + 1) * blk) if dil == 1 else pl.ds(j * blk * dil + r, blk, stride=dil)
                for c in range(width // LANE):
                    o_ref[c, dst, :] = o_t[:, c * LANE:(c + 1) * LANE]
                    lse_ref[c, dst, :] = lse_t[:, c * LANE:(c + 1) * LANE]
            yield

    units = [(r, j, hp) for r in range(dil) for j in range(nsub) for hp in range(n_hp)]
    per_chain = len(units) // A_CHAINS
    _run_round_robin(*[chain(units[i * per_chain:(i + 1) * per_chain]) for i in range(A_CHAINS)])


def _mixer_a_group(za, bias_a, g, dil, batch, seq):
    sub_len = seq // dil
    assert sub_len % A_BLOCK == 0
    tq = min(sub_len, A_TOKENS_PER_STEP // dil)
    assert tq % A_BLOCK == 0
    nsub = tq // A_BLOCK
    gw = 3 * A_OUT
    out_spec = pl.BlockSpec((A_OUT // LANE, tq * dil, LANE), lambda b, t: (0, b * (sub_len // tq) + t, 0))
    return pl.pallas_call(
        functools.partial(_mixer_a_kernel, dil=dil, nsub=nsub),
        grid=(batch, sub_len // tq),
        in_specs=[pl.BlockSpec((1, dil, tq, gw), lambda b, t: (b, 0, t, 0)),
                  pl.BlockSpec((1, dil, A_BLOCK, gw), lambda b, t: (b, 0, jnp.maximum(t * nsub - 1, 0), 0)),
                  pl.BlockSpec((A_HPG, 1, 2 * A_BLOCK, A_BLOCK), lambda b, t: (g, 0, 0, 0))],
        out_specs=[out_spec, out_spec],
        out_shape=[jax.ShapeDtypeStruct((A_OUT // LANE, batch * seq, LANE), F32)] * 2,
        compiler_params=_cparams(("arbitrary", "arbitrary")),
        name=f"mixer_a_d{dil}",
    )(za, za, bias_a)


SUBLANES = 8


def _mixer_b_kernel(z_ref, cw_ref, cb_ref, wax_ref, bax_ref, lam_ref, y_ref,
                    xs_scr, a_scr, b_scr, h_scr, *, ts):
    w = LRU_WIDTH

    @pl.when(pl.program_id(1) == 0)
    def _():
        xs_scr[0:SUBLANES, :] = jnp.zeros((SUBLANES, w), F32)
        h_scr[...] = jnp.zeros((1, w), F32)

    x = z_ref[0, :, 0:w].astype(F32)
    xs_scr[SUBLANES:SUBLANES + ts, :] = x
    xc = cb_ref[...]
    for i in range(CONV_WIDTH):
        off = SUBLANES - (CONV_WIDTH - 1) + i
        xc = xc + xs_scr[off:off + ts, :] * cw_ref[i:i + 1, :]
    xs_scr[0:SUBLANES, :] = xs_scr[ts:ts + SUBLANES, :]

    ra = jnp.dot(xc.astype(BF16), wax_ref[...], preferred_element_type=F32) + bax_ref[...]
    r = jax.nn.sigmoid(ra[:, :w])
    ig = jax.nn.sigmoid(ra[:, w:])
    nl = -lam_ref[...]
    softplus = jnp.maximum(nl, 0.0) + jnp.log1p(jnp.exp(-jnp.abs(nl)))
    log_a = -LRU_C * r * softplus
    a = jnp.exp(log_a)
    a_scr[...] = a
    b_scr[...] = jnp.sqrt(-jnp.tanh(log_a) * (a * a + 1.0)) * (ig * xc)

    row = lax.broadcasted_iota(jnp.int32, (SUBLANES, w), 0)

    def body(c, h):
        i = pl.multiple_of(c * SUBLANES, SUBLANES)
        a = a_scr[pl.ds(i, SUBLANES), :]
        b = b_scr[pl.ds(i, SUBLANES), :]
        for s in (1, 2, 4):
            keep = row >= s
            a_sh = jnp.where(keep, pltpu.roll(a, s, 0), 1.0)
            b_sh = jnp.where(keep, pltpu.roll(b, s, 0), 0.0)
            b = a * b_sh + b
            a = a * a_sh
        hs = a * h + b
        b_scr[pl.ds(i, SUBLANES), :] = hs
        return hs[SUBLANES - 1:SUBLANES, :]

    h_scr[...] = lax.fori_loop(0, ts // SUBLANES, body, h_scr[...])

    gb = z_ref[0, :, w:2 * w].astype(F32)
    cdf = 0.5 * (1.0 + jnp.tanh(math.sqrt(2.0 / math.pi) * (gb + 0.044715 * (gb * gb * gb))))
    y_ref[0] = (b_scr[...] * (gb * cdf)).astype(y_ref.dtype)


def _block_diag(wblocks):
    nb, di, do = wblocks.shape
    eye = jnp.eye(nb, dtype=wblocks.dtype)
    return (eye[:, None, :, None] * wblocks[:, :, None, :]).reshape(nb * di, nb * do)


def _mixer_b(zb, conv_w, conv_b, lru_wa, lru_ba, lru_wx, lru_bx, lru_lam, batch, seq, ts=512):
    w = LRU_WIDTH
    wax = jnp.concatenate([_block_diag(lru_wa), _block_diag(lru_wx)], axis=1).astype(BF16)
    bax = jnp.concatenate([lru_ba, lru_bx]).reshape(1, 2 * w)
    y = pl.pallas_call(
        functools.partial(_mixer_b_kernel, ts=ts),
        grid=(batch, seq // ts),
        in_specs=[pl.BlockSpec((1, ts, 2 * w), lambda b, t: (b, t, 0)),
                  _const_spec((CONV_WIDTH, w)), _const_spec((1, w)),
                  _const_spec((w, 2 * w)), _const_spec((1, 2 * w)), _const_spec((1, w))],
        out_specs=pl.BlockSpec((1, ts, w), lambda b, t: (b, t, 0)),
        out_shape=jax.ShapeDtypeStruct((batch, seq, w), BF16),
        scratch_shapes=[pltpu.VMEM((ts + SUBLANES, w), F32), pltpu.VMEM((ts, w), F32),
                        pltpu.VMEM((ts, w), F32), pltpu.VMEM((1, w), F32)],
        compiler_params=_cparams(("arbitrary", "arbitrary")),
        name="mixer_b",
    )(zb.reshape(batch, seq, 2 * w), conv_w, conv_b.reshape(1, w), wax, bax, lru_lam.reshape(1, w))
    return y.reshape(batch * seq, w)


HEAD_PAIR = LANE // HEAD_DIM


def _mixer_c_prologue(q_ref, k_ref, qm_scr, sel_scr, *, nblk):
    blk = MOBA_BLOCK
    seq = nblk * blk
    kmean = jnp.concatenate(
        [jnp.sum(k_ref[0, n * blk:(n + 1) * blk, :].astype(F32), axis=0, keepdims=True) for n in range(nblk)],
        axis=0) * (1.0 / blk)
    km_hi = kmean.astype(BF16)
    km_lo = (kmean - km_hi.astype(F32)).astype(BF16)
    q = q_ref[0]
    lane = lax.broadcasted_iota(jnp.int32, q.shape, 1)
    blk_id = lax.broadcasted_iota(jnp.int32, (nblk, seq), 0)
    q_blk = lax.broadcasted_iota(jnp.int32, (nblk, seq), 1) // blk
    past = blk_id < q_blk
    for hh in range(HEAD_PAIR):
        qm = jnp.where((lane >= hh * HEAD_DIM) & (lane < (hh + 1) * HEAD_DIM), q, jnp.zeros_like(q))
        qm_scr[hh] = qm
        gate = (lax.dot_general(km_hi, qm, NT_DIMS, preferred_element_type=F32)
                + lax.dot_general(km_lo, qm, NT_DIMS, preferred_element_type=F32))
        gate = jnp.where(past, gate, NEG_INF)
        sel = jnp.zeros((nblk, seq), F32)
        for _ in range(min(MOBA_TOPK, nblk - 1)):
            top = jnp.max(gate, axis=0, keepdims=True)
            first = jnp.min(jnp.where(gate == top, blk_id, nblk), axis=0, keepdims=True)
            pick = blk_id == first
            sel = jnp.where(pick, 1.0, sel)
            gate = jnp.where(pick, -jnp.inf, gate)
        sel_scr[hh] = jnp.where(past, sel, 0.0)


def _run_round_robin(*gens):
    gens = list(gens)
    while gens:
        gens = [g for g in gens if next(g, StopIteration) is not StopIteration]


def _mixer_c_block(c, k_ref, vt_ref, bias_ref, o_ref, qm_scr, sel_scr, s_scr, m_scr, p_scr, *, ndb):
    blk = MOBA_BLOCK
    par = c % 2
    qcols = slice(c * blk, (c + 1) * blk)
    fold = lambda t: t.reshape(blk // SUBLANES, SUBLANES, blk)
    heads = range(HEAD_PAIR)
    pairs = [(k0, min(k0 + 2, c + 1)) for k0 in range(0, c + 1, 2)]
    picked = lambda hh, kb: sel_scr[hh, kb:kb + 1, qcols] > 0.0
    far_bias = lambda hh: bias_ref[hh, ndb, 0:1, :]

    def logits(hh):
        qm = qm_scr[hh, qcols, :]
        m8 = jnp.full((SUBLANES, blk), NEG_INF, F32)
        for k0, k1 in pairs:
            s = lax.dot_general(k_ref[0, k0 * blk:k1 * blk, :], qm, NT_DIMS, preferred_element_type=F32)
            for kb in range(k0, k1):
                su = s[(kb - k0) * blk:(kb - k0 + 1) * blk]
                if c - kb < ndb:
                    su = su + bias_ref[hh, c - kb]
                    t8 = jnp.max(fold(su), axis=0)
                else:
                    t8 = jnp.max(fold(su), axis=0) + far_bias(hh)
                s_scr[par, hh, kb * blk:(kb + 1) * blk, :] = su
                m8 = jnp.maximum(m8, t8 if kb == c else jnp.where(picked(hh, kb), t8, NEG_INF))
            yield
        m_scr[par, hh] = jnp.broadcast_to(jnp.max(m8, axis=0, keepdims=True), (SUBLANES, blk))

    outs = {}

    def attend(hh):
        m = m_scr[par, hh, 0:1, :]
        pv = jnp.zeros((V_ROWS, blk), F32)
        for k0, k1 in pairs:
            for kb in range(k0, k1):
                mk = m if kb == c else jnp.where(picked(hh, kb), m, -NEG_INF)
                if c - kb >= ndb:
                    mk = mk - far_bias(hh)
                x = s_scr[par, hh, kb * blk:(kb + 1) * blk, :] - mk
                p_scr[hh, kb * blk:(kb + 1) * blk, :] = jnp.exp2(x.astype(BF16))
            pv = pv + jnp.dot(vt_ref[0, hh * V_ROWS:(hh + 1) * V_ROWS, k0 * blk:k1 * blk],
                              p_scr[hh, k0 * blk:k1 * blk, :], preferred_element_type=F32)
            yield
        outs[hh] = pv[:HEAD_DIM] / pv[HEAD_DIM:HEAD_DIM + 1]
        if len(outs) == HEAD_PAIR:
            o_ref[0, qcols, :] = jnp.concatenate([outs[h] for h in heads], axis=0).T.astype(o_ref.dtype)

    return [logits(hh) for hh in heads], [attend(hh) for hh in heads]


def _mixer_c_kernel(q_ref, k_ref, vt_ref, bias_ref, o_ref, qm_scr, sel_scr, s_scr, m_scr, p_scr, *, nblk, ndb):
    _mixer_c_prologue(q_ref, k_ref, qm_scr, sel_scr, nblk=nblk)
    stages = functools.partial(_mixer_c_block, k_ref=k_ref, vt_ref=vt_ref, bias_ref=bias_ref, o_ref=o_ref,
                               qm_scr=qm_scr, sel_scr=sel_scr, s_scr=s_scr, m_scr=m_scr, p_scr=p_scr, ndb=ndb)

    def step(t):
        gens = []
        if t < nblk:
            gens += stages(t)[0]
        if t >= 1:
            gens += stages(t - 1)[1]
        _run_round_robin(*gens)

    def body(i, carry):
        for t in range(nblk + 1):
            pl.when(i == t)(functools.partial(step, t))
        return carry

    lax.fori_loop(0, nblk + 1, body, 0)


def _mixer_c(zc, vt, bias_c, batch, seq, ndb):
    blk = MOBA_BLOCK
    assert seq % blk == 0
    nblk = seq // blk
    npair = C_HEADS // HEAD_PAIR
    z = zc.reshape(batch, seq, 2 * C_WIDTH)
    o = pl.pallas_call(
        functools.partial(_mixer_c_kernel, nblk=nblk, ndb=ndb),
        grid=(batch, npair),
        in_specs=[pl.BlockSpec((1, seq, LANE), lambda b, hp: (b, 0, hp)),
                  pl.BlockSpec((1, seq, LANE), lambda b, hp: (b, 0, npair + hp)),
                  pl.BlockSpec((1, HEAD_PAIR * V_ROWS, seq), lambda b, hp: (b, hp, 0)),
                  pl.BlockSpec((HEAD_PAIR, ndb + 1, blk, blk), lambda b, hp: (hp, 0, 0, 0))],
        out_specs=pl.BlockSpec((1, seq, LANE), lambda b, hp: (b, 0, hp)),
        out_shape=jax.ShapeDtypeStruct((batch, seq, C_WIDTH), BF16),
        scratch_shapes=[pltpu.VMEM((HEAD_PAIR, seq, LANE), BF16), pltpu.VMEM((HEAD_PAIR, nblk, seq), F32),
                        pltpu.VMEM((2, HEAD_PAIR, seq, blk), F32), pltpu.VMEM((2, HEAD_PAIR, SUBLANES, blk), F32),
                        pltpu.VMEM((HEAD_PAIR, seq, blk), BF16)],
        compiler_params=_cparams(("arbitrary", "arbitrary")),
        name="mixer_c",
    )(z, z, vt, bias_c)
    return o.reshape(batch * seq, C_WIDTH)


def _merge_kernel(x_ref, o1_ref, o2_ref, o3_ref, l1_ref, l2_ref, l3_ref, ob_ref, oc_ref, zg_ref,
                  pa_ref, pb_ref, pc_ref, wo_ref, out_ref):
    d = D_MODEL
    halves = lambda ref: jnp.concatenate([ref[c] for c in range(ref.shape[0])], axis=-1)
    l1, l2, l3 = halves(l1_ref), halves(l2_ref), halves(l3_ref)
    m = jnp.maximum(jnp.maximum(l1, l2), l3)
    e1, e2, e3 = jnp.exp(l1 - m), jnp.exp(l2 - m), jnp.exp(l3 - m)
    o_a = (e1 * halves(o1_ref) + e2 * halves(o2_ref) + e3 * halves(o3_ref)) / (e1 + e2 + e3)

    def branch(o, p_ref, k):
        gate = jax.nn.sigmoid(zg_ref[:, k * d:(k + 1) * d].astype(F32))
        return gate * jnp.dot(o, p_ref[...], preferred_element_type=F32)

    merged = branch(o_a.astype(BF16), pa_ref, 0) + branch(ob_ref[...], pb_ref, 1) + branch(oc_ref[...], pc_ref, 2)
    out_ref[...] = x_ref[...] + jnp.dot(merged.astype(BF16), wo_ref[...], preferred_element_type=F32)


def _merge(x2d, oa, lse, ob, oc, zg, p_a, p_b, p_c, w_out, layer, tm=512):
    n = x2d.shape[0]
    row = lambda width: pl.BlockSpec((tm, width), lambda i: (i, 0))
    return pl.pallas_call(
        _merge_kernel,
        grid=(n // tm,),
        in_specs=[row(D_MODEL)] + [pl.BlockSpec((A_OUT // LANE, tm, LANE), lambda i: (0, i, 0))] * 6
                 + [row(LRU_WIDTH), row(C_WIDTH), row(3 * D_MODEL),
                  _layer_spec(p_a, layer), _layer_spec(p_b, layer), _layer_spec(p_c, layer),
                  _layer_spec(w_out, layer)],
        out_specs=row(D_MODEL),
        out_shape=jax.ShapeDtypeStruct((n, D_MODEL), F32),
        compiler_params=_cparams(("arbitrary",)),
        name="merge",
    )(x2d, *oa, *lse, ob, oc, zg, p_a, p_b, p_c, w_out)


def _rms(x, g):
    return x * lax.rsqrt(jnp.mean(x * x, axis=-1, keepdims=True) + EPS) * g


def _ffn_kernel(x_ref, g_ref, wgu_ref, wd_ref, gf_ref, out_ref, h_scr, act_scr, *, final_norm):
    x = x_ref[...]
    h_scr[...] = _rms(x, g_ref[...]).astype(BF16)
    for off, w in _col_chunks(0, FFN_HIDDEN, 256):
        gate = jnp.dot(h_scr[...], wgu_ref[:, off:off + w], preferred_element_type=F32)
        up = jnp.dot(h_scr[...], wgu_ref[:, FFN_HIDDEN + off:FFN_HIDDEN + off + w], preferred_element_type=F32)
        act_scr[:, off:off + w] = (gate * jax.nn.sigmoid(gate) * up).astype(BF16)
    y = x + jnp.dot(act_scr[...], wd_ref[...], preferred_element_type=F32)
    if final_norm:
        y = _rms(y, gf_ref[...])
    out_ref[...] = y


def _ffn(x2d, g_ffn, w_gu, w_down, layer, g_final, final_norm, tm=512):
    n = x2d.shape[0]
    row = pl.BlockSpec((tm, D_MODEL), lambda i: (i, 0))
    return pl.pallas_call(
        functools.partial(_ffn_kernel, final_norm=final_norm),
        grid=(n // tm,),
        in_specs=[row, _const_spec((1, D_MODEL)), _layer_spec(w_gu, layer), _layer_spec(w_down, layer),
                  _const_spec((1, D_MODEL))],
        out_specs=row,
        out_shape=jax.ShapeDtypeStruct((n, D_MODEL), F32),
        scratch_shapes=[pltpu.VMEM((tm, D_MODEL), BF16), pltpu.VMEM((tm, FFN_HIDDEN), BF16)],
        compiler_params=_cparams(("arbitrary",)),
        name="ffn",
    )(x2d, g_ffn.reshape(1, D_MODEL), w_gu, w_down, g_final.reshape(1, D_MODEL))


def kernel(x, rel_bias, g_mix, w_in, conv_w, conv_b, lru_wa, lru_ba, lru_wx, lru_bx, lru_lam,
           p_a, p_b, p_c, w_out, g_ffn, w_gu, w_down, g_final):
    batch, seq, d = x.shape
    assert d == D_MODEL and w_in.shape[-1] == IN_COLS
    depth = w_in.shape[0]
    nblk = seq // MOBA_BLOCK
    ndb = _moba_far_blocks(nblk)

    bias_a = _build_bias(rel_bias, _bucket_index_a(), A_HEADS, 0, A_HPG)
    bias_c = _build_bias(rel_bias, _bucket_index_c(ndb), C_HEADS, A_HEADS, None, LOG2E)

    w_in, p_a, p_b, p_c, w_out, w_gu, w_down = (
        t.astype(BF16) for t in (w_in, p_a, p_b, p_c, w_out, w_gu, w_down))

    x2d = x.reshape(batch * seq, d)
    for l in range(depth):
        *za, zb, zc, vt, zg = _in_proj(x2d, g_mix[l], w_in, l, batch, seq)
        oa, lse = zip(*[_mixer_a_group(za[g], bias_a, g, dil, batch, seq)
                        for g, (_, dil) in enumerate(A_GROUPS)])
        ob = _mixer_b(zb, conv_w[l], conv_b[l], lru_wa[l], lru_ba[l], lru_wx[l], lru_bx[l], lru_lam[l],
                      batch, seq)
        oc = _mixer_c(zc, vt, bias_c, batch, seq, ndb)
        x2d = _merge(x2d, oa, lse, ob, oc, zg, p_a, p_b, p_c, w_out, l)
        x2d = _ffn(x2d, g_ffn[l], w_gu, w_down, l, g_final, final_norm=(l == depth - 1))
    return x2d.reshape(batch, seq, d)
```

```python
import functools
import math

import numpy as np
import jax
import jax.numpy as jnp
from jax import lax
from jax.experimental import pallas as pl
from jax.experimental.pallas import tpu as pltpu

F32 = jnp.float32
BF16 = jnp.bfloat16

D_MODEL = 1024
HEAD_DIM = 64
SCALE = HEAD_DIM ** -0.5
LOG2E = math.log2(math.e)
NEG_INF = -1e30
EPS = 1e-6
A_GROUPS = ((128, 1), (512, 4), (2048, 16))
A_HPG = 4
A_HEADS = A_HPG * len(A_GROUPS)
A_WIDTH = A_HEADS * HEAD_DIM
A_OUT = A_HPG * HEAD_DIM
A_BLOCK = 128
LRU_WIDTH = D_MODEL // 2
LRU_BLOCKS = 8
CONV_WIDTH = 4
LRU_C = 8.0
C_HEADS = 8
C_WIDTH = C_HEADS * HEAD_DIM
MOBA_BLOCK = 256
MOBA_TOPK = 3
REL_BUCKETS = 32
REL_MAX_DIST = 2048
FFN_HIDDEN = 2816
IN_COLS = 3 * A_WIDTH + 2 * LRU_WIDTH + 3 * C_WIDTH + 3 * D_MODEL
V_C_OFF = 3 * A_WIDTH + 2 * LRU_WIDTH + 2 * C_WIDTH

LANE = 128
V_ROWS = HEAD_DIM + 16
VMEM_LIMIT = 56 * 1024 * 1024

NT_DIMS = (((1,), (1,)), ((), ()))


def _cparams(sem, vmem=VMEM_LIMIT):
    return pltpu.CompilerParams(dimension_semantics=sem, vmem_limit_bytes=vmem)


def _const_spec(shape):
    nd = len(shape)
    return pl.BlockSpec(shape, lambda *_: (0,) * nd, pipeline_mode=pl.Buffered(1))


def _layer_spec(stacked, layer):
    return pl.BlockSpec((None,) + stacked.shape[1:], lambda *_: (layer, 0, 0), pipeline_mode=pl.Buffered(1))


def _rel_bucket_np(dist):
    max_exact = REL_BUCKETS // 2
    d = np.maximum(dist, 0)
    df = np.maximum(d, 1).astype(np.float32)
    large = max_exact + (np.log(df / np.float32(max_exact)) / np.float32(math.log(REL_MAX_DIST / max_exact))
                         * np.float32(REL_BUCKETS - max_exact)).astype(np.int32)
    large = np.minimum(large, REL_BUCKETS - 1)
    return np.where(d < max_exact, d, large).astype(np.int32)


def _bucket_index_a():
    qi = np.arange(A_BLOCK)[None, :] + A_BLOCK
    kj = np.arange(2 * A_BLOCK)[:, None]
    delta = qi - kj
    mats = []
    for window, dil in A_GROUPS:
        band = (delta >= 0) & (delta <= window // dil)
        mats.append(np.where(band, _rel_bucket_np(delta * dil), -1))
    return np.stack(mats).astype(np.int32)


def _moba_far_blocks(nblk):
    for db in range(1, nblk + 1):
        lo = db * MOBA_BLOCK - (MOBA_BLOCK - 1)
        if np.all(_rel_bucket_np(np.arange(lo, nblk * MOBA_BLOCK)) == REL_BUCKETS - 1):
            return db
    return nblk


def _bucket_index_c(ndb):
    k = np.arange(MOBA_BLOCK)[:, None]
    q = np.arange(MOBA_BLOCK)[None, :]
    mats = []
    for db in range(ndb + 1):
        dist = db * MOBA_BLOCK + q - k
        b = _rel_bucket_np(dist)
        if db == 0:
            b = np.where(dist >= 0, b, -1)
        mats.append(b)
    return np.stack(mats).astype(np.int32)


def _bias_kernel(tab_ref, idx_ref, out_ref, *, head_off, unit, present):
    h = pl.program_id(0) + head_off
    for m, buckets in enumerate(present):
        idx = idx_ref[m]
        out = jnp.full(idx.shape, NEG_INF, F32)
        for b in buckets:
            out = jnp.where(idx == b, tab_ref[b, h] * unit, out)
        out_ref[0, m] = out


def _build_bias(rel_bias, idx, n_heads, head_off, heads_per_idx_group, unit=1.0):
    r, c = idx.shape[-2:]
    if heads_per_idx_group is None:
        n_mats = idx.shape[0]
        idx_map = lambda h: (0, 0, 0)
        present = tuple(tuple(int(b) for b in np.unique(m[m >= 0])) for m in idx)
    else:
        n_mats = 1
        idx_map = lambda h: (h // heads_per_idx_group, 0, 0)
        present = (tuple(int(b) for b in np.unique(idx[idx >= 0])),)
    return pl.pallas_call(
        functools.partial(_bias_kernel, head_off=head_off, unit=unit, present=present),
        grid=(n_heads,),
        in_specs=[pl.BlockSpec(memory_space=pltpu.SMEM),
                  pl.BlockSpec((n_mats, r, c), idx_map)],
        out_specs=pl.BlockSpec((1, n_mats, r, c), lambda h: (h, 0, 0, 0)),
        out_shape=jax.ShapeDtypeStruct((n_heads, n_mats, r, c), F32),
        compiler_params=_cparams(("arbitrary",)),
        name="rel_bias_build",
    )(rel_bias, jnp.asarray(idx))


def _col_chunks(start, stop, width=512):
    out = []
    while start < stop:
        w = min(width, stop - start)
        out.append((start, w))
        start += w
    return out


def _in_proj_kernel(x_ref, g_ref, w_ref, wvt_ref, za0_ref, za1_ref, za2_ref, zb_ref, zc_ref, vt_ref, zg_ref,
                    h_scr, dil_scr, *, tm):
    x = x_ref[...]
    ms = jnp.mean(x * x, axis=-1, keepdims=True)
    h_scr[...] = (x * lax.rsqrt(ms + EPS) * g_ref[...]).astype(BF16)

    def proj(base, width):
        return jnp.dot(h_scr[...], w_ref[:, base:base + width], preferred_element_type=F32)

    def seg(out_ref, base, width, scaled_cols=0, scale=1.0, chunk=512):
        assert scaled_cols % chunk == 0
        for off, w in _col_chunks(0, width, chunk):
            r = proj(base + off, w)
            if off < scaled_cols:
                r = r * scale
            out_ref[:, off:off + w] = r.astype(out_ref.dtype)

    for g, (za_ref, (_, dil)) in enumerate(zip((za0_ref, za1_ref, za2_ref), A_GROUPS)):
        qkv = [proj(s * A_WIDTH + g * A_OUT, A_OUT) for s in range(3)]
        qkv[0] = qkv[0] * SCALE
        if dil == 1:
            for s, part in enumerate(qkv):
                za_ref[0, 0, :, s * A_OUT:(s + 1) * A_OUT] = part.astype(BF16)
        else:
            planes = [part[:, c * LANE:(c + 1) * LANE] for part in qkv for c in range(A_OUT // LANE)]
            for c, plane in enumerate(planes):
                dil_scr[c] = plane
            for r in range(dil):
                for c in range(len(planes)):
                    za_ref[0, r, :, c * LANE:(c + 1) * LANE] = (
                        dil_scr[c, pl.ds(r, tm // dil, stride=dil), :].astype(BF16))

    wa, wb, wc = 3 * A_WIDTH, 2 * LRU_WIDTH, 2 * C_WIDTH
    seg(zb_ref, wa, wb)
    seg(zc_ref, wa + wb, wc, C_WIDTH, SCALE * LOG2E)
    seg(zg_ref, V_C_OFF + C_WIDTH, 3 * D_MODEL)
    vt = lax.dot_general(wvt_ref[...], h_scr[...], NT_DIMS, preferred_element_type=F32).astype(BF16)
    for h in range(C_HEADS):
        vt_ref[0, h * V_ROWS:h * V_ROWS + HEAD_DIM, :] = vt[h * HEAD_DIM:(h + 1) * HEAD_DIM]
        vt_ref[0, h * V_ROWS + HEAD_DIM:(h + 1) * V_ROWS, :] = jnp.ones((V_ROWS - HEAD_DIM, tm), BF16)


def _in_proj(x2d, g, w_in, layer, batch, seq, tm=512):
    n = x2d.shape[0]
    wvt = w_in[layer, :, V_C_OFF:V_C_OFF + C_WIDTH].T
    wb, wc, wg = 2 * LRU_WIDTH, 2 * C_WIDTH, 3 * D_MODEL
    gw = 3 * A_OUT
    tiles_per_seq = seq // tm
    row = lambda i: (i, 0)
    seq_tile = lambda i: (i // tiles_per_seq, 0, i % tiles_per_seq, 0)
    za_specs = [pl.BlockSpec((1, dil, tm // dil, gw), seq_tile) for _, dil in A_GROUPS]
    za_shapes = [jax.ShapeDtypeStruct((batch, dil, seq // dil, gw), BF16) for _, dil in A_GROUPS]
    return pl.pallas_call(
        functools.partial(_in_proj_kernel, tm=tm),
        grid=(n // tm,),
        in_specs=[pl.BlockSpec((tm, D_MODEL), row),
                  _const_spec((1, D_MODEL)),
                  _layer_spec(w_in, layer),
                  _const_spec(wvt.shape)],
        out_specs=za_specs + [pl.BlockSpec((tm, wb), row),
                              pl.BlockSpec((tm, wc), row),
                              pl.BlockSpec((1, C_HEADS * V_ROWS, tm),
                                           lambda i: (i // tiles_per_seq, 0, i % tiles_per_seq)),
                              pl.BlockSpec((tm, wg), row)],
        out_shape=za_shapes + [jax.ShapeDtypeStruct((n, wb), BF16),
                               jax.ShapeDtypeStruct((n, wc), BF16),
                               jax.ShapeDtypeStruct((batch, C_HEADS * V_ROWS, seq), BF16),
                               jax.ShapeDtypeStruct((n, wg), BF16)],
        scratch_shapes=[pltpu.VMEM((tm, D_MODEL), BF16), pltpu.VMEM((gw // LANE, tm, LANE), F32)],
        compiler_params=_cparams(("arbitrary",)),
        name="in_proj",
    )(x2d, g.reshape(1, D_MODEL), w_in, wvt)


A_TOKENS_PER_STEP = 2048
A_CHAINS = 4


def _mixer_a_kernel(z_ref, zp_ref, bias_ref, o_ref, lse_ref, *, dil, nsub):
    t = pl.program_id(1)
    blk, width = A_BLOCK, A_OUT
    kcols, vcols = slice(width, 2 * width), slice(2 * width, 3 * width)
    lane_head = lax.broadcasted_iota(jnp.int32, (blk, width), 1) // HEAD_DIM
    in_prev = (lax.broadcasted_iota(jnp.int32, (2 * blk, 2 * blk), 0) < blk).astype(F32)
    no_prev = in_prev * jnp.where(t == 0, NEG_INF, 0.0)
    transpose_v = lambda v: v.astype(F32).T.astype(BF16)

    ones_rows = jnp.ones((V_ROWS - HEAD_DIM, 2 * blk), BF16)
    blocks, windows = {}, {}

    def block_kv(r, j):
        if (r, j) not in blocks:
            src = zp_ref[0, r] if j < 0 else z_ref[0, r, j * blk:(j + 1) * blk, :]
            blocks[(r, j)] = (src[:, kcols], transpose_v(src[:, vcols]))
        return blocks[(r, j)]

    def window(r, j):
        if (r, j) not in windows:
            (k_prev, vt_prev), (k_cur, vt_cur) = block_kv(r, j - 1), block_kv(r, j)
            k_win = jnp.concatenate([k_prev, k_cur], axis=0)
            vt_win = jnp.concatenate([vt_prev, vt_cur], axis=1)
            windows[(r, j)] = (z_ref[0, r, j * blk:(j + 1) * blk, 0:width], k_win, vt_win)
        return windows[(r, j)]

    def logits(r, j, hp):
        q, k_win, _ = window(r, j)
        heads = (2 * hp, 2 * hp + 1)
        q2 = jnp.concatenate([jnp.where(lane_head == h, q, jnp.zeros_like(q)) for h in heads], axis=0)
        bias2 = jnp.concatenate([bias_ref[h, 0] for h in heads], axis=1)
        s = lax.dot_general(k_win, q2, NT_DIMS, preferred_element_type=F32) + bias2
        return s + no_prev if j == 0 else s

    def attend(r, j, hp, s):
        vt_win = window(r, j)[2]
        m = jnp.max(s, axis=0, keepdims=True)
        p = jnp.exp((s - m).astype(BF16))
        vt_ext = jnp.concatenate([part for h in (2 * hp, 2 * hp + 1)
                                  for part in (vt_win[h * HEAD_DIM:(h + 1) * HEAD_DIM, :], ones_rows)], axis=0)
        pv = jnp.dot(vt_ext, p, preferred_element_type=F32)
        res = []
        for u in range(2):
            cols = slice(u * blk, (u + 1) * blk)
            den = pv[u * V_ROWS + HEAD_DIM:u * V_ROWS + HEAD_DIM + 1, cols]
            res.append((pv[u * V_ROWS:u * V_ROWS + HEAD_DIM, cols] / den,
                        jnp.broadcast_to(m[:, cols] + jnp.log(den), (HEAD_DIM, blk))))
        return res

    n_hp = A_HPG // 2

    def chain(units):
        s_next = logits(*units[0])
        outs, lses = [], []
        for i, (r, j, hp) in enumerate(units):
            s_cur = s_next
            if i + 1 < len(units):
                s_next = logits(*units[i + 1])
            for o, lse in attend(r, j, hp, s_cur):
                outs.append(o)
                lses.append(lse)
            if hp == n_hp - 1:
                o_t = jnp.concatenate(outs, axis=0).T
                lse_t = jnp.concatenate(lses, axis=0).T
                outs, lses = [], []
                dst = slice(j * blk, (j + 1) * blk) if dil == 1 else pl.ds(j * blk * dil + r, blk, stride=dil)
                for c in range(width // LANE):
                    o_ref[c, dst, :] = o_t[:, c * LANE:(c + 1) * LANE]
                    lse_ref[c, dst, :] = lse_t[:, c * LANE:(c + 1) * LANE]
            yield

    units = [(r, j, hp) for r in range(dil) for j in range(nsub) for hp in range(n_hp)]
    per_chain = len(units) // A_CHAINS
    _run_round_robin(*[chain(units[i * per_chain:(i + 1) * per_chain]) for i in range(A_CHAINS)])


def _mixer_a_group(za, bias_a, g, dil, batch, seq):
    sub_len = seq // dil
    assert sub_len % A_BLOCK == 0
    tq = min(sub_len, A_TOKENS_PER_STEP // dil)
    assert tq % A_BLOCK == 0
    nsub = tq // A_BLOCK
    gw = 3 * A_OUT
    out_spec = pl.BlockSpec((A_OUT // LANE, tq * dil, LANE), lambda b, t: (0, b * (sub_len // tq) + t, 0))
    return pl.pallas_call(
        functools.partial(_mixer_a_kernel, dil=dil, nsub=nsub),
        grid=(batch, sub_len // tq),
        in_specs=[pl.BlockSpec((1, dil, tq, gw), lambda b, t: (b, 0, t, 0)),
                  pl.BlockSpec((1, dil, A_BLOCK, gw), lambda b, t: (b, 0, jnp.maximum(t * nsub - 1, 0), 0)),
                  pl.BlockSpec((A_HPG, 1, 2 * A_BLOCK, A_BLOCK), lambda b, t: (g, 0, 0, 0))],
        out_specs=[out_spec, out_spec],
        out_shape=[jax.ShapeDtypeStruct((A_OUT // LANE, batch * seq, LANE), F32)] * 2,
        compiler_params=_cparams(("arbitrary", "arbitrary")),
        name=f"mixer_a_d{dil}",
    )(za, za, bias_a)


SUBLANES = 8


def _mixer_b_kernel(z_ref, cw_ref, cb_ref, wax_ref, bax_ref, lam_ref, y_ref,
                    nat_scr, xe_scr, a_scr, b_scr, tail_scr, h_scr, *, ts):
    w = LRU_WIDTH
    seg = ts // SUBLANES
    halo = CONV_WIDTH - 1
    planes = w // LANE
    vrow = lambda i: slice(i * SUBLANES, (i + 1) * SUBLANES)
    sub = lax.broadcasted_iota(jnp.int32, (SUBLANES, w), 0)

    @pl.when(pl.program_id(1) == 0)
    def _():
        tail_scr[...] = jnp.zeros((halo * SUBLANES, w), F32)
        h_scr[...] = jnp.zeros((1, w), F32)

    x = z_ref[0, :, 0:w].astype(F32)
    for c in range(planes):
        nat_scr[c] = x[:, c * LANE:(c + 1) * LANE]
    for i in range(seg):
        for c in range(planes):
            xe_scr[vrow(halo + i), c * LANE:(c + 1) * LANE] = nat_scr[c, pl.ds(i, SUBLANES, stride=seg), :]
    for k in range(1, halo + 1):
        cur = xe_scr[vrow(halo + seg - k), :]
        prev = tail_scr[vrow(halo - k), :]
        xe_scr[vrow(halo - k), :] = jnp.where(sub == 0, pltpu.roll(prev, 1, 0), pltpu.roll(cur, 1, 0))
        tail_scr[vrow(halo - k), :] = cur
    xc = cb_ref[...]
    for j in range(CONV_WIDTH):
        start = (halo - (CONV_WIDTH - 1 - j)) * SUBLANES
        xc = xc + xe_scr[start:start + ts, :] * cw_ref[j:j + 1, :]

    ra = jnp.dot(xc.astype(BF16), wax_ref[...], preferred_element_type=F32) + bax_ref[...]
    r = jax.nn.sigmoid(ra[:, :w])
    ig = jax.nn.sigmoid(ra[:, w:])
    nl = -lam_ref[...]
    softplus = jnp.maximum(nl, 0.0) + jnp.log1p(jnp.exp(-jnp.abs(nl)))
    log_a = -LRU_C * r * softplus
    a = jnp.exp(log_a)
    a_scr[...] = a
    b_scr[...] = jnp.sqrt(-jnp.tanh(log_a) * (a * a + 1.0)) * (ig * xc)

    h = jnp.zeros((SUBLANES, w), F32)
    p = jnp.ones((SUBLANES, w), F32)
    for i in range(seg):
        a_i = a_scr[vrow(i), :]
        h = a_i * h + b_scr[vrow(i), :]
        p = a_i * p
        b_scr[vrow(i), :] = h
        a_scr[vrow(i), :] = p
    for s in (1, 2, 4):
        keep = sub >= s
        p_sh = jnp.where(keep, pltpu.roll(p, s, 0), 1.0)
        h_sh = jnp.where(keep, pltpu.roll(h, s, 0), 0.0)
        h = p * h_sh + h
        p = p * p_sh
    h_in = h_scr[...]
    after = p * h_in + h
    carry = jnp.where(sub == 0, h_in, pltpu.roll(after, 1, 0))
    h_scr[...] = after[SUBLANES - 1:SUBLANES, :]
    for i in range(seg):
        hs = b_scr[vrow(i), :] + a_scr[vrow(i), :] * carry
        for c in range(planes):
            nat_scr[c, pl.ds(i, SUBLANES, stride=seg), :] = hs[:, c * LANE:(c + 1) * LANE]
    h_nat = jnp.concatenate([nat_scr[c] for c in range(planes)], axis=1)

    gb = z_ref[0, :, w:2 * w].astype(F32)
    cdf = 0.5 * (1.0 + jnp.tanh(math.sqrt(2.0 / math.pi) * (gb + 0.044715 * (gb * gb * gb))))
    y_ref[0] = (h_nat * (gb * cdf)).astype(y_ref.dtype)


def _block_diag(wblocks):
    nb, di, do = wblocks.shape
    eye = jnp.eye(nb, dtype=wblocks.dtype)
    return (eye[:, None, :, None] * wblocks[:, :, None, :]).reshape(nb * di, nb * do)


def _mixer_b(zb, conv_w, conv_b, lru_wa, lru_ba, lru_wx, lru_bx, lru_lam, batch, seq, ts=512):
    w = LRU_WIDTH
    wax = jnp.concatenate([_block_diag(lru_wa), _block_diag(lru_wx)], axis=1).astype(BF16)
    bax = jnp.concatenate([lru_ba, lru_bx]).reshape(1, 2 * w)
    y = pl.pallas_call(
        functools.partial(_mixer_b_kernel, ts=ts),
        grid=(batch, seq // ts),
        in_specs=[pl.BlockSpec((1, ts, 2 * w), lambda b, t: (b, t, 0)),
                  _const_spec((CONV_WIDTH, w)), _const_spec((1, w)),
                  _const_spec((w, 2 * w)), _const_spec((1, 2 * w)), _const_spec((1, w))],
        out_specs=pl.BlockSpec((1, ts, w), lambda b, t: (b, t, 0)),
        out_shape=jax.ShapeDtypeStruct((batch, seq, w), BF16),
        scratch_shapes=[pltpu.VMEM((w // LANE, ts, LANE), F32),
                        pltpu.VMEM((ts + (CONV_WIDTH - 1) * SUBLANES, w), F32),
                        pltpu.VMEM((ts, w), F32), pltpu.VMEM((ts, w), F32),
                        pltpu.VMEM(((CONV_WIDTH - 1) * SUBLANES, w), F32), pltpu.VMEM((1, w), F32)],
        compiler_params=_cparams(("arbitrary", "arbitrary")),
        name="mixer_b",
    )(zb.reshape(batch, seq, 2 * w), conv_w, conv_b.reshape(1, w), wax, bax, lru_lam.reshape(1, w))
    return y.reshape(batch * seq, w)


HEAD_PAIR = LANE // HEAD_DIM


def _mixer_c_prologue(q_ref, k_ref, qm_scr, sel_scr, *, nblk):
    blk = MOBA_BLOCK
    seq = nblk * blk
    kmean = jnp.concatenate(
        [jnp.sum(k_ref[0, n * blk:(n + 1) * blk, :].astype(F32), axis=0, keepdims=True) for n in range(nblk)],
        axis=0) * (1.0 / blk)
    km_hi = kmean.astype(BF16)
    km_lo = (kmean - km_hi.astype(F32)).astype(BF16)
    q = q_ref[0]
    lane = lax.broadcasted_iota(jnp.int32, q.shape, 1)
    blk_id = lax.broadcasted_iota(jnp.int32, (nblk, seq), 0)
    q_blk = lax.broadcasted_iota(jnp.int32, (nblk, seq), 1) // blk
    past = blk_id < q_blk
    for hh in range(HEAD_PAIR):
        qm = jnp.where((lane >= hh * HEAD_DIM) & (lane < (hh + 1) * HEAD_DIM), q, jnp.zeros_like(q))
        qm_scr[hh] = qm
        gate = (lax.dot_general(km_hi, qm, NT_DIMS, preferred_element_type=F32)
                + lax.dot_general(km_lo, qm, NT_DIMS, preferred_element_type=F32))
        gate = jnp.where(past, gate, NEG_INF)
        sel = jnp.zeros((nblk, seq), F32)
        for _ in range(min(MOBA_TOPK, nblk - 1)):
            top = jnp.max(gate, axis=0, keepdims=True)
            first = jnp.min(jnp.where(gate == top, blk_id, nblk), axis=0, keepdims=True)
            pick = blk_id == first
            sel = jnp.where(pick, 1.0, sel)
            gate = jnp.where(pick, -jnp.inf, gate)
        sel_scr[hh] = jnp.where(past, sel, 0.0)


def _run_round_robin(*gens):
    gens = list(gens)
    while gens:
        gens = [g for g in gens if next(g, StopIteration) is not StopIteration]


def _mixer_c_block(c, k_ref, vt_ref, bias_ref, o_ref, qm_scr, sel_scr, s_scr, m_scr, p_scr, *, ndb):
    blk = MOBA_BLOCK
    par = c % 2
    qcols = slice(c * blk, (c + 1) * blk)
    fold = lambda t: t.reshape(blk // SUBLANES, SUBLANES, blk)
    heads = range(HEAD_PAIR)
    pairs = [(k0, min(k0 + 2, c + 1)) for k0 in range(0, c + 1, 2)]
    picked = lambda hh, kb: sel_scr[hh, kb:kb + 1, qcols] > 0.0
    far_bias = lambda hh: bias_ref[hh, ndb, 0:1, :]

    def logits(hh):
        qm = qm_scr[hh, qcols, :]
        m8 = jnp.full((SUBLANES, blk), NEG_INF, F32)
        for k0, k1 in pairs:
            s = lax.dot_general(k_ref[0, k0 * blk:k1 * blk, :], qm, NT_DIMS, preferred_element_type=F32)
            for kb in range(k0, k1):
                su = s[(kb - k0) * blk:(kb - k0 + 1) * blk]
                if c - kb < ndb:
                    su = su + bias_ref[hh, c - kb]
                    t8 = jnp.max(fold(su), axis=0)
                else:
                    t8 = jnp.max(fold(su), axis=0) + far_bias(hh)
                s_scr[par, hh, kb * blk:(kb + 1) * blk, :] = su
                m8 = jnp.maximum(m8, t8 if kb == c else jnp.where(picked(hh, kb), t8, NEG_INF))
            yield
        m_scr[par, hh] = jnp.broadcast_to(jnp.max(m8, axis=0, keepdims=True), (SUBLANES, blk))

    outs = {}

    def attend(hh):
        m = m_scr[par, hh, 0:1, :]
        pv = jnp.zeros((V_ROWS, blk), F32)
        for k0, k1 in pairs:
            for kb in range(k0, k1):
                mk = m if kb == c else jnp.where(picked(hh, kb), m, -NEG_INF)
                if c - kb >= ndb:
                    mk = mk - far_bias(hh)
                x = s_scr[par, hh, kb * blk:(kb + 1) * blk, :] - mk
                p_scr[hh, kb * blk:(kb + 1) * blk, :] = jnp.exp2(x.astype(BF16))
            pv = pv + jnp.dot(vt_ref[0, hh * V_ROWS:(hh + 1) * V_ROWS, k0 * blk:k1 * blk],
                              p_scr[hh, k0 * blk:k1 * blk, :], preferred_element_type=F32)
            yield
        outs[hh] = pv[:HEAD_DIM] / pv[HEAD_DIM:HEAD_DIM + 1]
        if len(outs) == HEAD_PAIR:
            o_ref[0, qcols, :] = jnp.concatenate([outs[h] for h in heads], axis=0).T.astype(o_ref.dtype)

    return [logits(hh) for hh in heads], [attend(hh) for hh in heads]


def _mixer_c_kernel(q_ref, k_ref, vt_ref, bias_ref, o_ref, qm_scr, sel_scr, s_scr, m_scr, p_scr, *, nblk, ndb):
    _mixer_c_prologue(q_ref, k_ref, qm_scr, sel_scr, nblk=nblk)
    stages = functools.partial(_mixer_c_block, k_ref=k_ref, vt_ref=vt_ref, bias_ref=bias_ref, o_ref=o_ref,
                               qm_scr=qm_scr, sel_scr=sel_scr, s_scr=s_scr, m_scr=m_scr, p_scr=p_scr, ndb=ndb)

    def step(t):
        gens = []
        if t < nblk:
            gens += stages(t)[0]
        if t >= 1:
            gens += stages(t - 1)[1]
        _run_round_robin(*gens)

    def body(i, carry):
        for t in range(nblk + 1):
            pl.when(i == t)(functools.partial(step, t))
        return carry

    lax.fori_loop(0, nblk + 1, body, 0)


def _mixer_c(zc, vt, bias_c, batch, seq, ndb):
    blk = MOBA_BLOCK
    assert seq % blk == 0
    nblk = seq // blk
    npair = C_HEADS // HEAD_PAIR
    z = zc.reshape(batch, seq, 2 * C_WIDTH)
    o = pl.pallas_call(
        functools.partial(_mixer_c_kernel, nblk=nblk, ndb=ndb),
        grid=(batch, npair),
        in_specs=[pl.BlockSpec((1, seq, LANE), lambda b, hp: (b, 0, hp)),
                  pl.BlockSpec((1, seq, LANE), lambda b, hp: (b, 0, npair + hp)),
                  pl.BlockSpec((1, HEAD_PAIR * V_ROWS, seq), lambda b, hp: (b, hp, 0)),
                  pl.BlockSpec((HEAD_PAIR, ndb + 1, blk, blk), lambda b, hp: (hp, 0, 0, 0))],
        out_specs=pl.BlockSpec((1, seq, LANE), lambda b, hp: (b, 0, hp)),
        out_shape=jax.ShapeDtypeStruct((batch, seq, C_WIDTH), BF16),
        scratch_shapes=[pltpu.VMEM((HEAD_PAIR, seq, LANE), BF16), pltpu.VMEM((HEAD_PAIR, nblk, seq), F32),
                        pltpu.VMEM((2, HEAD_PAIR, seq, blk), F32), pltpu.VMEM((2, HEAD_PAIR, SUBLANES, blk), F32),
                        pltpu.VMEM((HEAD_PAIR, seq, blk), BF16)],
        compiler_params=_cparams(("arbitrary", "arbitrary")),
        name="mixer_c",
    )(z, z, vt, bias_c)
    return o.reshape(batch * seq, C_WIDTH)


def _merge_kernel(x_ref, o1_ref, o2_ref, o3_ref, l1_ref, l2_ref, l3_ref, ob_ref, oc_ref, zg_ref,
                  pa_ref, pb_ref, pc_ref, wo_ref, out_ref):
    d = D_MODEL
    halves = lambda ref: jnp.concatenate([ref[c] for c in range(ref.shape[0])], axis=-1)
    l1, l2, l3 = halves(l1_ref), halves(l2_ref), halves(l3_ref)
    m = jnp.maximum(jnp.maximum(l1, l2), l3)
    e1, e2, e3 = jnp.exp(l1 - m), jnp.exp(l2 - m), jnp.exp(l3 - m)
    o_a = (e1 * halves(o1_ref) + e2 * halves(o2_ref) + e3 * halves(o3_ref)) / (e1 + e2 + e3)

    def branch(o, p_ref, k):
        gate = jax.nn.sigmoid(zg_ref[:, k * d:(k + 1) * d].astype(F32))
        return gate * jnp.dot(o, p_ref[...], preferred_element_type=F32)

    merged = branch(o_a.astype(BF16), pa_ref, 0) + branch(ob_ref[...], pb_ref, 1) + branch(oc_ref[...], pc_ref, 2)
    out_ref[...] = x_ref[...] + jnp.dot(merged.astype(BF16), wo_ref[...], preferred_element_type=F32)


def _merge(x2d, oa, lse, ob, oc, zg, p_a, p_b, p_c, w_out, layer, tm=512):
    n = x2d.shape[0]
    row = lambda width: pl.BlockSpec((tm, width), lambda i: (i, 0))
    return pl.pallas_call(
        _merge_kernel,
        grid=(n // tm,),
        in_specs=[row(D_MODEL)] + [pl.BlockSpec((A_OUT // LANE, tm, LANE), lambda i: (0, i, 0))] * 6
                 + [row(LRU_WIDTH), row(C_WIDTH), row(3 * D_MODEL),
                  _layer_spec(p_a, layer), _layer_spec(p_b, layer), _layer_spec(p_c, layer),
                  _layer_spec(w_out, layer)],
        out_specs=row(D_MODEL),
        out_shape=jax.ShapeDtypeStruct((n, D_MODEL), F32),
        compiler_params=_cparams(("arbitrary",)),
        name="merge",
    )(x2d, *oa, *lse, ob, oc, zg, p_a, p_b, p_c, w_out)


def _rms(x, g):
    return x * lax.rsqrt(jnp.mean(x * x, axis=-1, keepdims=True) + EPS) * g


def _ffn_kernel(x_ref, g_ref, wgu_ref, wd_ref, gf_ref, out_ref, h_scr, act_scr, *, final_norm):
    x = x_ref[...]
    h_scr[...] = _rms(x, g_ref[...]).astype(BF16)
    for off, w in _col_chunks(0, FFN_HIDDEN, 256):
        gate = jnp.dot(h_scr[...], wgu_ref[:, off:off + w], preferred_element_type=F32)
        up = jnp.dot(h_scr[...], wgu_ref[:, FFN_HIDDEN + off:FFN_HIDDEN + off + w], preferred_element_type=F32)
        act_scr[:, off:off + w] = (gate * jax.nn.sigmoid(gate) * up).astype(BF16)
    y = x + jnp.dot(act_scr[...], wd_ref[...], preferred_element_type=F32)
    if final_norm:
        y = _rms(y, gf_ref[...])
    out_ref[...] = y


def _ffn(x2d, g_ffn, w_gu, w_down, layer, g_final, final_norm, tm=512):
    n = x2d.shape[0]
    row = pl.BlockSpec((tm, D_MODEL), lambda i: (i, 0))
    return pl.pallas_call(
        functools.partial(_ffn_kernel, final_norm=final_norm),
        grid=(n // tm,),
        in_specs=[row, _const_spec((1, D_MODEL)), _layer_spec(w_gu, layer), _layer_spec(w_down, layer),
                  _const_spec((1, D_MODEL))],
        out_specs=row,
        out_shape=jax.ShapeDtypeStruct((n, D_MODEL), F32),
        scratch_shapes=[pltpu.VMEM((tm, D_MODEL), BF16), pltpu.VMEM((tm, FFN_HIDDEN), BF16)],
        compiler_params=_cparams(("arbitrary",)),
        name="ffn",
    )(x2d, g_ffn.reshape(1, D_MODEL), w_gu, w_down, g_final.reshape(1, D_MODEL))


def kernel(x, rel_bias, g_mix, w_in, conv_w, conv_b, lru_wa, lru_ba, lru_wx, lru_bx, lru_lam,
           p_a, p_b, p_c, w_out, g_ffn, w_gu, w_down, g_final):
    batch, seq, d = x.shape
    assert d == D_MODEL and w_in.shape[-1] == IN_COLS
    depth = w_in.shape[0]
    nblk = seq // MOBA_BLOCK
    ndb = _moba_far_blocks(nblk)

    bias_a = _build_bias(rel_bias, _bucket_index_a(), A_HEADS, 0, A_HPG)
    bias_c = _build_bias(rel_bias, _bucket_index_c(ndb), C_HEADS, A_HEADS, None, LOG2E)

    w_in, p_a, p_b, p_c, w_out, w_gu, w_down = (
        t.astype(BF16) for t in (w_in, p_a, p_b, p_c, w_out, w_gu, w_down))

    x2d = x.reshape(batch * seq, d)
    for l in range(depth):
        *za, zb, zc, vt, zg = _in_proj(x2d, g_mix[l], w_in, l, batch, seq)
        oa, lse = zip(*[_mixer_a_group(za[g], bias_a, g, dil, batch, seq)
                        for g, (_, dil) in enumerate(A_GROUPS)])
        ob = _mixer_b(zb, conv_w[l], conv_b[l], lru_wa[l], lru_ba[l], lru_wx[l], lru_bx[l], lru_lam[l],
                      batch, seq)
        oc = _mixer_c(zc, vt, bias_c, batch, seq, ndb)
        x2d = _merge(x2d, oa, lse, ob, oc, zg, p_a, p_b, p_c, w_out, l)
        x2d = _ffn(x2d, g_ffn[l], w_gu, w_down, l, g_final, final_norm=(l == depth - 1))
    return x2d.reshape(batch, seq, d)
```

```python
import functools
import math

import numpy as np
import jax
import jax.numpy as jnp
from jax import lax
from jax.experimental import pallas as pl
from jax.experimental.pallas import tpu as pltpu

F32 = jnp.float32
BF16 = jnp.bfloat16

D_MODEL = 1024
HEAD_DIM = 64
SCALE = HEAD_DIM ** -0.5
LOG2E = math.log2(math.e)
NEG_INF = -1e30
EPS = 1e-6
A_GROUPS = ((128, 1), (512, 4), (2048, 16))
A_HPG = 4
A_HEADS = A_HPG * len(A_GROUPS)
A_WIDTH = A_HEADS * HEAD_DIM
A_OUT = A_HPG * HEAD_DIM
A_BLOCK = 128
LRU_WIDTH = D_MODEL // 2
LRU_BLOCKS = 8
CONV_WIDTH = 4
LRU_C = 8.0
C_HEADS = 8
C_WIDTH = C_HEADS * HEAD_DIM
MOBA_BLOCK = 256
MOBA_TOPK = 3
REL_BUCKETS = 32
REL_MAX_DIST = 2048
FFN_HIDDEN = 2816
IN_COLS = 3 * A_WIDTH + 2 * LRU_WIDTH + 3 * C_WIDTH + 3 * D_MODEL
V_C_OFF = 3 * A_WIDTH + 2 * LRU_WIDTH + 2 * C_WIDTH

LANE = 128
V_ROWS = HEAD_DIM + 16
VMEM_LIMIT = 56 * 1024 * 1024

NT_DIMS = (((1,), (1,)), ((), ()))


def _cparams(sem, vmem=VMEM_LIMIT):
    return pltpu.CompilerParams(dimension_semantics=sem, vmem_limit_bytes=vmem)


def _const_spec(shape):
    nd = len(shape)
    return pl.BlockSpec(shape, lambda *_: (0,) * nd, pipeline_mode=pl.Buffered(1))


def _layer_spec(stacked, layer):
    return pl.BlockSpec((None,) + stacked.shape[1:], lambda *_: (layer, 0, 0), pipeline_mode=pl.Buffered(1))


def _rel_bucket_np(dist):
    max_exact = REL_BUCKETS // 2
    d = np.maximum(dist, 0)
    df = np.maximum(d, 1).astype(np.float32)
    large = max_exact + (np.log(df / np.float32(max_exact)) / np.float32(math.log(REL_MAX_DIST / max_exact))
                         * np.float32(REL_BUCKETS - max_exact)).astype(np.int32)
    large = np.minimum(large, REL_BUCKETS - 1)
    return np.where(d < max_exact, d, large).astype(np.int32)


def _bucket_index_a():
    qi = np.arange(A_BLOCK)[None, :] + A_BLOCK
    kj = np.arange(2 * A_BLOCK)[:, None]
    delta = qi - kj
    mats = []
    for window, dil in A_GROUPS:
        band = (delta >= 0) & (delta <= window // dil)
        mats.append(np.where(band, _rel_bucket_np(delta * dil), -1))
    return np.stack(mats).astype(np.int32)


def _moba_far_blocks(nblk):
    for db in range(1, nblk + 1):
        lo = db * MOBA_BLOCK - (MOBA_BLOCK - 1)
        if np.all(_rel_bucket_np(np.arange(lo, nblk * MOBA_BLOCK)) == REL_BUCKETS - 1):
            return db
    return nblk


def _bucket_index_c(ndb):
    k = np.arange(MOBA_BLOCK)[:, None]
    q = np.arange(MOBA_BLOCK)[None, :]
    mats = []
    for db in range(ndb + 1):
        dist = db * MOBA_BLOCK + q - k
        b = _rel_bucket_np(dist)
        if db == 0:
            b = np.where(dist >= 0, b, -1)
        mats.append(b)
    return np.stack(mats).astype(np.int32)


def _bias_kernel(tab_ref, idx_ref, out_ref, *, head_off, unit, present):
    h = pl.program_id(0) + head_off
    for m, buckets in enumerate(present):
        idx = idx_ref[m]
        out = jnp.full(idx.shape, NEG_INF, F32)
        for b in buckets:
            out = jnp.where(idx == b, tab_ref[b, h] * unit, out)
        out_ref[0, m] = out


def _build_bias(rel_bias, idx, n_heads, head_off, heads_per_idx_group, unit=1.0):
    r, c = idx.shape[-2:]
    if heads_per_idx_group is None:
        n_mats = idx.shape[0]
        idx_map = lambda h: (0, 0, 0)
        present = tuple(tuple(int(b) for b in np.unique(m[m >= 0])) for m in idx)
    else:
        n_mats = 1
        idx_map = lambda h: (h // heads_per_idx_group, 0, 0)
        present = (tuple(int(b) for b in np.unique(idx[idx >= 0])),)
    return pl.pallas_call(
        functools.partial(_bias_kernel, head_off=head_off, unit=unit, present=present),
        grid=(n_heads,),
        in_specs=[pl.BlockSpec(memory_space=pltpu.SMEM),
                  pl.BlockSpec((n_mats, r, c), idx_map)],
        out_specs=pl.BlockSpec((1, n_mats, r, c), lambda h: (h, 0, 0, 0)),
        out_shape=jax.ShapeDtypeStruct((n_heads, n_mats, r, c), F32),
        compiler_params=_cparams(("arbitrary",)),
        name="rel_bias_build",
    )(rel_bias, jnp.asarray(idx))


def _col_chunks(start, stop, width=512):
    out = []
    while start < stop:
        w = min(width, stop - start)
        out.append((start, w))
        start += w
    return out


def _in_proj_kernel(x_ref, g_ref, w_ref, wvt_ref, za0_ref, za1_ref, za2_ref, zb_ref, zc_ref, vt_ref, zg_ref,
                    h_scr, dil_scr, *, tm):
    x = x_ref[...]
    ms = jnp.mean(x * x, axis=-1, keepdims=True)
    h_scr[...] = (x * lax.rsqrt(ms + EPS) * g_ref[...]).astype(BF16)

    def proj(base, width):
        return jnp.dot(h_scr[...], w_ref[:, base:base + width], preferred_element_type=F32)

    def seg(out_ref, base, width, scaled_cols=0, scale=1.0, chunk=512):
        assert scaled_cols % chunk == 0
        for off, w in _col_chunks(0, width, chunk):
            r = proj(base + off, w)
            if off < scaled_cols:
                r = r * scale
            out_ref[:, off:off + w] = r.astype(out_ref.dtype)

    for g, (za_ref, (_, dil)) in enumerate(zip((za0_ref, za1_ref, za2_ref), A_GROUPS)):
        qkv = [proj(s * A_WIDTH + g * A_OUT, A_OUT) for s in range(3)]
        qkv[0] = qkv[0] * SCALE
        if dil == 1:
            for s, part in enumerate(qkv):
                za_ref[0, 0, :, s * A_OUT:(s + 1) * A_OUT] = part.astype(BF16)
        else:
            planes = [part[:, c * LANE:(c + 1) * LANE] for part in qkv for c in range(A_OUT // LANE)]
            for c, plane in enumerate(planes):
                dil_scr[c] = plane
            for r in range(dil):
                for c in range(len(planes)):
                    za_ref[0, r, :, c * LANE:(c + 1) * LANE] = (
                        dil_scr[c, pl.ds(r, tm // dil, stride=dil), :].astype(BF16))

    wa, wb, wc = 3 * A_WIDTH, 2 * LRU_WIDTH, 2 * C_WIDTH
    seg(zb_ref, wa, wb)
    seg(zc_ref, wa + wb, wc, C_WIDTH, SCALE * LOG2E)
    seg(zg_ref, V_C_OFF + C_WIDTH, 3 * D_MODEL)
    vt = lax.dot_general(wvt_ref[...], h_scr[...], NT_DIMS, preferred_element_type=F32).astype(BF16)
    for h in range(C_HEADS):
        vt_ref[0, h * V_ROWS:h * V_ROWS + HEAD_DIM, :] = vt[h * HEAD_DIM:(h + 1) * HEAD_DIM]
        vt_ref[0, h * V_ROWS + HEAD_DIM:(h + 1) * V_ROWS, :] = jnp.ones((V_ROWS - HEAD_DIM, tm), BF16)


def _in_proj(x2d, g, w_in, layer, batch, seq, tm=512):
    n = x2d.shape[0]
    wvt = w_in[layer, :, V_C_OFF:V_C_OFF + C_WIDTH].T
    wb, wc, wg = 2 * LRU_WIDTH, 2 * C_WIDTH, 3 * D_MODEL
    gw = 3 * A_OUT
    tiles_per_seq = seq // tm
    row = lambda i: (i, 0)
    seq_tile = lambda i: (i // tiles_per_seq, 0, i % tiles_per_seq, 0)
    za_specs = [pl.BlockSpec((1, dil, tm // dil, gw), seq_tile) for _, dil in A_GROUPS]
    za_shapes = [jax.ShapeDtypeStruct((batch, dil, seq // dil, gw), BF16) for _, dil in A_GROUPS]
    return pl.pallas_call(
        functools.partial(_in_proj_kernel, tm=tm),
        grid=(n // tm,),
        in_specs=[pl.BlockSpec((tm, D_MODEL), row),
                  _const_spec((1, D_MODEL)),
                  _layer_spec(w_in, layer),
                  _const_spec(wvt.shape)],
        out_specs=za_specs + [pl.BlockSpec((tm, wb), row),
                              pl.BlockSpec((tm, wc), row),
                              pl.BlockSpec((1, C_HEADS * V_ROWS, tm),
                                           lambda i: (i // tiles_per_seq, 0, i % tiles_per_seq)),
                              pl.BlockSpec((tm, wg), row)],
        out_shape=za_shapes + [jax.ShapeDtypeStruct((n, wb), BF16),
                               jax.ShapeDtypeStruct((n, wc), BF16),
                               jax.ShapeDtypeStruct((batch, C_HEADS * V_ROWS, seq), BF16),
                               jax.ShapeDtypeStruct((n, wg), BF16)],
        scratch_shapes=[pltpu.VMEM((tm, D_MODEL), BF16), pltpu.VMEM((gw // LANE, tm, LANE), F32)],
        compiler_params=_cparams(("arbitrary",)),
        name="in_proj",
    )(x2d, g.reshape(1, D_MODEL), w_in, wvt)


A_TOKENS_PER_STEP = 2048
A_CHAINS = 4


def _mixer_a_kernel(z_ref, zp_ref, bias_ref, o_ref, lse_ref, *, dil, nsub):
    t = pl.program_id(1)
    blk, width = A_BLOCK, A_OUT
    kcols, vcols = slice(width, 2 * width), slice(2 * width, 3 * width)
    lane_head = lax.broadcasted_iota(jnp.int32, (blk, width), 1) // HEAD_DIM
    in_prev = (lax.broadcasted_iota(jnp.int32, (2 * blk, 2 * blk), 0) < blk).astype(F32)
    no_prev = in_prev * jnp.where(t == 0, NEG_INF, 0.0)
    transpose_v = lambda v: v.astype(F32).T.astype(BF16)

    ones_rows = jnp.ones((V_ROWS - HEAD_DIM, 2 * blk), BF16)
    blocks, windows = {}, {}

    def block_kv(r, j):
        if (r, j) not in blocks:
            src = zp_ref[0, r] if j < 0 else z_ref[0, r, j * blk:(j + 1) * blk, :]
            blocks[(r, j)] = (src[:, kcols], transpose_v(src[:, vcols]))
        return blocks[(r, j)]

    def window(r, j):
        if (r, j) not in windows:
            (k_prev, vt_prev), (k_cur, vt_cur) = block_kv(r, j - 1), block_kv(r, j)
            k_win = jnp.concatenate([k_prev, k_cur], axis=0)
            vt_win = jnp.concatenate([vt_prev, vt_cur], axis=1)
            windows[(r, j)] = (z_ref[0, r, j * blk:(j + 1) * blk, 0:width], k_win, vt_win)
        return windows[(r, j)]

    def logits(r, j, hp):
        q, k_win, _ = window(r, j)
        heads = (2 * hp, 2 * hp + 1)
        q2 = jnp.concatenate([jnp.where(lane_head == h, q, jnp.zeros_like(q)) for h in heads], axis=0)
        bias2 = jnp.concatenate([bias_ref[h, 0] for h in heads], axis=1)
        s = lax.dot_general(k_win, q2, NT_DIMS, preferred_element_type=F32) + bias2
        return s + no_prev if j == 0 else s

    def attend(r, j, hp, s):
        vt_win = window(r, j)[2]
        m = jnp.max(s, axis=0, keepdims=True)
        p = jnp.exp((s - m).astype(BF16))
        vt_ext = jnp.concatenate([part for h in (2 * hp, 2 * hp + 1)
                                  for part in (vt_win[h * HEAD_DIM:(h + 1) * HEAD_DIM, :], ones_rows)], axis=0)
        pv = jnp.dot(vt_ext, p, preferred_element_type=F32)
        res = []
        for u in range(2):
            cols = slice(u * blk, (u + 1) * blk)
            den = pv[u * V_ROWS + HEAD_DIM:u * V_ROWS + HEAD_DIM + 1, cols]
            res.append((pv[u * V_ROWS:u * V_ROWS + HEAD_DIM, cols] / den,
                        jnp.broadcast_to(m[:, cols] + jnp.log(den), (HEAD_DIM, blk))))
        return res

    n_hp = A_HPG // 2

    def chain(units):
        s_next = logits(*units[0])
        outs, lses = [], []
        for i, (r, j, hp) in enumerate(units):
            s_cur = s_next
            if i + 1 < len(units):
                s_next = logits(*units[i + 1])
            for o, lse in attend(r, j, hp, s_cur):
                outs.append(o)
                lses.append(lse)
            if hp == n_hp - 1:
                o_t = jnp.concatenate(outs, axis=0).T
                lse_t = jnp.concatenate(lses, axis=0).T
                outs, lses = [], []
                dst = slice(j * blk, (j + 1) * blk) if dil == 1 else pl.ds(j * blk * dil + r, blk, stride=dil)
                for c in range(width // LANE):
                    o_ref[c, dst, :] = o_t[:, c * LANE:(c + 1) * LANE]
                    lse_ref[c, dst, :] = lse_t[:, c * LANE:(c + 1) * LANE]
            yield

    units = [(r, j, hp) for r in range(dil) for j in range(nsub) for hp in range(n_hp)]
    per_chain = len(units) // A_CHAINS
    _run_round_robin(*[chain(units[i * per_chain:(i + 1) * per_chain]) for i in range(A_CHAINS)])


def _mixer_a_group(za, bias_a, g, dil, batch, seq):
    sub_len = seq // dil
    assert sub_len % A_BLOCK == 0
    tq = min(sub_len, A_TOKENS_PER_STEP // dil)
    assert tq % A_BLOCK == 0
    nsub = tq // A_BLOCK
    gw = 3 * A_OUT
    out_spec = pl.BlockSpec((A_OUT // LANE, tq * dil, LANE), lambda b, t: (0, b * (sub_len // tq) + t, 0))
    return pl.pallas_call(
        functools.partial(_mixer_a_kernel, dil=dil, nsub=nsub),
        grid=(batch, sub_len // tq),
        in_specs=[pl.BlockSpec((1, dil, tq, gw), lambda b, t: (b, 0, t, 0)),
                  pl.BlockSpec((1, dil, A_BLOCK, gw), lambda b, t: (b, 0, jnp.maximum(t * nsub - 1, 0), 0)),
                  pl.BlockSpec((A_HPG, 1, 2 * A_BLOCK, A_BLOCK), lambda b, t: (g, 0, 0, 0))],
        out_specs=[out_spec, out_spec],
        out_shape=[jax.ShapeDtypeStruct((A_OUT // LANE, batch * seq, LANE), F32)] * 2,
        compiler_params=_cparams(("arbitrary", "arbitrary")),
        name=f"mixer_a_d{dil}",
    )(za, za, bias_a)


SUBLANES = 8


def _mixer_b_kernel(z_ref, cw_ref, cb_ref, wax_ref, bax_ref, lam_ref, y_ref,
                    nat_scr, xe_scr, a_scr, b_scr, tail_scr, h_scr, *, ts):
    w = LRU_WIDTH
    seg = ts // SUBLANES
    halo = CONV_WIDTH - 1
    planes = w // LANE
    vrow = lambda i: slice(i * SUBLANES, (i + 1) * SUBLANES)
    sub = lax.broadcasted_iota(jnp.int32, (SUBLANES, w), 0)

    @pl.when(pl.program_id(1) == 0)
    def _():
        tail_scr[...] = jnp.zeros((halo * SUBLANES, w), F32)
        h_scr[...] = jnp.zeros((1, w), F32)

    x = z_ref[0, :, 0:w].astype(F32)
    for c in range(planes):
        nat_scr[c] = x[:, c * LANE:(c + 1) * LANE]
    for i in range(seg):
        for c in range(planes):
            xe_scr[vrow(halo + i), c * LANE:(c + 1) * LANE] = nat_scr[c, pl.ds(i, SUBLANES, stride=seg), :]
    for k in range(1, halo + 1):
        cur = xe_scr[vrow(halo + seg - k), :]
        prev = tail_scr[vrow(halo - k), :]
        xe_scr[vrow(halo - k), :] = jnp.where(sub == 0, pltpu.roll(prev, 1, 0), pltpu.roll(cur, 1, 0))
        tail_scr[vrow(halo - k), :] = cur
    xc = cb_ref[...]
    for j in range(CONV_WIDTH):
        start = (halo - (CONV_WIDTH - 1 - j)) * SUBLANES
        xc = xc + xe_scr[start:start + ts, :] * cw_ref[j:j + 1, :]

    ra = jnp.dot(xc.astype(BF16), wax_ref[...], preferred_element_type=F32) + bax_ref[...]
    r = jax.nn.sigmoid(ra[:, :w])
    ig = jax.nn.sigmoid(ra[:, w:])
    nl = -lam_ref[...]
    softplus = jnp.maximum(nl, 0.0) + jnp.log1p(jnp.exp(-jnp.abs(nl)))
    log_a = -LRU_C * r * softplus
    a = jnp.exp(log_a)
    a_scr[...] = a
    b_scr[...] = jnp.sqrt(-jnp.tanh(log_a) * (a * a + 1.0)) * (ig * xc)

    h = jnp.zeros((SUBLANES, w), F32)
    p = jnp.ones((SUBLANES, w), F32)
    for i in range(seg):
        a_i = a_scr[vrow(i), :]
        h = a_i * h + b_scr[vrow(i), :]
        p = a_i * p
        b_scr[vrow(i), :] = h
        a_scr[vrow(i), :] = p
    for s in (1, 2, 4):
        keep = sub >= s
        p_sh = jnp.where(keep, pltpu.roll(p, s, 0), 1.0)
        h_sh = jnp.where(keep, pltpu.roll(h, s, 0), 0.0)
        h = p * h_sh + h
        p = p * p_sh
    h_in = h_scr[...]
    after = p * h_in + h
    carry = jnp.where(sub == 0, h_in, pltpu.roll(after, 1, 0))
    h_scr[...] = after[SUBLANES - 1:SUBLANES, :]
    for i in range(seg):
        hs = b_scr[vrow(i), :] + a_scr[vrow(i), :] * carry
        for c in range(planes):
            nat_scr[c, pl.ds(i, SUBLANES, stride=seg), :] = hs[:, c * LANE:(c + 1) * LANE]
    h_nat = jnp.concatenate([nat_scr[c] for c in range(planes)], axis=1)

    gb = z_ref[0, :, w:2 * w].astype(F32)
    cdf = 0.5 * (1.0 + jnp.tanh(math.sqrt(2.0 / math.pi) * (gb + 0.044715 * (gb * gb * gb))))
    y_ref[0] = (h_nat * (gb * cdf)).astype(y_ref.dtype)


def _block_diag(wblocks):
    nb, di, do = wblocks.shape
    eye = jnp.eye(nb, dtype=wblocks.dtype)
    return (eye[:, None, :, None] * wblocks[:, :, None, :]).reshape(nb * di, nb * do)


def _mixer_b(zb, conv_w, conv_b, lru_wa, lru_ba, lru_wx, lru_bx, lru_lam, batch, seq, ts=512):
    w = LRU_WIDTH
    wax = jnp.concatenate([_block_diag(lru_wa), _block_diag(lru_wx)], axis=1).astype(BF16)
    bax = jnp.concatenate([lru_ba, lru_bx]).reshape(1, 2 * w)
    y = pl.pallas_call(
        functools.partial(_mixer_b_kernel, ts=ts),
        grid=(batch, seq // ts),
        in_specs=[pl.BlockSpec((1, ts, 2 * w), lambda b, t: (b, t, 0)),
                  _const_spec((CONV_WIDTH, w)), _const_spec((1, w)),
                  _const_spec((w, 2 * w)), _const_spec((1, 2 * w)), _const_spec((1, w))],
        out_specs=pl.BlockSpec((1, ts, w), lambda b, t: (b, t, 0)),
        out_shape=jax.ShapeDtypeStruct((batch, seq, w), BF16),
        scratch_shapes=[pltpu.VMEM((w // LANE, ts, LANE), F32),
                        pltpu.VMEM((ts + (CONV_WIDTH - 1) * SUBLANES, w), F32),
                        pltpu.VMEM((ts, w), F32), pltpu.VMEM((ts, w), F32),
                        pltpu.VMEM(((CONV_WIDTH - 1) * SUBLANES, w), F32), pltpu.VMEM((1, w), F32)],
        compiler_params=_cparams(("arbitrary", "arbitrary")),
        name="mixer_b",
    )(zb.reshape(batch, seq, 2 * w), conv_w, conv_b.reshape(1, w), wax, bax, lru_lam.reshape(1, w))
    return y.reshape(batch * seq, w)


HEAD_PAIR = LANE // HEAD_DIM


def _mixer_c_prologue(q_ref, k_ref, qm_scr, sel_scr, *, nblk):
    blk = MOBA_BLOCK
    seq = nblk * blk
    kmean = jnp.concatenate(
        [jnp.sum(k_ref[0, n * blk:(n + 1) * blk, :].astype(F32), axis=0, keepdims=True) for n in range(nblk)],
        axis=0) * (1.0 / blk)
    km_hi = kmean.astype(BF16)
    km_lo = (kmean - km_hi.astype(F32)).astype(BF16)
    q = q_ref[0]
    lane = lax.broadcasted_iota(jnp.int32, q.shape, 1)
    km_lane = lax.broadcasted_iota(jnp.int32, kmean.shape, 1)
    blk_id = lax.broadcasted_iota(jnp.int32, (nblk, seq), 0)
    q_blk = lax.broadcasted_iota(jnp.int32, (nblk, seq), 1) // blk
    past = blk_id < q_blk
    in_head = lambda idx, hh: (idx >= hh * HEAD_DIM) & (idx < (hh + 1) * HEAD_DIM)
    km_rows = jnp.concatenate([jnp.where(in_head(km_lane, hh), part, jnp.zeros_like(part))
                               for hh in range(HEAD_PAIR) for part in (km_hi, km_lo)], axis=0)
    gates = lax.dot_general(km_rows, q, NT_DIMS, preferred_element_type=F32)
    for hh in range(HEAD_PAIR):
        qm_scr[hh] = jnp.where(in_head(lane, hh), q, jnp.zeros_like(q))
        gate = gates[2 * hh * nblk:(2 * hh + 1) * nblk] + gates[(2 * hh + 1) * nblk:(2 * hh + 2) * nblk]
        gate = jnp.where(past, gate, NEG_INF)
        sel = jnp.zeros((nblk, seq), F32)
        for _ in range(min(MOBA_TOPK, nblk - 1)):
            top = jnp.max(gate, axis=0, keepdims=True)
            first = jnp.min(jnp.where(gate == top, blk_id, nblk), axis=0, keepdims=True)
            pick = blk_id == first
            sel = jnp.where(pick, 1.0, sel)
            gate = jnp.where(pick, -jnp.inf, gate)
        sel_scr[hh] = jnp.where(past, sel, 0.0)


def _run_round_robin(*gens):
    gens = list(gens)
    while gens:
        gens = [g for g in gens if next(g, StopIteration) is not StopIteration]


def _mixer_c_block(c, k_ref, vt_ref, bias_ref, o_ref, qm_scr, sel_scr, s_scr, m_scr, p_scr, *, ndb):
    blk = MOBA_BLOCK
    par = c % 2
    qcols = slice(c * blk, (c + 1) * blk)
    fold = lambda t: t.reshape(blk // SUBLANES, SUBLANES, blk)
    heads = range(HEAD_PAIR)
    pairs = [(k0, min(k0 + 2, c + 1)) for k0 in range(0, c + 1, 2)]
    picked = lambda hh, kb: sel_scr[hh, kb:kb + 1, qcols] > 0.0
    far_bias = lambda hh: bias_ref[hh, ndb, 0:1, :]

    def logits(hh):
        qm = qm_scr[hh, qcols, :]
        m8 = jnp.full((SUBLANES, blk), NEG_INF, F32)
        for k0, k1 in pairs:
            s = lax.dot_general(k_ref[0, k0 * blk:k1 * blk, :], qm, NT_DIMS, preferred_element_type=F32)
            for kb in range(k0, k1):
                su = s[(kb - k0) * blk:(kb - k0 + 1) * blk]
                if c - kb < ndb:
                    su = su + bias_ref[hh, c - kb]
                    t8 = jnp.max(fold(su), axis=0)
                else:
                    t8 = jnp.max(fold(su), axis=0) + far_bias(hh)
                s_scr[par, hh, kb * blk:(kb + 1) * blk, :] = su
                m8 = jnp.maximum(m8, t8 if kb == c else jnp.where(picked(hh, kb), t8, NEG_INF))
            yield
        m_scr[par, hh] = jnp.broadcast_to(jnp.max(m8, axis=0, keepdims=True), (SUBLANES, blk))

    outs = {}

    def attend(hh):
        m = m_scr[par, hh, 0:1, :]
        pv = jnp.zeros((V_ROWS, blk), F32)
        for k0, k1 in pairs:
            for kb in range(k0, k1):
                mk = m if kb == c else jnp.where(picked(hh, kb), m, -NEG_INF)
                if c - kb >= ndb:
                    mk = mk - far_bias(hh)
                x = s_scr[par, hh, kb * blk:(kb + 1) * blk, :] - mk
                p_scr[hh, kb * blk:(kb + 1) * blk, :] = jnp.exp2(x.astype(BF16))
            pv = pv + jnp.dot(vt_ref[0, hh * V_ROWS:(hh + 1) * V_ROWS, k0 * blk:k1 * blk],
                              p_scr[hh, k0 * blk:k1 * blk, :], preferred_element_type=F32)
            yield
        outs[hh] = pv[:HEAD_DIM] / pv[HEAD_DIM:HEAD_DIM + 1]
        if len(outs) == HEAD_PAIR:
            o_ref[0, qcols, :] = jnp.concatenate([outs[h] for h in heads], axis=0).T.astype(o_ref.dtype)

    return [logits(hh) for hh in heads], [attend(hh) for hh in heads]


def _mixer_c_kernel(q_ref, k_ref, vt_ref, bias_ref, o_ref, qm_scr, sel_scr, s_scr, m_scr, p_scr, *, nblk, ndb):
    _mixer_c_prologue(q_ref, k_ref, qm_scr, sel_scr, nblk=nblk)
    stages = functools.partial(_mixer_c_block, k_ref=k_ref, vt_ref=vt_ref, bias_ref=bias_ref, o_ref=o_ref,
                               qm_scr=qm_scr, sel_scr=sel_scr, s_scr=s_scr, m_scr=m_scr, p_scr=p_scr, ndb=ndb)

    def step(t):
        gens = []
        if t < nblk:
            gens += stages(t)[0]
        if t >= 1:
            gens += stages(t - 1)[1]
        _run_round_robin(*gens)

    def body(i, carry):
        for t in range(nblk + 1):
            pl.when(i == t)(functools.partial(step, t))
        return carry

    lax.fori_loop(0, nblk + 1, body, 0)


def _mixer_c(zc, vt, bias_c, batch, seq, ndb):
    blk = MOBA_BLOCK
    assert seq % blk == 0
    nblk = seq // blk
    npair = C_HEADS // HEAD_PAIR
    z = zc.reshape(batch, seq, 2 * C_WIDTH)
    o = pl.pallas_call(
        functools.partial(_mixer_c_kernel, nblk=nblk, ndb=ndb),
        grid=(batch, npair),
        in_specs=[pl.BlockSpec((1, seq, LANE), lambda b, hp: (b, 0, hp)),
                  pl.BlockSpec((1, seq, LANE), lambda b, hp: (b, 0, npair + hp)),
                  pl.BlockSpec((1, HEAD_PAIR * V_ROWS, seq), lambda b, hp: (b, hp, 0)),
                  pl.BlockSpec((HEAD_PAIR, ndb + 1, blk, blk), lambda b, hp: (hp, 0, 0, 0))],
        out_specs=pl.BlockSpec((1, seq, LANE), lambda b, hp: (b, 0, hp)),
        out_shape=jax.ShapeDtypeStruct((batch, seq, C_WIDTH), BF16),
        scratch_shapes=[pltpu.VMEM((HEAD_PAIR, seq, LANE), BF16), pltpu.VMEM((HEAD_PAIR, nblk, seq), F32),
                        pltpu.VMEM((2, HEAD_PAIR, seq, blk), F32), pltpu.VMEM((2, HEAD_PAIR, SUBLANES, blk), F32),
                        pltpu.VMEM((HEAD_PAIR, seq, blk), BF16)],
        compiler_params=_cparams(("arbitrary", "arbitrary")),
        name="mixer_c",
    )(z, z, vt, bias_c)
    return o.reshape(batch * seq, C_WIDTH)


def _merge_kernel(x_ref, o1_ref, o2_ref, o3_ref, l1_ref, l2_ref, l3_ref, ob_ref, oc_ref, zg_ref,
                  pa_ref, pb_ref, pc_ref, wo_ref, out_ref):
    d = D_MODEL
    halves = lambda ref: jnp.concatenate([ref[c] for c in range(ref.shape[0])], axis=-1)
    l1, l2, l3 = halves(l1_ref), halves(l2_ref), halves(l3_ref)
    m = jnp.maximum(jnp.maximum(l1, l2), l3)
    e1, e2, e3 = jnp.exp(l1 - m), jnp.exp(l2 - m), jnp.exp(l3 - m)
    o_a = (e1 * halves(o1_ref) + e2 * halves(o2_ref) + e3 * halves(o3_ref)) / (e1 + e2 + e3)

    def branch(o, p_ref, k):
        gate = jax.nn.sigmoid(zg_ref[:, k * d:(k + 1) * d].astype(F32))
        return gate * jnp.dot(o, p_ref[...], preferred_element_type=F32)

    merged = branch(o_a.astype(BF16), pa_ref, 0) + branch(ob_ref[...], pb_ref, 1) + branch(oc_ref[...], pc_ref, 2)
    out_ref[...] = x_ref[...] + jnp.dot(merged.astype(BF16), wo_ref[...], preferred_element_type=F32)


def _merge(x2d, oa, lse, ob, oc, zg, p_a, p_b, p_c, w_out, layer, tm=512):
    n = x2d.shape[0]
    row = lambda width: pl.BlockSpec((tm, width), lambda i: (i, 0))
    return pl.pallas_call(
        _merge_kernel,
        grid=(n // tm,),
        in_specs=[row(D_MODEL)] + [pl.BlockSpec((A_OUT // LANE, tm, LANE), lambda i: (0, i, 0))] * 6
                 + [row(LRU_WIDTH), row(C_WIDTH), row(3 * D_MODEL),
                  _layer_spec(p_a, layer), _layer_spec(p_b, layer), _layer_spec(p_c, layer),
                  _layer_spec(w_out, layer)],
        out_specs=row(D_MODEL),
        out_shape=jax.ShapeDtypeStruct((n, D_MODEL), F32),
        compiler_params=_cparams(("arbitrary",)),
        name="merge",
    )(x2d, *oa, *lse, ob, oc, zg, p_a, p_b, p_c, w_out)


def _rms(x, g):
    return x * lax.rsqrt(jnp.mean(x * x, axis=-1, keepdims=True) + EPS) * g


def _ffn_kernel(x_ref, g_ref, wgu_ref, wd_ref, gf_ref, out_ref, h_scr, act_scr, *, final_norm):
    x = x_ref[...]
    h_scr[...] = _rms(x, g_ref[...]).astype(BF16)
    for off, w in _col_chunks(0, FFN_HIDDEN, 256):
        gate = jnp.dot(h_scr[...], wgu_ref[:, off:off + w], preferred_element_type=F32)
        up = jnp.dot(h_scr[...], wgu_ref[:, FFN_HIDDEN + off:FFN_HIDDEN + off + w], preferred_element_type=F32)
        act_scr[:, off:off + w] = (gate * jax.nn.sigmoid(gate) * up).astype(BF16)
    y = x + jnp.dot(act_scr[...], wd_ref[...], preferred_element_type=F32)
    if final_norm:
        y = _rms(y, gf_ref[...])
    out_ref[...] = y


def _ffn(x2d, g_ffn, w_gu, w_down, layer, g_final, final_norm, tm=512):
    n = x2d.shape[0]
    row = pl.BlockSpec((tm, D_MODEL), lambda i: (i, 0))
    return pl.pallas_call(
        functools.partial(_ffn_kernel, final_norm=final_norm),
        grid=(n // tm,),
        in_specs=[row, _const_spec((1, D_MODEL)), _layer_spec(w_gu, layer), _layer_spec(w_down, layer),
                  _const_spec((1, D_MODEL))],
        out_specs=row,
        out_shape=jax.ShapeDtypeStruct((n, D_MODEL), F32),
        scratch_shapes=[pltpu.VMEM((tm, D_MODEL), BF16), pltpu.VMEM((tm, FFN_HIDDEN), BF16)],
        compiler_params=_cparams(("arbitrary",)),
        name="ffn",
    )(x2d, g_ffn.reshape(1, D_MODEL), w_gu, w_down, g_final.reshape(1, D_MODEL))


def kernel(x, rel_bias, g_mix, w_in, conv_w, conv_b, lru_wa, lru_ba, lru_wx, lru_bx, lru_lam,
           p_a, p_b, p_c, w_out, g_ffn, w_gu, w_down, g_final):
    batch, seq, d = x.shape
    assert d == D_MODEL and w_in.shape[-1] == IN_COLS
    depth = w_in.shape[0]
    nblk = seq // MOBA_BLOCK
    ndb = _moba_far_blocks(nblk)

    bias_a = _build_bias(rel_bias, _bucket_index_a(), A_HEADS, 0, A_HPG)
    bias_c = _build_bias(rel_bias, _bucket_index_c(ndb), C_HEADS, A_HEADS, None, LOG2E)

    w_in, p_a, p_b, p_c, w_out, w_gu, w_down = (
        t.astype(BF16) for t in (w_in, p_a, p_b, p_c, w_out, w_gu, w_down))

    x2d = x.reshape(batch * seq, d)
    for l in range(depth):
        *za, zb, zc, vt, zg = _in_proj(x2d, g_mix[l], w_in, l, batch, seq)
        oa, lse = zip(*[_mixer_a_group(za[g], bias_a, g, dil, batch, seq)
                        for g, (_, dil) in enumerate(A_GROUPS)])
        ob = _mixer_b(zb, conv_w[l], conv_b[l], lru_wa[l], lru_ba[l], lru_wx[l], lru_bx[l], lru_lam[l],
                      batch, seq)
        oc = _mixer_c(zc, vt, bias_c, batch, seq, ndb)
        x2d = _merge(x2d, oa, lse, ob, oc, zg, p_a, p_b, p_c, w_out, l)
        x2d = _ffn(x2d, g_ffn[l], w_gu, w_down, l, g_final, final_norm=(l == depth - 1))
    return x2d.reshape(batch, seq, d)
```

```python
import functools
import math

import numpy as np
import jax
import jax.numpy as jnp
from jax import lax
from jax.experimental import pallas as pl
from jax.experimental.pallas import tpu as pltpu

F32 = jnp.float32
BF16 = jnp.bfloat16

D_MODEL = 1024
HEAD_DIM = 64
SCALE = HEAD_DIM ** -0.5
LOG2E = math.log2(math.e)
NEG_INF = -1e30
EPS = 1e-6
A_GROUPS = ((128, 1), (512, 4), (2048, 16))
A_HPG = 4
A_HEADS = A_HPG * len(A_GROUPS)
A_WIDTH = A_HEADS * HEAD_DIM
A_OUT = A_HPG * HEAD_DIM
A_BLOCK = 128
LRU_WIDTH = D_MODEL // 2
LRU_BLOCKS = 8
CONV_WIDTH = 4
LRU_C = 8.0
C_HEADS = 8
C_WIDTH = C_HEADS * HEAD_DIM
MOBA_BLOCK = 256
MOBA_TOPK = 3
REL_BUCKETS = 32
REL_MAX_DIST = 2048
FFN_HIDDEN = 2816
IN_COLS = 3 * A_WIDTH + 2 * LRU_WIDTH + 3 * C_WIDTH + 3 * D_MODEL
V_C_OFF = 3 * A_WIDTH + 2 * LRU_WIDTH + 2 * C_WIDTH

LANE = 128
V_ROWS = HEAD_DIM + 16
VMEM_LIMIT = 56 * 1024 * 1024

NT_DIMS = (((1,), (1,)), ((), ()))


def _cparams(sem, vmem=VMEM_LIMIT):
    return pltpu.CompilerParams(dimension_semantics=sem, vmem_limit_bytes=vmem)


def _const_spec(shape):
    nd = len(shape)
    return pl.BlockSpec(shape, lambda *_: (0,) * nd, pipeline_mode=pl.Buffered(1))


def _layer_spec(stacked, layer):
    return pl.BlockSpec((None,) + stacked.shape[1:], lambda *_: (layer, 0, 0), pipeline_mode=pl.Buffered(1))


def _rel_bucket_np(dist):
    max_exact = REL_BUCKETS // 2
    d = np.maximum(dist, 0)
    df = np.maximum(d, 1).astype(np.float32)
    large = max_exact + (np.log(df / np.float32(max_exact)) / np.float32(math.log(REL_MAX_DIST / max_exact))
                         * np.float32(REL_BUCKETS - max_exact)).astype(np.int32)
    large = np.minimum(large, REL_BUCKETS - 1)
    return np.where(d < max_exact, d, large).astype(np.int32)


def _bucket_index_a():
    qi = np.arange(A_BLOCK)[None, :] + A_BLOCK
    kj = np.arange(2 * A_BLOCK)[:, None]
    delta = qi - kj
    mats = []
    for window, dil in A_GROUPS:
        band = (delta >= 0) & (delta <= window // dil)
        mats.append(np.where(band, _rel_bucket_np(delta * dil), -1))
    return np.stack(mats).astype(np.int32)


def _moba_far_blocks(nblk):
    for db in range(1, nblk + 1):
        lo = db * MOBA_BLOCK - (MOBA_BLOCK - 1)
        if np.all(_rel_bucket_np(np.arange(lo, nblk * MOBA_BLOCK)) == REL_BUCKETS - 1):
            return db
    return nblk


def _bucket_index_c(ndb):
    k = np.arange(MOBA_BLOCK)[:, None]
    q = np.arange(MOBA_BLOCK)[None, :]
    mats = []
    for db in range(ndb + 1):
        dist = db * MOBA_BLOCK + q - k
        b = _rel_bucket_np(dist)
        if db == 0:
            b = np.where(dist >= 0, b, -1)
        mats.append(b)
    return np.stack(mats).astype(np.int32)


def _bias_kernel(tab_ref, idx_ref, out_ref, *, head_off, unit, present):
    h = pl.program_id(0) + head_off
    for m, buckets in enumerate(present):
        idx = idx_ref[m]
        out = jnp.full(idx.shape, NEG_INF, F32)
        for b in buckets:
            out = jnp.where(idx == b, tab_ref[b, h] * unit, out)
        out_ref[0, m] = out


def _build_bias(rel_bias, idx, n_heads, head_off, heads_per_idx_group, unit=1.0):
    r, c = idx.shape[-2:]
    if heads_per_idx_group is None:
        n_mats = idx.shape[0]
        idx_map = lambda h: (0, 0, 0)
        present = tuple(tuple(int(b) for b in np.unique(m[m >= 0])) for m in idx)
    else:
        n_mats = 1
        idx_map = lambda h: (h // heads_per_idx_group, 0, 0)
        present = (tuple(int(b) for b in np.unique(idx[idx >= 0])),)
    return pl.pallas_call(
        functools.partial(_bias_kernel, head_off=head_off, unit=unit, present=present),
        grid=(n_heads,),
        in_specs=[pl.BlockSpec(memory_space=pltpu.SMEM),
                  pl.BlockSpec((n_mats, r, c), idx_map)],
        out_specs=pl.BlockSpec((1, n_mats, r, c), lambda h: (h, 0, 0, 0)),
        out_shape=jax.ShapeDtypeStruct((n_heads, n_mats, r, c), F32),
        compiler_params=_cparams(("arbitrary",)),
        name="rel_bias_build",
    )(rel_bias, jnp.asarray(idx))


def _col_chunks(start, stop, width=512):
    out = []
    while start < stop:
        w = min(width, stop - start)
        out.append((start, w))
        start += w
    return out


def _in_proj_kernel(x_ref, g_ref, w_ref, wvt_ref, za0_ref, za1_ref, za2_ref, zb_ref, zc_ref, vt_ref, zg_ref,
                    h_scr, dil_scr, *, tm):
    x = x_ref[...]
    ms = jnp.mean(x * x, axis=-1, keepdims=True)
    h_scr[...] = (x * lax.rsqrt(ms + EPS) * g_ref[...]).astype(BF16)

    def proj(base, width):
        return jnp.dot(h_scr[...], w_ref[:, base:base + width], preferred_element_type=F32)

    def seg(out_ref, base, width, scaled_cols=0, scale=1.0, chunk=512):
        assert scaled_cols % chunk == 0
        for off, w in _col_chunks(0, width, chunk):
            r = proj(base + off, w)
            if off < scaled_cols:
                r = r * scale
            out_ref[:, off:off + w] = r.astype(out_ref.dtype)

    for g, (za_ref, (_, dil)) in enumerate(zip((za0_ref, za1_ref, za2_ref), A_GROUPS)):
        qkv = [proj(s * A_WIDTH + g * A_OUT, A_OUT) for s in range(3)]
        qkv[0] = qkv[0] * SCALE
        if dil == 1:
            for s, part in enumerate(qkv):
                za_ref[0, 0, :, s * A_OUT:(s + 1) * A_OUT] = part.astype(BF16)
        else:
            planes = [part[:, c * LANE:(c + 1) * LANE] for part in qkv for c in range(A_OUT // LANE)]
            for c, plane in enumerate(planes):
                dil_scr[c] = plane
            for r in range(dil):
                for c in range(len(planes)):
                    za_ref[0, r, :, c * LANE:(c + 1) * LANE] = (
                        dil_scr[c, pl.ds(r, tm // dil, stride=dil), :].astype(BF16))

    wa, wb, wc = 3 * A_WIDTH, 2 * LRU_WIDTH, 2 * C_WIDTH
    seg(zb_ref, wa, wb)
    seg(zc_ref, wa + wb, wc, C_WIDTH, SCALE * LOG2E)
    seg(zg_ref, V_C_OFF + C_WIDTH, 3 * D_MODEL)
    vt = lax.dot_general(wvt_ref[...], h_scr[...], NT_DIMS, preferred_element_type=F32).astype(BF16)
    for h in range(C_HEADS):
        vt_ref[0, h * V_ROWS:h * V_ROWS + HEAD_DIM, :] = vt[h * HEAD_DIM:(h + 1) * HEAD_DIM]
        vt_ref[0, h * V_ROWS + HEAD_DIM:(h + 1) * V_ROWS, :] = jnp.ones((V_ROWS - HEAD_DIM, tm), BF16)


def _in_proj(x2d, g, w_in, layer, batch, seq, tm=512):
    n = x2d.shape[0]
    wvt = w_in[layer, :, V_C_OFF:V_C_OFF + C_WIDTH].T
    wb, wc, wg = 2 * LRU_WIDTH, 2 * C_WIDTH, 3 * D_MODEL
    gw = 3 * A_OUT
    tiles_per_seq = seq // tm
    row = lambda i: (i, 0)
    seq_tile = lambda i: (i // tiles_per_seq, 0, i % tiles_per_seq, 0)
    za_specs = [pl.BlockSpec((1, dil, tm // dil, gw), seq_tile) for _, dil in A_GROUPS]
    za_shapes = [jax.ShapeDtypeStruct((batch, dil, seq // dil, gw), BF16) for _, dil in A_GROUPS]
    return pl.pallas_call(
        functools.partial(_in_proj_kernel, tm=tm),
        grid=(n // tm,),
        in_specs=[pl.BlockSpec((tm, D_MODEL), row),
                  _const_spec((1, D_MODEL)),
                  _layer_spec(w_in, layer),
                  _const_spec(wvt.shape)],
        out_specs=za_specs + [pl.BlockSpec((tm, wb), row),
                              pl.BlockSpec((tm, wc), row),
                              pl.BlockSpec((1, C_HEADS * V_ROWS, tm),
                                           lambda i: (i // tiles_per_seq, 0, i % tiles_per_seq)),
                              pl.BlockSpec((tm, wg), row)],
        out_shape=za_shapes + [jax.ShapeDtypeStruct((n, wb), BF16),
                               jax.ShapeDtypeStruct((n, wc), BF16),
                               jax.ShapeDtypeStruct((batch, C_HEADS * V_ROWS, seq), BF16),
                               jax.ShapeDtypeStruct((n, wg), BF16)],
        scratch_shapes=[pltpu.VMEM((tm, D_MODEL), BF16), pltpu.VMEM((gw // LANE, tm, LANE), F32)],
        compiler_params=_cparams(("arbitrary",)),
        name="in_proj",
    )(x2d, g.reshape(1, D_MODEL), w_in, wvt)


A_TOKENS_PER_STEP = 2048
A_CHAINS = 4


def _mixer_a_kernel(z_ref, zp_ref, bias_ref, o_ref, lse_ref, *, dil, nsub):
    t = pl.program_id(1)
    blk, width = A_BLOCK, A_OUT
    kcols, vcols = slice(width, 2 * width), slice(2 * width, 3 * width)
    lane_head = lax.broadcasted_iota(jnp.int32, (blk, width), 1) // HEAD_DIM
    in_prev = (lax.broadcasted_iota(jnp.int32, (2 * blk, 2 * blk), 0) < blk).astype(F32)
    no_prev = in_prev * jnp.where(t == 0, NEG_INF, 0.0)
    transpose_v = lambda v: v.astype(F32).T.astype(BF16)

    ones_rows = jnp.ones((V_ROWS - HEAD_DIM, 2 * blk), BF16)
    blocks, windows = {}, {}

    def block_kv(r, j):
        if (r, j) not in blocks:
            src = zp_ref[0, r] if j < 0 else z_ref[0, r, j * blk:(j + 1) * blk, :]
            blocks[(r, j)] = (src[:, kcols], transpose_v(src[:, vcols]))
        return blocks[(r, j)]

    def window(r, j):
        if (r, j) not in windows:
            (k_prev, vt_prev), (k_cur, vt_cur) = block_kv(r, j - 1), block_kv(r, j)
            k_win = jnp.concatenate([k_prev, k_cur], axis=0)
            vt_win = jnp.concatenate([vt_prev, vt_cur], axis=1)
            windows[(r, j)] = (z_ref[0, r, j * blk:(j + 1) * blk, 0:width], k_win, vt_win)
        return windows[(r, j)]

    def logits(r, j, hp):
        q, k_win, _ = window(r, j)
        heads = (2 * hp, 2 * hp + 1)
        q2 = jnp.concatenate([jnp.where(lane_head == h, q, jnp.zeros_like(q)) for h in heads], axis=0)
        bias2 = jnp.concatenate([bias_ref[h, 0] for h in heads], axis=1)
        s = lax.dot_general(k_win, q2, NT_DIMS, preferred_element_type=F32) + bias2
        return s + no_prev if j == 0 else s

    def attend(r, j, hp, s):
        vt_win = window(r, j)[2]
        m = jnp.max(s, axis=0, keepdims=True)
        p = jnp.exp((s - m).astype(BF16))
        vt_ext = jnp.concatenate([part for h in (2 * hp, 2 * hp + 1)
                                  for part in (vt_win[h * HEAD_DIM:(h + 1) * HEAD_DIM, :], ones_rows)], axis=0)
        pv = jnp.dot(vt_ext, p, preferred_element_type=F32)
        res = []
        for u in range(2):
            cols = slice(u * blk, (u + 1) * blk)
            den = pv[u * V_ROWS + HEAD_DIM:u * V_ROWS + HEAD_DIM + 1, cols]
            res.append((pv[u * V_ROWS:u * V_ROWS + HEAD_DIM, cols] / den,
                        jnp.broadcast_to(m[:, cols] + jnp.log(den), (HEAD_DIM, blk))))
        return res

    n_hp = A_HPG // 2

    def chain(units):
        s_next = logits(*units[0])
        outs, lses = [], []
        for i, (r, j, hp) in enumerate(units):
            s_cur = s_next
            if i + 1 < len(units):
                s_next = logits(*units[i + 1])
            for o, lse in attend(r, j, hp, s_cur):
                outs.append(o)
                lses.append(lse)
            if hp == n_hp - 1:
                o_t = jnp.concatenate(outs, axis=0).T
                lse_t = jnp.concatenate(lses, axis=0).T
                outs, lses = [], []
                dst = slice(j * blk, (j + 1) * blk) if dil == 1 else pl.ds(j * blk * dil + r, blk, stride=dil)
                for c in range(width // LANE):
                    o_ref[c, dst, :] = o_t[:, c * LANE:(c + 1) * LANE]
                    lse_ref[c, dst, :] = lse_t[:, c * LANE:(c + 1) * LANE]
            yield

    units = [(r, j, hp) for r in range(dil) for j in range(nsub) for hp in range(n_hp)]
    per_chain = len(units) // A_CHAINS
    _run_round_robin(*[chain(units[i * per_chain:(i + 1) * per_chain]) for i in range(A_CHAINS)])


def _mixer_a_group(za, bias_a, g, dil, batch, seq):
    sub_len = seq // dil
    assert sub_len % A_BLOCK == 0
    tq = min(sub_len, A_TOKENS_PER_STEP // dil)
    assert tq % A_BLOCK == 0
    nsub = tq // A_BLOCK
    gw = 3 * A_OUT
    out_spec = pl.BlockSpec((A_OUT // LANE, tq * dil, LANE), lambda b, t: (0, b * (sub_len // tq) + t, 0))
    return pl.pallas_call(
        functools.partial(_mixer_a_kernel, dil=dil, nsub=nsub),
        grid=(batch, sub_len // tq),
        in_specs=[pl.BlockSpec((1, dil, tq, gw), lambda b, t: (b, 0, t, 0)),
                  pl.BlockSpec((1, dil, A_BLOCK, gw), lambda b, t: (b, 0, jnp.maximum(t * nsub - 1, 0), 0)),
                  pl.BlockSpec((A_HPG, 1, 2 * A_BLOCK, A_BLOCK), lambda b, t: (g, 0, 0, 0))],
        out_specs=[out_spec, out_spec],
        out_shape=[jax.ShapeDtypeStruct((A_OUT // LANE, batch * seq, LANE), F32)] * 2,
        compiler_params=_cparams(("arbitrary", "arbitrary")),
        name=f"mixer_a_d{dil}",
    )(za, za, bias_a)


SUBLANES = 8


def _mixer_b_kernel(z_ref, cw_ref, cb_ref, wax_ref, bax_ref, lam_ref, y_ref,
                    nat_scr, xe_scr, a_scr, b_scr, tail_scr, h_scr, *, ts):
    w = LRU_WIDTH
    seg = ts // SUBLANES
    halo = CONV_WIDTH - 1
    planes = w // LANE
    vrow = lambda i: slice(i * SUBLANES, (i + 1) * SUBLANES)
    sub = lax.broadcasted_iota(jnp.int32, (SUBLANES, w), 0)

    @pl.when(pl.program_id(1) == 0)
    def _():
        tail_scr[...] = jnp.zeros((halo * SUBLANES, w), F32)
        h_scr[...] = jnp.zeros((1, w), F32)

    x = z_ref[0, :, 0:w].astype(F32)
    for c in range(planes):
        nat_scr[c] = x[:, c * LANE:(c + 1) * LANE]
    for i in range(seg):
        for c in range(planes):
            xe_scr[vrow(halo + i), c * LANE:(c + 1) * LANE] = nat_scr[c, pl.ds(i, SUBLANES, stride=seg), :]
    for k in range(1, halo + 1):
        cur = xe_scr[vrow(halo + seg - k), :]
        prev = tail_scr[vrow(halo - k), :]
        xe_scr[vrow(halo - k), :] = jnp.where(sub == 0, pltpu.roll(prev, 1, 0), pltpu.roll(cur, 1, 0))
        tail_scr[vrow(halo - k), :] = cur
    xc = cb_ref[...]
    for j in range(CONV_WIDTH):
        start = (halo - (CONV_WIDTH - 1 - j)) * SUBLANES
        xc = xc + xe_scr[start:start + ts, :] * cw_ref[j:j + 1, :]

    ra = jnp.dot(xc.astype(BF16), wax_ref[...], preferred_element_type=F32) + bax_ref[...]
    r = jax.nn.sigmoid(ra[:, :w])
    ig = jax.nn.sigmoid(ra[:, w:])
    nl = -lam_ref[...]
    softplus = jnp.maximum(nl, 0.0) + jnp.log1p(jnp.exp(-jnp.abs(nl)))
    log_a = -LRU_C * r * softplus
    a = jnp.exp(log_a)
    a_scr[...] = a
    b_scr[...] = jnp.sqrt(-jnp.tanh(log_a) * (a * a + 1.0)) * (ig * xc)

    h = jnp.zeros((SUBLANES, w), F32)
    p = jnp.ones((SUBLANES, w), F32)
    for i in range(seg):
        a_i = a_scr[vrow(i), :]
        h = a_i * h + b_scr[vrow(i), :]
        p = a_i * p
        b_scr[vrow(i), :] = h
        a_scr[vrow(i), :] = p
    for s in (1, 2, 4):
        keep = sub >= s
        p_sh = jnp.where(keep, pltpu.roll(p, s, 0), 1.0)
        h_sh = jnp.where(keep, pltpu.roll(h, s, 0), 0.0)
        h = p * h_sh + h
        p = p * p_sh
    h_in = h_scr[...]
    after = p * h_in + h
    carry = jnp.where(sub == 0, h_in, pltpu.roll(after, 1, 0))
    h_scr[...] = after[SUBLANES - 1:SUBLANES, :]
    for i in range(seg):
        hs = b_scr[vrow(i), :] + a_scr[vrow(i), :] * carry
        for c in range(planes):
            nat_scr[c, pl.ds(i, SUBLANES, stride=seg), :] = hs[:, c * LANE:(c + 1) * LANE]
    h_nat = jnp.concatenate([nat_scr[c] for c in range(planes)], axis=1)

    gb = z_ref[0, :, w:2 * w].astype(F32)
    cdf = 0.5 * (1.0 + jnp.tanh(math.sqrt(2.0 / math.pi) * (gb + 0.044715 * (gb * gb * gb))))
    y_ref[0] = (h_nat * (gb * cdf)).astype(y_ref.dtype)


def _block_diag(wblocks):
    nb, di, do = wblocks.shape
    eye = jnp.eye(nb, dtype=wblocks.dtype)
    return (eye[:, None, :, None] * wblocks[:, :, None, :]).reshape(nb * di, nb * do)


def _mixer_b(zb, conv_w, conv_b, lru_wa, lru_ba, lru_wx, lru_bx, lru_lam, batch, seq, ts=512):
    w = LRU_WIDTH
    wax = jnp.concatenate([_block_diag(lru_wa), _block_diag(lru_wx)], axis=1).astype(BF16)
    bax = jnp.concatenate([lru_ba, lru_bx]).reshape(1, 2 * w)
    y = pl.pallas_call(
        functools.partial(_mixer_b_kernel, ts=ts),
        grid=(batch, seq // ts),
        in_specs=[pl.BlockSpec((1, ts, 2 * w), lambda b, t: (b, t, 0)),
                  _const_spec((CONV_WIDTH, w)), _const_spec((1, w)),
                  _const_spec((w, 2 * w)), _const_spec((1, 2 * w)), _const_spec((1, w))],
        out_specs=pl.BlockSpec((1, ts, w), lambda b, t: (b, t, 0)),
        out_shape=jax.ShapeDtypeStruct((batch, seq, w), BF16),
        scratch_shapes=[pltpu.VMEM((w // LANE, ts, LANE), F32),
                        pltpu.VMEM((ts + (CONV_WIDTH - 1) * SUBLANES, w), F32),
                        pltpu.VMEM((ts, w), F32), pltpu.VMEM((ts, w), F32),
                        pltpu.VMEM(((CONV_WIDTH - 1) * SUBLANES, w), F32), pltpu.VMEM((1, w), F32)],
        compiler_params=_cparams(("arbitrary", "arbitrary")),
        name="mixer_b",
    )(zb.reshape(batch, seq, 2 * w), conv_w, conv_b.reshape(1, w), wax, bax, lru_lam.reshape(1, w))
    return y.reshape(batch * seq, w)


HEAD_PAIR = LANE // HEAD_DIM
C_BLOCKS_PER_DOT = 1


def _mixer_c_prologue(q_ref, k_ref, qm_scr, sel_scr, *, nblk):
    blk = MOBA_BLOCK
    seq = nblk * blk
    kmean = jnp.concatenate(
        [jnp.sum(k_ref[0, n * blk:(n + 1) * blk, :].astype(F32), axis=0, keepdims=True) for n in range(nblk)],
        axis=0) * (1.0 / blk)
    km_hi = kmean.astype(BF16)
    km_lo = (kmean - km_hi.astype(F32)).astype(BF16)
    q = q_ref[0]
    lane = lax.broadcasted_iota(jnp.int32, q.shape, 1)
    km_lane = lax.broadcasted_iota(jnp.int32, kmean.shape, 1)
    blk_id = lax.broadcasted_iota(jnp.int32, (nblk, seq), 0)
    q_blk = lax.broadcasted_iota(jnp.int32, (nblk, seq), 1) // blk
    past = blk_id < q_blk
    in_head = lambda idx, hh: (idx >= hh * HEAD_DIM) & (idx < (hh + 1) * HEAD_DIM)
    km_rows = jnp.concatenate([jnp.where(in_head(km_lane, hh), part, jnp.zeros_like(part))
                               for hh in range(HEAD_PAIR) for part in (km_hi, km_lo)], axis=0)
    gates = lax.dot_general(km_rows, q, NT_DIMS, preferred_element_type=F32)
    for hh in range(HEAD_PAIR):
        qm_scr[hh] = jnp.where(in_head(lane, hh), q, jnp.zeros_like(q))
        gate = gates[2 * hh * nblk:(2 * hh + 1) * nblk] + gates[(2 * hh + 1) * nblk:(2 * hh + 2) * nblk]
        gate = jnp.where(past, gate, NEG_INF)
        sel = jnp.zeros((nblk, seq), F32)
        for _ in range(min(MOBA_TOPK, nblk - 1)):
            top = jnp.max(gate, axis=0, keepdims=True)
            first = jnp.min(jnp.where(gate == top, blk_id, nblk), axis=0, keepdims=True)
            pick = blk_id == first
            sel = jnp.where(pick, 1.0, sel)
            gate = jnp.where(pick, -jnp.inf, gate)
        sel_scr[hh] = jnp.where(past, sel, 0.0)


def _run_round_robin(*gens):
    gens = list(gens)
    while gens:
        gens = [g for g in gens if next(g, StopIteration) is not StopIteration]


def _mixer_c_block(c, k_ref, vt_ref, bias_ref, o_ref, qm_scr, sel_scr, s_scr, m_scr, p_scr, *, ndb):
    blk = MOBA_BLOCK
    par = c % 2
    qcols = slice(c * blk, (c + 1) * blk)
    fold = lambda t: t.reshape(blk // SUBLANES, SUBLANES, blk)
    heads = range(HEAD_PAIR)
    pairs = [(k0, min(k0 + C_BLOCKS_PER_DOT, c + 1))
             for k0 in range(0, c + 1, C_BLOCKS_PER_DOT)]
    picked = lambda hh, kb: sel_scr[hh, kb:kb + 1, qcols] > 0.0
    far_bias = lambda hh: bias_ref[hh, ndb, 0:1, :]

    def logits(hh):
        qm = qm_scr[hh, qcols, :]
        m8 = jnp.full((SUBLANES, blk), NEG_INF, F32)
        for k0, k1 in pairs:
            s = lax.dot_general(k_ref[0, k0 * blk:k1 * blk, :], qm, NT_DIMS, preferred_element_type=F32)
            for kb in range(k0, k1):
                su = s[(kb - k0) * blk:(kb - k0 + 1) * blk]
                if c - kb < ndb:
                    su = su + bias_ref[hh, c - kb]
                    t8 = jnp.max(fold(su), axis=0)
                else:
                    t8 = jnp.max(fold(su), axis=0) + far_bias(hh)
                s_scr[par, hh, kb * blk:(kb + 1) * blk, :] = su
                m8 = jnp.maximum(m8, t8 if kb == c else jnp.where(picked(hh, kb), t8, NEG_INF))
            yield
        m_scr[par, hh] = jnp.broadcast_to(jnp.max(m8, axis=0, keepdims=True), (SUBLANES, blk))

    outs = {}

    def attend(hh):
        m = m_scr[par, hh, 0:1, :]
        pv = jnp.zeros((V_ROWS, blk), F32)
        for k0, k1 in pairs:
            for kb in range(k0, k1):
                mk = m if kb == c else jnp.where(picked(hh, kb), m, -NEG_INF)
                if c - kb >= ndb:
                    mk = mk - far_bias(hh)
                x = s_scr[par, hh, kb * blk:(kb + 1) * blk, :] - mk
                p_scr[hh, kb * blk:(kb + 1) * blk, :] = jnp.exp2(x.astype(BF16))
            pv = pv + jnp.dot(vt_ref[0, hh * V_ROWS:(hh + 1) * V_ROWS, k0 * blk:k1 * blk],
                              p_scr[hh, k0 * blk:k1 * blk, :], preferred_element_type=F32)
            yield
        outs[hh] = pv[:HEAD_DIM] / pv[HEAD_DIM:HEAD_DIM + 1]
        if len(outs) == HEAD_PAIR:
            o_ref[0, qcols, :] = jnp.concatenate([outs[h] for h in heads], axis=0).T.astype(o_ref.dtype)

    return [logits(hh) for hh in heads], [attend(hh) for hh in heads]


def _mixer_c_kernel(q_ref, k_ref, vt_ref, bias_ref, o_ref, qm_scr, sel_scr, s_scr, m_scr, p_scr, *, nblk, ndb):
    _mixer_c_prologue(q_ref, k_ref, qm_scr, sel_scr, nblk=nblk)
    stages = functools.partial(_mixer_c_block, k_ref=k_ref, vt_ref=vt_ref, bias_ref=bias_ref, o_ref=o_ref,
                               qm_scr=qm_scr, sel_scr=sel_scr, s_scr=s_scr, m_scr=m_scr, p_scr=p_scr, ndb=ndb)

    def step(t):
        gens = []
        if t < nblk:
            gens += stages(t)[0]
        if t >= 1:
            gens += stages(t - 1)[1]
        _run_round_robin(*gens)

    def body(i, carry):
        for t in range(nblk + 1):
            pl.when(i == t)(functools.partial(step, t))
        return carry

    lax.fori_loop(0, nblk + 1, body, 0)


def _mixer_c(zc, vt, bias_c, batch, seq, ndb):
    blk = MOBA_BLOCK
    assert seq % blk == 0
    nblk = seq // blk
    npair = C_HEADS // HEAD_PAIR
    z = zc.reshape(batch, seq, 2 * C_WIDTH)
    o = pl.pallas_call(
        functools.partial(_mixer_c_kernel, nblk=nblk, ndb=ndb),
        grid=(batch, npair),
        in_specs=[pl.BlockSpec((1, seq, LANE), lambda b, hp: (b, 0, hp)),
                  pl.BlockSpec((1, seq, LANE), lambda b, hp: (b, 0, npair + hp)),
                  pl.BlockSpec((1, HEAD_PAIR * V_ROWS, seq), lambda b, hp: (b, hp, 0)),
                  pl.BlockSpec((HEAD_PAIR, ndb + 1, blk, blk), lambda b, hp: (hp, 0, 0, 0))],
        out_specs=pl.BlockSpec((1, seq, LANE), lambda b, hp: (b, 0, hp)),
        out_shape=jax.ShapeDtypeStruct((batch, seq, C_WIDTH), BF16),
        scratch_shapes=[pltpu.VMEM((HEAD_PAIR, seq, LANE), BF16), pltpu.VMEM((HEAD_PAIR, nblk, seq), F32),
                        pltpu.VMEM((2, HEAD_PAIR, seq, blk), F32), pltpu.VMEM((2, HEAD_PAIR, SUBLANES, blk), F32),
                        pltpu.VMEM((HEAD_PAIR, seq, blk), BF16)],
        compiler_params=_cparams(("arbitrary", "arbitrary")),
        name="mixer_c",
    )(z, z, vt, bias_c)
    return o.reshape(batch * seq, C_WIDTH)


def _merge_kernel(x_ref, o1_ref, o2_ref, o3_ref, l1_ref, l2_ref, l3_ref, ob_ref, oc_ref, zg_ref,
                  pa_ref, pb_ref, pc_ref, wo_ref, out_ref):
    d = D_MODEL
    halves = lambda ref: jnp.concatenate([ref[c] for c in range(ref.shape[0])], axis=-1)
    l1, l2, l3 = halves(l1_ref), halves(l2_ref), halves(l3_ref)
    m = jnp.maximum(jnp.maximum(l1, l2), l3)
    e1, e2, e3 = jnp.exp(l1 - m), jnp.exp(l2 - m), jnp.exp(l3 - m)
    o_a = (e1 * halves(o1_ref) + e2 * halves(o2_ref) + e3 * halves(o3_ref)) / (e1 + e2 + e3)

    def branch(o, p_ref, k):
        gate = jax.nn.sigmoid(zg_ref[:, k * d:(k + 1) * d].astype(F32))
        return gate * jnp.dot(o, p_ref[...], preferred_element_type=F32)

    merged = branch(o_a.astype(BF16), pa_ref, 0) + branch(ob_ref[...], pb_ref, 1) + branch(oc_ref[...], pc_ref, 2)
    out_ref[...] = x_ref[...] + jnp.dot(merged.astype(BF16), wo_ref[...], preferred_element_type=F32)


def _merge(x2d, oa, lse, ob, oc, zg, p_a, p_b, p_c, w_out, layer, tm=512):
    n = x2d.shape[0]
    row = lambda width: pl.BlockSpec((tm, width), lambda i: (i, 0))
    return pl.pallas_call(
        _merge_kernel,
        grid=(n // tm,),
        in_specs=[row(D_MODEL)] + [pl.BlockSpec((A_OUT // LANE, tm, LANE), lambda i: (0, i, 0))] * 6
                 + [row(LRU_WIDTH), row(C_WIDTH), row(3 * D_MODEL),
                  _layer_spec(p_a, layer), _layer_spec(p_b, layer), _layer_spec(p_c, layer),
                  _layer_spec(w_out, layer)],
        out_specs=row(D_MODEL),
        out_shape=jax.ShapeDtypeStruct((n, D_MODEL), F32),
        compiler_params=_cparams(("arbitrary",)),
        name="merge",
    )(x2d, *oa, *lse, ob, oc, zg, p_a, p_b, p_c, w_out)


def _rms(x, g):
    return x * lax.rsqrt(jnp.mean(x * x, axis=-1, keepdims=True) + EPS) * g


def _ffn_kernel(x_ref, g_ref, wgu_ref, wd_ref, gf_ref, out_ref, h_scr, act_scr, *, final_norm):
    x = x_ref[...]
    h_scr[...] = _rms(x, g_ref[...]).astype(BF16)
    for off, w in _col_chunks(0, FFN_HIDDEN, 256):
        gate = jnp.dot(h_scr[...], wgu_ref[:, off:off + w], preferred_element_type=F32)
        up = jnp.dot(h_scr[...], wgu_ref[:, FFN_HIDDEN + off:FFN_HIDDEN + off + w], preferred_element_type=F32)
        act_scr[:, off:off + w] = (gate * jax.nn.sigmoid(gate) * up).astype(BF16)
    y = x + jnp.dot(act_scr[...], wd_ref[...], preferred_element_type=F32)
    if final_norm:
        y = _rms(y, gf_ref[...])
    out_ref[...] = y


def _ffn(x2d, g_ffn, w_gu, w_down, layer, g_final, final_norm, tm=512):
    n = x2d.shape[0]
    row = pl.BlockSpec((tm, D_MODEL), lambda i: (i, 0))
    return pl.pallas_call(
        functools.partial(_ffn_kernel, final_norm=final_norm),
        grid=(n // tm,),
        in_specs=[row, _const_spec((1, D_MODEL)), _layer_spec(w_gu, layer), _layer_spec(w_down, layer),
                  _const_spec((1, D_MODEL))],
        out_specs=row,
        out_shape=jax.ShapeDtypeStruct((n, D_MODEL), F32),
        scratch_shapes=[pltpu.VMEM((tm, D_MODEL), BF16), pltpu.VMEM((tm, FFN_HIDDEN), BF16)],
        compiler_params=_cparams(("arbitrary",)),
        name="ffn",
    )(x2d, g_ffn.reshape(1, D_MODEL), w_gu, w_down, g_final.reshape(1, D_MODEL))


def kernel(x, rel_bias, g_mix, w_in, conv_w, conv_b, lru_wa, lru_ba, lru_wx, lru_bx, lru_lam,
           p_a, p_b, p_c, w_out, g_ffn, w_gu, w_down, g_final):
    batch, seq, d = x.shape
    assert d == D_MODEL and w_in.shape[-1] == IN_COLS
    depth = w_in.shape[0]
    nblk = seq // MOBA_BLOCK
    ndb = _moba_far_blocks(nblk)

    bias_a = _build_bias(rel_bias, _bucket_index_a(), A_HEADS, 0, A_HPG)
    bias_c = _build_bias(rel_bias, _bucket_index_c(ndb), C_HEADS, A_HEADS, None, LOG2E)

    w_in, p_a, p_b, p_c, w_out, w_gu, w_down = (
        t.astype(BF16) for t in (w_in, p_a, p_b, p_c, w_out, w_gu, w_down))

    x2d = x.reshape(batch * seq, d)
    for l in range(depth):
        *za, zb, zc, vt, zg = _in_proj(x2d, g_mix[l], w_in, l, batch, seq)
        oa, lse = zip(*[_mixer_a_group(za[g], bias_a, g, dil, batch, seq)
                        for g, (_, dil) in enumerate(A_GROUPS)])
        ob = _mixer_b(zb, conv_w[l], conv_b[l], lru_wa[l], lru_ba[l], lru_wx[l], lru_bx[l], lru_lam[l],
                      batch, seq)
        oc = _mixer_c(zc, vt, bias_c, batch, seq, ndb)
        x2d = _merge(x2d, oa, lse, ob, oc, zg, p_a, p_b, p_c, w_out, l)
        x2d = _ffn(x2d, g_ffn[l], w_gu, w_down, l, g_final, final_norm=(l == depth - 1))
    return x2d.reshape(batch, seq, d)
```

```python
import functools
import math

import numpy as np
import jax
import jax.numpy as jnp
from jax import lax
from jax.experimental import pallas as pl
from jax.experimental.pallas import tpu as pltpu

F32 = jnp.float32
BF16 = jnp.bfloat16

D_MODEL = 1024
HEAD_DIM = 64
SCALE = HEAD_DIM ** -0.5
LOG2E = math.log2(math.e)
NEG_INF = -1e30
EPS = 1e-6
A_GROUPS = ((128, 1), (512, 4), (2048, 16))
A_HPG = 4
A_HEADS = A_HPG * len(A_GROUPS)
A_WIDTH = A_HEADS * HEAD_DIM
A_OUT = A_HPG * HEAD_DIM
A_BLOCK = 128
LRU_WIDTH = D_MODEL // 2
LRU_BLOCKS = 8
CONV_WIDTH = 4
LRU_C = 8.0
C_HEADS = 8
C_WIDTH = C_HEADS * HEAD_DIM
MOBA_BLOCK = 256
MOBA_TOPK = 3
REL_BUCKETS = 32
REL_MAX_DIST = 2048
FFN_HIDDEN = 2816
IN_COLS = 3 * A_WIDTH + 2 * LRU_WIDTH + 3 * C_WIDTH + 3 * D_MODEL
V_C_OFF = 3 * A_WIDTH + 2 * LRU_WIDTH + 2 * C_WIDTH

LANE = 128
V_ROWS = HEAD_DIM + 16
VMEM_LIMIT = 56 * 1024 * 1024

NT_DIMS = (((1,), (1,)), ((), ()))


def _cparams(sem, vmem=VMEM_LIMIT):
    return pltpu.CompilerParams(dimension_semantics=sem, vmem_limit_bytes=vmem)


def _const_spec(shape):
    nd = len(shape)
    return pl.BlockSpec(shape, lambda *_: (0,) * nd, pipeline_mode=pl.Buffered(1))


def _layer_spec(stacked, layer):
    return pl.BlockSpec((None,) + stacked.shape[1:], lambda *_: (layer, 0, 0), pipeline_mode=pl.Buffered(1))


def _rel_bucket_np(dist):
    max_exact = REL_BUCKETS // 2
    d = np.maximum(dist, 0)
    df = np.maximum(d, 1).astype(np.float32)
    large = max_exact + (np.log(df / np.float32(max_exact)) / np.float32(math.log(REL_MAX_DIST / max_exact))
                         * np.float32(REL_BUCKETS - max_exact)).astype(np.int32)
    large = np.minimum(large, REL_BUCKETS - 1)
    return np.where(d < max_exact, d, large).astype(np.int32)


def _bucket_index_a():
    qi = np.arange(A_BLOCK)[None, :] + A_BLOCK
    kj = np.arange(2 * A_BLOCK)[:, None]
    delta = qi - kj
    mats = []
    for window, dil in A_GROUPS:
        band = (delta >= 0) & (delta <= window // dil)
        mats.append(np.where(band, _rel_bucket_np(delta * dil), -1))
    return np.stack(mats).astype(np.int32)


def _moba_far_blocks(nblk):
    for db in range(1, nblk + 1):
        lo = db * MOBA_BLOCK - (MOBA_BLOCK - 1)
        if np.all(_rel_bucket_np(np.arange(lo, nblk * MOBA_BLOCK)) == REL_BUCKETS - 1):
            return db
    return nblk


def _bucket_index_c(ndb):
    k = np.arange(MOBA_BLOCK)[:, None]
    q = np.arange(MOBA_BLOCK)[None, :]
    mats = []
    for db in range(ndb + 1):
        dist = db * MOBA_BLOCK + q - k
        b = _rel_bucket_np(dist)
        if db == 0:
            b = np.where(dist >= 0, b, -1)
        mats.append(b)
    return np.stack(mats).astype(np.int32)


def _bias_kernel(tab_ref, idx_ref, out_ref, *, head_off, unit, present):
    h = pl.program_id(0) + head_off
    for m, buckets in enumerate(present):
        idx = idx_ref[m]
        out = jnp.full(idx.shape, NEG_INF, F32)
        for b in buckets:
            out = jnp.where(idx == b, tab_ref[b, h] * unit, out)
        out_ref[0, m] = out


def _build_bias(rel_bias, idx, n_heads, head_off, heads_per_idx_group, unit=1.0):
    r, c = idx.shape[-2:]
    if heads_per_idx_group is None:
        n_mats = idx.shape[0]
        idx_map = lambda h: (0, 0, 0)
        present = tuple(tuple(int(b) for b in np.unique(m[m >= 0])) for m in idx)
    else:
        n_mats = 1
        idx_map = lambda h: (h // heads_per_idx_group, 0, 0)
        present = (tuple(int(b) for b in np.unique(idx[idx >= 0])),)
    return pl.pallas_call(
        functools.partial(_bias_kernel, head_off=head_off, unit=unit, present=present),
        grid=(n_heads,),
        in_specs=[pl.BlockSpec(memory_space=pltpu.SMEM),
                  pl.BlockSpec((n_mats, r, c), idx_map)],
        out_specs=pl.BlockSpec((1, n_mats, r, c), lambda h: (h, 0, 0, 0)),
        out_shape=jax.ShapeDtypeStruct((n_heads, n_mats, r, c), F32),
        compiler_params=_cparams(("arbitrary",)),
        name="rel_bias_build",
    )(rel_bias, jnp.asarray(idx))


def _col_chunks(start, stop, width=512):
    out = []
    while start < stop:
        w = min(width, stop - start)
        out.append((start, w))
        start += w
    return out


def _in_proj_kernel(x_ref, g_ref, w_ref, wvt_ref, za0_ref, za1_ref, za2_ref, zb_ref, zc_ref, vt_ref, zg_ref,
                    h_scr, dil_scr, *, tm):
    x = x_ref[...]
    ms = jnp.mean(x * x, axis=-1, keepdims=True)
    h_scr[...] = (x * lax.rsqrt(ms + EPS) * g_ref[...]).astype(BF16)

    def proj(base, width):
        return jnp.dot(h_scr[...], w_ref[:, base:base + width], preferred_element_type=F32)

    def seg(out_ref, base, width, scaled_cols=0, scale=1.0, chunk=512):
        assert scaled_cols % chunk == 0
        for off, w in _col_chunks(0, width, chunk):
            r = proj(base + off, w)
            if off < scaled_cols:
                r = r * scale
            out_ref[:, off:off + w] = r.astype(out_ref.dtype)

    for g, (za_ref, (_, dil)) in enumerate(zip((za0_ref, za1_ref, za2_ref), A_GROUPS)):
        qkv = [proj(s * A_WIDTH + g * A_OUT, A_OUT) for s in range(3)]
        qkv[0] = qkv[0] * SCALE
        if dil == 1:
            for s, part in enumerate(qkv):
                za_ref[0, 0, :, s * A_OUT:(s + 1) * A_OUT] = part.astype(BF16)
        else:
            planes = [part[:, c * LANE:(c + 1) * LANE] for part in qkv for c in range(A_OUT // LANE)]
            for c, plane in enumerate(planes):
                dil_scr[c] = plane
            for r in range(dil):
                for c in range(len(planes)):
                    za_ref[0, r, :, c * LANE:(c + 1) * LANE] = (
                        dil_scr[c, pl.ds(r, tm // dil, stride=dil), :].astype(BF16))

    wa, wb, wc = 3 * A_WIDTH, 2 * LRU_WIDTH, 2 * C_WIDTH
    seg(zb_ref, wa, wb)
    seg(zc_ref, wa + wb, wc, C_WIDTH, SCALE * LOG2E)
    seg(zg_ref, V_C_OFF + C_WIDTH, 3 * D_MODEL)
    vt = lax.dot_general(wvt_ref[...], h_scr[...], NT_DIMS, preferred_element_type=F32).astype(BF16)
    for h in range(C_HEADS):
        vt_ref[0, h * V_ROWS:h * V_ROWS + HEAD_DIM, :] = vt[h * HEAD_DIM:(h + 1) * HEAD_DIM]
        vt_ref[0, h * V_ROWS + HEAD_DIM:(h + 1) * V_ROWS, :] = jnp.ones((V_ROWS - HEAD_DIM, tm), BF16)


def _in_proj(x2d, g, w_in, layer, batch, seq, tm=512):
    n = x2d.shape[0]
    wvt = w_in[layer, :, V_C_OFF:V_C_OFF + C_WIDTH].T
    wb, wc, wg = 2 * LRU_WIDTH, 2 * C_WIDTH, 3 * D_MODEL
    gw = 3 * A_OUT
    tiles_per_seq = seq // tm
    row = lambda i: (i, 0)
    seq_tile = lambda i: (i // tiles_per_seq, 0, i % tiles_per_seq, 0)
    za_specs = [pl.BlockSpec((1, dil, tm // dil, gw), seq_tile) for _, dil in A_GROUPS]
    za_shapes = [jax.ShapeDtypeStruct((batch, dil, seq // dil, gw), BF16) for _, dil in A_GROUPS]
    return pl.pallas_call(
        functools.partial(_in_proj_kernel, tm=tm),
        grid=(n // tm,),
        in_specs=[pl.BlockSpec((tm, D_MODEL), row),
                  _const_spec((1, D_MODEL)),
                  _layer_spec(w_in, layer),
                  _const_spec(wvt.shape)],
        out_specs=za_specs + [pl.BlockSpec((tm, wb), row),
                              pl.BlockSpec((tm, wc), row),
                              pl.BlockSpec((1, C_HEADS * V_ROWS, tm),
                                           lambda i: (i // tiles_per_seq, 0, i % tiles_per_seq)),
                              pl.BlockSpec((tm, wg), row)],
        out_shape=za_shapes + [jax.ShapeDtypeStruct((n, wb), BF16),
                               jax.ShapeDtypeStruct((n, wc), BF16),
                               jax.ShapeDtypeStruct((batch, C_HEADS * V_ROWS, seq), BF16),
                               jax.ShapeDtypeStruct((n, wg), BF16)],
        scratch_shapes=[pltpu.VMEM((tm, D_MODEL), BF16), pltpu.VMEM((gw // LANE, tm, LANE), F32)],
        compiler_params=_cparams(("arbitrary",)),
        name="in_proj",
    )(x2d, g.reshape(1, D_MODEL), w_in, wvt)


A_TOKENS_PER_STEP = 2048
A_CHAINS = 4


def _mixer_a_kernel(z_ref, zp_ref, bias_ref, o_ref, lse_ref, *, dil, nsub):
    t = pl.program_id(1)
    blk, width = A_BLOCK, A_OUT
    kcols, vcols = slice(width, 2 * width), slice(2 * width, 3 * width)
    lane_head = lax.broadcasted_iota(jnp.int32, (blk, width), 1) // HEAD_DIM
    in_prev = (lax.broadcasted_iota(jnp.int32, (2 * blk, 2 * blk), 0) < blk).astype(F32)
    no_prev = in_prev * jnp.where(t == 0, NEG_INF, 0.0)
    transpose_v = lambda v: v.astype(F32).T.astype(BF16)

    ones_rows = jnp.ones((V_ROWS - HEAD_DIM, 2 * blk), BF16)
    blocks, windows = {}, {}

    def block_kv(r, j):
        if (r, j) not in blocks:
            src = zp_ref[0, r] if j < 0 else z_ref[0, r, j * blk:(j + 1) * blk, :]
            blocks[(r, j)] = (src[:, kcols], transpose_v(src[:, vcols]))
        return blocks[(r, j)]

    def window(r, j):
        if (r, j) not in windows:
            (k_prev, vt_prev), (k_cur, vt_cur) = block_kv(r, j - 1), block_kv(r, j)
            k_win = jnp.concatenate([k_prev, k_cur], axis=0)
            vt_win = jnp.concatenate([vt_prev, vt_cur], axis=1)
            windows[(r, j)] = (z_ref[0, r, j * blk:(j + 1) * blk, 0:width], k_win, vt_win)
        return windows[(r, j)]

    def logits(r, j, hp):
        q, k_win, _ = window(r, j)
        heads = (2 * hp, 2 * hp + 1)
        q2 = jnp.concatenate([jnp.where(lane_head == h, q, jnp.zeros_like(q)) for h in heads], axis=0)
        bias2 = jnp.concatenate([bias_ref[h, 0] for h in heads], axis=1)
        s = lax.dot_general(k_win, q2, NT_DIMS, preferred_element_type=F32) + bias2
        return s + no_prev if j == 0 else s

    def attend(r, j, hp, s):
        vt_win = window(r, j)[2]
        m = jnp.max(s, axis=0, keepdims=True)
        p = jnp.exp((s - m).astype(BF16))
        vt_ext = jnp.concatenate([part for h in (2 * hp, 2 * hp + 1)
                                  for part in (vt_win[h * HEAD_DIM:(h + 1) * HEAD_DIM, :], ones_rows)], axis=0)
        pv = jnp.dot(vt_ext, p, preferred_element_type=F32)
        res = []
        for u in range(2):
            cols = slice(u * blk, (u + 1) * blk)
            den = pv[u * V_ROWS + HEAD_DIM:u * V_ROWS + HEAD_DIM + 1, cols]
            res.append((pv[u * V_ROWS:u * V_ROWS + HEAD_DIM, cols] / den,
                        jnp.broadcast_to(m[:, cols] + jnp.log(den), (HEAD_DIM, blk))))
        return res

    n_hp = A_HPG // 2

    def chain(units):
        s_next = logits(*units[0])
        outs, lses = [], []
        for i, (r, j, hp) in enumerate(units):
            s_cur = s_next
            if i + 1 < len(units):
                s_next = logits(*units[i + 1])
            for o, lse in attend(r, j, hp, s_cur):
                outs.append(o)
                lses.append(lse)
            if hp == n_hp - 1:
                o_t = jnp.concatenate(outs, axis=0).T
                lse_t = jnp.concatenate(lses, axis=0).T
                outs, lses = [], []
                dst = slice(j * blk, (j + 1) * blk) if dil == 1 else pl.ds(j * blk * dil + r, blk, stride=dil)
                for c in range(width // LANE):
                    o_ref[c, dst, :] = o_t[:, c * LANE:(c + 1) * LANE]
                    lse_ref[c, dst, :] = lse_t[:, c * LANE:(c + 1) * LANE]
            yield

    units = [(r, j, hp) for r in range(dil) for j in range(nsub) for hp in range(n_hp)]
    per_chain = len(units) // A_CHAINS
    _run_round_robin(*[chain(units[i * per_chain:(i + 1) * per_chain]) for i in range(A_CHAINS)])


def _mixer_a_group(za, bias_a, g, dil, batch, seq):
    sub_len = seq // dil
    assert sub_len % A_BLOCK == 0
    tq = min(sub_len, A_TOKENS_PER_STEP // dil)
    assert tq % A_BLOCK == 0
    nsub = tq // A_BLOCK
    gw = 3 * A_OUT
    out_spec = pl.BlockSpec((A_OUT // LANE, tq * dil, LANE), lambda b, t: (0, b * (sub_len // tq) + t, 0))
    return pl.pallas_call(
        functools.partial(_mixer_a_kernel, dil=dil, nsub=nsub),
        grid=(batch, sub_len // tq),
        in_specs=[pl.BlockSpec((1, dil, tq, gw), lambda b, t: (b, 0, t, 0)),
                  pl.BlockSpec((1, dil, A_BLOCK, gw), lambda b, t: (b, 0, jnp.maximum(t * nsub - 1, 0), 0)),
                  pl.BlockSpec((A_HPG, 1, 2 * A_BLOCK, A_BLOCK), lambda b, t: (g, 0, 0, 0))],
        out_specs=[out_spec, out_spec],
        out_shape=[jax.ShapeDtypeStruct((A_OUT // LANE, batch * seq, LANE), F32)] * 2,
        compiler_params=_cparams(("arbitrary", "arbitrary")),
        name=f"mixer_a_d{dil}",
    )(za, za, bias_a)


SUBLANES = 8


def _mixer_b_kernel(z_ref, cw_ref, cb_ref, wax_ref, bax_ref, lam_ref, y_ref,
                    nat_scr, xe_scr, a_scr, b_scr, tail_scr, h_scr, *, ts):
    w = LRU_WIDTH
    seg = ts // SUBLANES
    halo = CONV_WIDTH - 1
    planes = w // LANE
    vrow = lambda i: slice(i * SUBLANES, (i + 1) * SUBLANES)
    sub = lax.broadcasted_iota(jnp.int32, (SUBLANES, w), 0)

    @pl.when(pl.program_id(1) == 0)
    def _():
        tail_scr[...] = jnp.zeros((halo * SUBLANES, w), F32)
        h_scr[...] = jnp.zeros((1, w), F32)

    x = z_ref[0, :, 0:w].astype(F32)
    for c in range(planes):
        nat_scr[c] = x[:, c * LANE:(c + 1) * LANE]
    for i in range(seg):
        for c in range(planes):
            xe_scr[vrow(halo + i), c * LANE:(c + 1) * LANE] = nat_scr[c, pl.ds(i, SUBLANES, stride=seg), :]
    for k in range(1, halo + 1):
        cur = xe_scr[vrow(halo + seg - k), :]
        prev = tail_scr[vrow(halo - k), :]
        xe_scr[vrow(halo - k), :] = jnp.where(sub == 0, pltpu.roll(prev, 1, 0), pltpu.roll(cur, 1, 0))
        tail_scr[vrow(halo - k), :] = cur
    xc = cb_ref[...]
    for j in range(CONV_WIDTH):
        start = (halo - (CONV_WIDTH - 1 - j)) * SUBLANES
        xc = xc + xe_scr[start:start + ts, :] * cw_ref[j:j + 1, :]

    ra = jnp.dot(xc.astype(BF16), wax_ref[...], preferred_element_type=F32) + bax_ref[...]
    r = jax.nn.sigmoid(ra[:, :w])
    ig = jax.nn.sigmoid(ra[:, w:])
    nl = -lam_ref[...]
    softplus = jnp.maximum(nl, 0.0) + jnp.log1p(jnp.exp(-jnp.abs(nl)))
    log_a = -LRU_C * r * softplus
    a = jnp.exp(log_a)
    a_scr[...] = a
    b_scr[...] = jnp.sqrt(-jnp.tanh(log_a) * (a * a + 1.0)) * (ig * xc)

    h = jnp.zeros((SUBLANES, w), F32)
    p = jnp.ones((SUBLANES, w), F32)
    for i in range(seg):
        a_i = a_scr[vrow(i), :]
        h = a_i * h + b_scr[vrow(i), :]
        p = a_i * p
        b_scr[vrow(i), :] = h
        a_scr[vrow(i), :] = p
    for s in (1, 2, 4):
        keep = sub >= s
        p_sh = jnp.where(keep, pltpu.roll(p, s, 0), 1.0)
        h_sh = jnp.where(keep, pltpu.roll(h, s, 0), 0.0)
        h = p * h_sh + h
        p = p * p_sh
    h_in = h_scr[...]
    after = p * h_in + h
    carry = jnp.where(sub == 0, h_in, pltpu.roll(after, 1, 0))
    h_scr[...] = after[SUBLANES - 1:SUBLANES, :]
    for i in range(seg):
        hs = b_scr[vrow(i), :] + a_scr[vrow(i), :] * carry
        for c in range(planes):
            nat_scr[c, pl.ds(i, SUBLANES, stride=seg), :] = hs[:, c * LANE:(c + 1) * LANE]
    h_nat = jnp.concatenate([nat_scr[c] for c in range(planes)], axis=1)

    gb = z_ref[0, :, w:2 * w].astype(F32)
    cdf = 0.5 * (1.0 + jnp.tanh(math.sqrt(2.0 / math.pi) * (gb + 0.044715 * (gb * gb * gb))))
    y_ref[0] = (h_nat * (gb * cdf)).astype(y_ref.dtype)


def _block_diag(wblocks):
    nb, di, do = wblocks.shape
    eye = jnp.eye(nb, dtype=wblocks.dtype)
    return (eye[:, None, :, None] * wblocks[:, :, None, :]).reshape(nb * di, nb * do)


def _mixer_b(zb, conv_w, conv_b, lru_wa, lru_ba, lru_wx, lru_bx, lru_lam, batch, seq, ts=512):
    w = LRU_WIDTH
    wax = jnp.concatenate([_block_diag(lru_wa), _block_diag(lru_wx)], axis=1).astype(BF16)
    bax = jnp.concatenate([lru_ba, lru_bx]).reshape(1, 2 * w)
    y = pl.pallas_call(
        functools.partial(_mixer_b_kernel, ts=ts),
        grid=(batch, seq // ts),
        in_specs=[pl.BlockSpec((1, ts, 2 * w), lambda b, t: (b, t, 0)),
                  _const_spec((CONV_WIDTH, w)), _const_spec((1, w)),
                  _const_spec((w, 2 * w)), _const_spec((1, 2 * w)), _const_spec((1, w))],
        out_specs=pl.BlockSpec((1, ts, w), lambda b, t: (b, t, 0)),
        out_shape=jax.ShapeDtypeStruct((batch, seq, w), BF16),
        scratch_shapes=[pltpu.VMEM((w // LANE, ts, LANE), F32),
                        pltpu.VMEM((ts + (CONV_WIDTH - 1) * SUBLANES, w), F32),
                        pltpu.VMEM((ts, w), F32), pltpu.VMEM((ts, w), F32),
                        pltpu.VMEM(((CONV_WIDTH - 1) * SUBLANES, w), F32), pltpu.VMEM((1, w), F32)],
        compiler_params=_cparams(("arbitrary", "arbitrary")),
        name="mixer_b",
    )(zb.reshape(batch, seq, 2 * w), conv_w, conv_b.reshape(1, w), wax, bax, lru_lam.reshape(1, w))
    return y.reshape(batch * seq, w)


HEAD_PAIR = LANE // HEAD_DIM
C_BLOCKS_PER_DOT = 1


def _mixer_c_prologue(q_ref, k_ref, qm_scr, sel_scr, *, nblk):
    blk = MOBA_BLOCK
    seq = nblk * blk
    kmean = jnp.concatenate(
        [jnp.sum(k_ref[0, n * blk:(n + 1) * blk, :].astype(F32), axis=0, keepdims=True) for n in range(nblk)],
        axis=0) * (1.0 / blk)
    km_hi = kmean.astype(BF16)
    km_lo = (kmean - km_hi.astype(F32)).astype(BF16)
    q = q_ref[0]
    lane = lax.broadcasted_iota(jnp.int32, q.shape, 1)
    km_lane = lax.broadcasted_iota(jnp.int32, kmean.shape, 1)
    blk_id = lax.broadcasted_iota(jnp.int32, (nblk, seq), 0)
    q_blk = lax.broadcasted_iota(jnp.int32, (nblk, seq), 1) // blk
    past = blk_id < q_blk
    in_head = lambda idx, hh: (idx >= hh * HEAD_DIM) & (idx < (hh + 1) * HEAD_DIM)
    km_rows = jnp.concatenate([jnp.where(in_head(km_lane, hh), part, jnp.zeros_like(part))
                               for hh in range(HEAD_PAIR) for part in (km_hi, km_lo)], axis=0)
    gates = lax.dot_general(km_rows, q, NT_DIMS, preferred_element_type=F32)
    for hh in range(HEAD_PAIR):
        qm_scr[hh] = jnp.where(in_head(lane, hh), q, jnp.zeros_like(q))
        gate = gates[2 * hh * nblk:(2 * hh + 1) * nblk] + gates[(2 * hh + 1) * nblk:(2 * hh + 2) * nblk]
        gate = jnp.where(past, gate, NEG_INF)
        sel = jnp.zeros((nblk, seq), F32)
        for _ in range(min(MOBA_TOPK, nblk - 1)):
            top = jnp.max(gate, axis=0, keepdims=True)
            first = jnp.min(jnp.where(gate == top, blk_id, nblk), axis=0, keepdims=True)
            pick = blk_id == first
            sel = jnp.where(pick, 1.0, sel)
            gate = jnp.where(pick, -jnp.inf, gate)
        sel_scr[hh] = jnp.where(past, sel, 0.0)


def _run_round_robin(*gens):
    gens = list(gens)
    while gens:
        gens = [g for g in gens if next(g, StopIteration) is not StopIteration]


def _mixer_c_block(c, k_ref, vt_ref, bias_ref, o_ref, qm_scr, sel_scr, s_scr, m_scr, p_scr, *, ndb):
    blk = MOBA_BLOCK
    par = c % 2
    qcols = slice(c * blk, (c + 1) * blk)
    fold = lambda t: t.reshape(blk // SUBLANES, SUBLANES, blk)
    heads = range(HEAD_PAIR)
    pairs = [(k0, min(k0 + C_BLOCKS_PER_DOT, c + 1))
             for k0 in range(0, c + 1, C_BLOCKS_PER_DOT)]
    picked = lambda hh, kb: sel_scr[hh, kb:kb + 1, qcols] > 0.0
    far_bias = lambda hh: bias_ref[hh, ndb, 0:1, :]

    def logits(hh):
        qm = qm_scr[hh, qcols, :]
        m8 = jnp.full((SUBLANES, blk), NEG_INF, F32)
        for k0, k1 in pairs:
            s = lax.dot_general(k_ref[0, k0 * blk:k1 * blk, :], qm, NT_DIMS, preferred_element_type=F32)
            for kb in range(k0, k1):
                su = s[(kb - k0) * blk:(kb - k0 + 1) * blk]
                if c - kb < ndb:
                    su = su + bias_ref[hh, c - kb]
                    t8 = jnp.max(fold(su), axis=0)
                else:
                    t8 = jnp.max(fold(su), axis=0) + far_bias(hh)
                s_scr[par, hh, kb * blk:(kb + 1) * blk, :] = su
                m8 = jnp.maximum(m8, t8 if kb == c else jnp.where(picked(hh, kb), t8, NEG_INF))
            yield
        m_scr[par, hh] = jnp.broadcast_to(jnp.max(m8, axis=0, keepdims=True), (SUBLANES, blk))

    outs = {}

    def attend(hh):
        m = m_scr[par, hh, 0:1, :]
        pv = jnp.zeros((V_ROWS, blk), F32)
        for k0, k1 in pairs:
            for kb in range(k0, k1):
                mk = m if kb == c else jnp.where(picked(hh, kb), m, -NEG_INF)
                if c - kb >= ndb:
                    mk = mk - far_bias(hh)
                x = s_scr[par, hh, kb * blk:(kb + 1) * blk, :] - mk
                p_scr[hh, kb * blk:(kb + 1) * blk, :] = jnp.exp2(x.astype(BF16))
            pv = pv + jnp.dot(vt_ref[0, hh * V_ROWS:(hh + 1) * V_ROWS, k0 * blk:k1 * blk],
                              p_scr[hh, k0 * blk:k1 * blk, :], preferred_element_type=F32)
            yield
        outs[hh] = pv[:HEAD_DIM] / pv[HEAD_DIM:HEAD_DIM + 1]
        if len(outs) == HEAD_PAIR:
            o_ref[0, qcols, :] = jnp.concatenate([outs[h] for h in heads], axis=0).T.astype(o_ref.dtype)

    return [logits(hh) for hh in heads], [attend(hh) for hh in heads]


def _mixer_c_kernel(q_ref, k_ref, vt_ref, bias_ref, o_ref, qm_scr, sel_scr, s_scr, m_scr, p_scr, *, nblk, ndb):
    _mixer_c_prologue(q_ref, k_ref, qm_scr, sel_scr, nblk=nblk)
    stages = functools.partial(_mixer_c_block, k_ref=k_ref, vt_ref=vt_ref, bias_ref=bias_ref, o_ref=o_ref,
                               qm_scr=qm_scr, sel_scr=sel_scr, s_scr=s_scr, m_scr=m_scr, p_scr=p_scr, ndb=ndb)

    def step(t):
        gens = []
        if t >= 1:
            gens += stages(t - 1)[1]
        if t < nblk:
            gens += stages(t)[0]
        _run_round_robin(*gens)

    def body(i, carry):
        for t in range(nblk + 1):
            pl.when(i == t)(functools.partial(step, t))
        return carry

    lax.fori_loop(0, nblk + 1, body, 0)


def _mixer_c(zc, vt, bias_c, batch, seq, ndb):
    blk = MOBA_BLOCK
    assert seq % blk == 0
    nblk = seq // blk
    npair = C_HEADS // HEAD_PAIR
    z = zc.reshape(batch, seq, 2 * C_WIDTH)
    o = pl.pallas_call(
        functools.partial(_mixer_c_kernel, nblk=nblk, ndb=ndb),
        grid=(batch, npair),
        in_specs=[pl.BlockSpec((1, seq, LANE), lambda b, hp: (b, 0, hp)),
                  pl.BlockSpec((1, seq, LANE), lambda b, hp: (b, 0, npair + hp)),
                  pl.BlockSpec((1, HEAD_PAIR * V_ROWS, seq), lambda b, hp: (b, hp, 0)),
                  pl.BlockSpec((HEAD_PAIR, ndb + 1, blk, blk), lambda b, hp: (hp, 0, 0, 0))],
        out_specs=pl.BlockSpec((1, seq, LANE), lambda b, hp: (b, 0, hp)),
        out_shape=jax.ShapeDtypeStruct((batch, seq, C_WIDTH), BF16),
        scratch_shapes=[pltpu.VMEM((HEAD_PAIR, seq, LANE), BF16), pltpu.VMEM((HEAD_PAIR, nblk, seq), F32),
                        pltpu.VMEM((2, HEAD_PAIR, seq, blk), F32), pltpu.VMEM((2, HEAD_PAIR, SUBLANES, blk), F32),
                        pltpu.VMEM((HEAD_PAIR, seq, blk), BF16)],
        compiler_params=_cparams(("arbitrary", "arbitrary")),
        name="mixer_c",
    )(z, z, vt, bias_c)
    return o.reshape(batch * seq, C_WIDTH)


def _merge_kernel(x_ref, o1_ref, o2_ref, o3_ref, l1_ref, l2_ref, l3_ref, ob_ref, oc_ref, zg_ref,
                  pa_ref, pb_ref, pc_ref, wo_ref, out_ref):
    d = D_MODEL
    halves = lambda ref: jnp.concatenate([ref[c] for c in range(ref.shape[0])], axis=-1)
    l1, l2, l3 = halves(l1_ref), halves(l2_ref), halves(l3_ref)
    m = jnp.maximum(jnp.maximum(l1, l2), l3)
    e1, e2, e3 = jnp.exp(l1 - m), jnp.exp(l2 - m), jnp.exp(l3 - m)
    o_a = (e1 * halves(o1_ref) + e2 * halves(o2_ref) + e3 * halves(o3_ref)) / (e1 + e2 + e3)

    def branch(o, p_ref, k):
        gate = jax.nn.sigmoid(zg_ref[:, k * d:(k + 1) * d].astype(F32))
        return gate * jnp.dot(o, p_ref[...], preferred_element_type=F32)

    merged = branch(o_a.astype(BF16), pa_ref, 0) + branch(ob_ref[...], pb_ref, 1) + branch(oc_ref[...], pc_ref, 2)
    out_ref[...] = x_ref[...] + jnp.dot(merged.astype(BF16), wo_ref[...], preferred_element_type=F32)


def _merge(x2d, oa, lse, ob, oc, zg, p_a, p_b, p_c, w_out, layer, tm=512):
    n = x2d.shape[0]
    row = lambda width: pl.BlockSpec((tm, width), lambda i: (i, 0))
    return pl.pallas_call(
        _merge_kernel,
        grid=(n // tm,),
        in_specs=[row(D_MODEL)] + [pl.BlockSpec((A_OUT // LANE, tm, LANE), lambda i: (0, i, 0))] * 6
                 + [row(LRU_WIDTH), row(C_WIDTH), row(3 * D_MODEL),
                  _layer_spec(p_a, layer), _layer_spec(p_b, layer), _layer_spec(p_c, layer),
                  _layer_spec(w_out, layer)],
        out_specs=row(D_MODEL),
        out_shape=jax.ShapeDtypeStruct((n, D_MODEL), F32),
        compiler_params=_cparams(("arbitrary",)),
        name="merge",
    )(x2d, *oa, *lse, ob, oc, zg, p_a, p_b, p_c, w_out)


def _rms(x, g):
    return x * lax.rsqrt(jnp.mean(x * x, axis=-1, keepdims=True) + EPS) * g


def _ffn_kernel(x_ref, g_ref, wgu_ref, wd_ref, gf_ref, out_ref, h_scr, act_scr, *, final_norm):
    x = x_ref[...]
    h_scr[...] = _rms(x, g_ref[...]).astype(BF16)
    for off, w in _col_chunks(0, FFN_HIDDEN, 256):
        gate = jnp.dot(h_scr[...], wgu_ref[:, off:off + w], preferred_element_type=F32)
        up = jnp.dot(h_scr[...], wgu_ref[:, FFN_HIDDEN + off:FFN_HIDDEN + off + w], preferred_element_type=F32)
        act_scr[:, off:off + w] = (gate * jax.nn.sigmoid(gate) * up).astype(BF16)
    y = x + jnp.dot(act_scr[...], wd_ref[...], preferred_element_type=F32)
    if final_norm:
        y = _rms(y, gf_ref[...])
    out_ref[...] = y


def _ffn(x2d, g_ffn, w_gu, w_down, layer, g_final, final_norm, tm=512):
    n = x2d.shape[0]
    row = pl.BlockSpec((tm, D_MODEL), lambda i: (i, 0))
    return pl.pallas_call(
        functools.partial(_ffn_kernel, final_norm=final_norm),
        grid=(n // tm,),
        in_specs=[row, _const_spec((1, D_MODEL)), _layer_spec(w_gu, layer), _layer_spec(w_down, layer),
                  _const_spec((1, D_MODEL))],
        out_specs=row,
        out_shape=jax.ShapeDtypeStruct((n, D_MODEL), F32),
        scratch_shapes=[pltpu.VMEM((tm, D_MODEL), BF16), pltpu.VMEM((tm, FFN_HIDDEN), BF16)],
        compiler_params=_cparams(("arbitrary",)),
        name="ffn",
    )(x2d, g_ffn.reshape(1, D_MODEL), w_gu, w_down, g_final.reshape(1, D_MODEL))


def kernel(x, rel_bias, g_mix, w_in, conv_w, conv_b, lru_wa, lru_ba, lru_wx, lru_bx, lru_lam,
           p_a, p_b, p_c, w_out, g_ffn, w_gu, w_down, g_final):
    batch, seq, d = x.shape
    assert d == D_MODEL and w_in.shape[-1] == IN_COLS
    depth = w_in.shape[0]
    nblk = seq // MOBA_BLOCK
    ndb = _moba_far_blocks(nblk)

    bias_a = _build_bias(rel_bias, _bucket_index_a(), A_HEADS, 0, A_HPG)
    bias_c = _build_bias(rel_bias, _bucket_index_c(ndb), C_HEADS, A_HEADS, None, LOG2E)

    w_in, p_a, p_b, p_c, w_out, w_gu, w_down = (
        t.astype(BF16) for t in (w_in, p_a, p_b, p_c, w_out, w_gu, w_down))

    x2d = x.reshape(batch * seq, d)
    for l in range(depth):
        *za, zb, zc, vt, zg = _in_proj(x2d, g_mix[l], w_in, l, batch, seq)
        oa, lse = zip(*[_mixer_a_group(za[g], bias_a, g, dil, batch, seq)
                        for g, (_, dil) in enumerate(A_GROUPS)])
        ob = _mixer_b(zb, conv_w[l], conv_b[l], lru_wa[l], lru_ba[l], lru_wx[l], lru_bx[l], lru_lam[l],
                      batch, seq)
        oc = _mixer_c(zc, vt, bias_c, batch, seq, ndb)
        x2d = _merge(x2d, oa, lse, ob, oc, zg, p_a, p_b, p_c, w_out, l)
        x2d = _ffn(x2d, g_ffn[l], w_gu, w_down, l, g_final, final_norm=(l == depth - 1))
    return x2d.reshape(batch, seq, d)
```
